```python
import jax
import jax.numpy as jnp
from jax import lax
import numpy as np

D_MODEL = 1024
BATCH = 4
SEQ = 8192
DEPTH = 4

HEAD_DIM = 64
ROT_DIM = HEAD_DIM // 4
ROPE_THETA = 500000.0
BLOCK = 128
NORM_EPS = 1e-5

A_Q_HEADS = 12
A_KV_HEADS = 3
A_GROUP = A_Q_HEADS // A_KV_HEADS
A_WINDOW = 128

B_PATTERNS = ((128, 1), (512, 4), (2048, 16))
B_N_PAT = len(B_PATTERNS)
B_HEADS = 4

A_Q_W = A_Q_HEADS * HEAD_DIM
A_KV_W = A_KV_HEADS * HEAD_DIM
B_W = B_N_PAT * B_HEADS * HEAD_DIM
IN_W = A_Q_W + 2 * A_KV_W + 3 * B_W
IN_SPLITS = (A_Q_W, A_Q_W + A_KV_W, A_Q_W + 2 * A_KV_W, A_Q_W + 2 * A_KV_W + B_W, A_Q_W + 2 * A_KV_W + 2 * B_W)
MIX_W = A_Q_W + B_HEADS * HEAD_DIM

RWKV_HEADS = D_MODEL // HEAD_DIM
DECAY_LORA = max(32, int(round(1.8 * D_MODEL ** 0.5 / 32)) * 32)
AAA_LORA = max(32, int(round(1.8 * D_MODEL ** 0.5 / 32)) * 32)
MV_LORA = max(32, int(round(1.3 * D_MODEL ** 0.5 / 32)) * 32)
GATE_LORA = max(32, int(round(0.6 * D_MODEL ** 0.8 / 32)) * 32)
LNX_EPS = 64e-5

D_FF = ((8 * D_MODEL + 3 * 256 - 1) // (3 * 256)) * 256

N_ATTN_LAYERS = (DEPTH + 1) // 2
N_RWKV_LAYERS = DEPTH // 2
MAX_POS_OFFSET = 4096

kernel_name = 'hybrid_swa_dilated_rwkv7_trunk'


def rms_norm(x, g):
    xf = x.astype(jnp.float32)
    y = xf * lax.rsqrt(jnp.mean(xf * xf, axis=-1, keepdims=True) + NORM_EPS) * g
    return y.astype(x.dtype)


def rope_partial(x, positions):
    half = ROT_DIM // 2
    inv_freq = jnp.power(ROPE_THETA, -2.0 * jnp.arange(half, dtype=jnp.float32) / ROT_DIM)
    ang = positions.astype(jnp.float32)[..., None] * inv_freq
    cos = jnp.cos(ang)[:, :, None, :]
    sin = jnp.sin(ang)[:, :, None, :]
    xf = x.astype(jnp.float32)
    x1, x2, xp = xf[..., :half], xf[..., half:ROT_DIM], xf[..., ROT_DIM:]
    out = jnp.concatenate([x1 * cos - x2 * sin, x2 * cos + x1 * sin, xp], axis=-1)
    return out.astype(x.dtype)


def banded_attention(q, k, v, max_dist, sink=None):
    n, t, hk, g, d = q.shape
    nb = -(-t // BLOCK)
    tp = nb * BLOCK
    pad = tp - t
    if pad:
        q = jnp.pad(q, ((0, 0), (0, pad), (0, 0), (0, 0), (0, 0)))
        k = jnp.pad(k, ((0, 0), (0, pad), (0, 0), (0, 0)))
        v = jnp.pad(v, ((0, 0), (0, pad), (0, 0), (0, 0)))
    qb = q.reshape(n, nb, BLOCK, hk, g, d).astype(jnp.float32)

    def with_prev(z):
        zb = z.reshape(n, nb, BLOCK, hk, d).astype(jnp.float32)
        prev = jnp.pad(zb[:, :-1], ((0, 0), (1, 0), (0, 0), (0, 0), (0, 0)))
        return jnp.concatenate([prev, zb], axis=2)

    kb, vb = with_prev(k), with_prev(v)
    s = jnp.einsum('nbqhgd,nbkhd->nbhgqk', qb, kb) * (d ** -0.5)
    blk = jnp.arange(nb)[:, None, None]
    qi = jnp.arange(BLOCK)[None, :, None]
    kj = jnp.arange(2 * BLOCK)[None, None, :]
    dist = BLOCK + qi - kj
    valid = (dist >= 0) & (dist <= max_dist) & (blk * BLOCK + kj - BLOCK >= 0)
    s = jnp.where(valid[None, :, None, None], s, -jnp.inf)
    m = jnp.max(s, axis=-1)
    if sink is not None:
        sk = sink.astype(jnp.float32).reshape(1, 1, hk, g, 1)
        m = jnp.maximum(m, sk)
    p = jnp.exp(s - m[..., None])
    den = jnp.sum(p, axis=-1)
    if sink is not None:
        den = den + jnp.exp(sk - m)
    o = jnp.einsum('nbhgqk,nbkhd->nbqhgd', p, vb) / jnp.moveaxis(den, -1, 2)[..., None]
    lse = jnp.moveaxis(m + jnp.log(den), -1, 2).reshape(n, tp, hk, g)[:, :t]
    o = o.reshape(n, tp, hk, g, d)[:, :t].astype(q.dtype)
    return o, lse


def dilated_attention(q, k, v, window, dil):
    bsz, s, h, d = q.shape
    length = s // dil

    def to_sub(z):
        return z.reshape(bsz, length, dil, h, d).transpose(0, 2, 1, 3, 4).reshape(bsz * dil, length, h, d)

    o, lse = banded_attention(to_sub(q)[:, :, :, None], to_sub(k), to_sub(v), window // dil)
    o = o[:, :, :, 0].reshape(bsz, dil, length, h, d).transpose(0, 2, 1, 3, 4).reshape(bsz, s, h, d)
    lse = lse[..., 0].reshape(bsz, dil, length, h).transpose(0, 2, 1, 3).reshape(bsz, s, h)
    return o, lse


def parallel_attention(h, positions, w_in, b_in, sinks, w_out):
    bsz, s, _ = h.shape
    proj = h @ w_in + b_in
    qa, ka, va, qb, kb, vb = jnp.split(proj, IN_SPLITS, axis=-1)
    qa = rope_partial(qa.reshape(bsz, s, A_Q_HEADS, HEAD_DIM), positions).reshape(bsz, s, A_KV_HEADS, A_GROUP, HEAD_DIM)
    ka = rope_partial(ka.reshape(bsz, s, A_KV_HEADS, HEAD_DIM), positions)
    va = va.reshape(bsz, s, A_KV_HEADS, HEAD_DIM)
    out_a, _ = banded_attention(qa, ka, va, A_WINDOW - 1, sinks)
    out_a = out_a.reshape(bsz, s, A_Q_W)
    nbh = B_N_PAT * B_HEADS
    qb = rope_partial(qb.reshape(bsz, s, nbh, HEAD_DIM), positions).reshape(bsz, s, B_N_PAT, B_HEADS, HEAD_DIM)
    kb = rope_partial(kb.reshape(bsz, s, nbh, HEAD_DIM), positions).reshape(bsz, s, B_N_PAT, B_HEADS, HEAD_DIM)
    vb = vb.reshape(bsz, s, B_N_PAT, B_HEADS, HEAD_DIM)
    outs, lses = [], []
    for gi, (win, dil) in enumerate(B_PATTERNS):
        o, l = dilated_attention(qb[:, :, gi], kb[:, :, gi], vb[:, :, gi], win, dil)
        outs.append(o.astype(jnp.float32))
        lses.append(l)
    wts = jax.nn.softmax(jnp.stack(lses), axis=0)
    out_b = jnp.sum(wts[..., None] * jnp.stack(outs), axis=0).astype(h.dtype).reshape(bsz, s, B_HEADS * HEAD_DIM)
    return jnp.concatenate([out_a, out_b], axis=-1) @ w_out


def wkv7_scan(r, decay, k, v, a_vec, b_vec):
    bsz, _, nh, n = r.shape
    xs = tuple(jnp.moveaxis(z, 1, 0) for z in (r, decay, k, v, a_vec, b_vec))

    def step(state, inp):
        r_t, w_t, k_t, v_t, a_t, b_t = inp
        sa = jnp.einsum('bhvk,bhk->bhv', state, a_t)
        state = state * w_t[:, :, None, :] + sa[..., None] * b_t[:, :, None, :] + v_t[..., None] * k_t[:, :, None, :]
        return state, jnp.einsum('bhvk,bhk->bhv', state, r_t)

    state0 = jnp.zeros((bsz, nh, n, n), jnp.float32)
    _, ys = lax.scan(step, state0, xs)
    return jnp.moveaxis(ys, 0, 1)


def rwkv_time_mix(h, v_first, v_lora, mu, w_rkv, w0, w1, w2, a0, a1, a2, g1, g2, k_k, k_a, r_k, lnx_w, lnx_b, w_o):
    bsz, s, dm = h.shape
    xx = jnp.pad(h[:, :-1], ((0, 0), (1, 0), (0, 0))) - h
    xr, xw, xk, xv, xa, xg = [h + xx * mu[i] for i in range(6)]
    r = xr @ w_rkv[0]
    k = xk @ w_rkv[1]
    v = xv @ w_rkv[2]
    w = -jax.nn.softplus(-(w0 + jnp.tanh(xw @ w1) @ w2)) - 0.5
    if v_lora is None:
        v_first = v
    else:
        v0, v1, v2 = v_lora
        v = v + (v_first - v) * jax.nn.sigmoid(v0 + (xv @ v1) @ v2)
    a = jax.nn.sigmoid(a0 + (xa @ a1) @ a2)
    g = jax.nn.sigmoid(xg @ g1) @ g2

    def heads(z):
        return z.reshape(bsz, s, RWKV_HEADS, HEAD_DIM).astype(jnp.float32)

    kk = heads(k * k_k)
    kk = kk / jnp.maximum(jnp.sqrt(jnp.sum(kk * kk, axis=-1, keepdims=True)), 1e-12)
    k = k * (1.0 + (a - 1.0) * k_a)
    rh, kh, vh, ah = heads(r), heads(k), heads(v), heads(a)
    decay = jnp.exp(-jnp.exp(heads(w)))
    y = wkv7_scan(rh, decay, kh, vh, -kk, kk * ah)
    mean = jnp.mean(y, axis=-1, keepdims=True)
    var = jnp.mean(jnp.square(y - mean), axis=-1, keepdims=True)
    y = ((y - mean) * lax.rsqrt(var + LNX_EPS)).reshape(bsz, s, dm) * lnx_w + lnx_b
    bonus = jnp.sum(rh * kh * r_k, axis=-1, keepdims=True) * vh
    y = y + bonus.reshape(bsz, s, dm)
    out = (y.astype(h.dtype) * g) @ w_o
    return out, v_first


def swiglu(h, w_gate, w_up, w_down):
    return (jax.nn.silu(h @ w_gate) * (h @ w_up)) @ w_down


def setup_inputs(seed: int = 0) -> dict:
    key = jax.random.key(seed)
    keys = list(jax.random.split(key, 40))

    def nrm(shape, scale):
        return scale * jax.random.normal(keys.pop(), shape, jnp.float32)

    def gain(shape):
        return 1.0 + nrm(shape, 0.02)

    def unif(shape, lo, hi):
        return jax.random.uniform(keys.pop(), shape, jnp.float32, lo, hi)

    na, nr, d = N_ATTN_LAYERS, N_RWKV_LAYERS, D_MODEL
    x = nrm((BATCH, SEQ, d), 1.0)
    start = jax.random.randint(keys.pop(), (BATCH, 1), 0, MAX_POS_OFFSET, dtype=jnp.int32)
    positions = start + jnp.arange(SEQ, dtype=jnp.int32)[None, :]
    return {
        'x': x,
        'positions': positions,
        'norm_mix': gain((DEPTH, d)),
        'norm_ffn': gain((DEPTH, d)),
        'norm_final': gain((d,)),
        'attn_w_in': nrm((na, d, IN_W), d ** -0.5),
        'attn_b_in': nrm((na, IN_W), 0.02),
        'attn_sinks': nrm((na, A_Q_HEADS), 0.5),
        'attn_w_out': nrm((na, MIX_W, d), MIX_W ** -0.5),
        'rwkv_mu': unif((nr, 6, d), 0.0, 1.0),
        'rwkv_w_rkv': nrm((nr, 3, d, d), d ** -0.5),
        'rwkv_w0': unif((nr, d), -6.0, -1.0),
        'rwkv_w1': nrm((nr, d, DECAY_LORA), d ** -0.5),
        'rwkv_w2': nrm((nr, DECAY_LORA, d), 0.1 * DECAY_LORA ** -0.5),
        'rwkv_a0': nrm((nr, d), 0.1),
        'rwkv_a1': nrm((nr, d, AAA_LORA), d ** -0.5),
        'rwkv_a2': nrm((nr, AAA_LORA, d), 0.1 * AAA_LORA ** -0.5),
        'rwkv_g1': nrm((nr, d, GATE_LORA), d ** -0.5),
        'rwkv_g2': nrm((nr, GATE_LORA, d), GATE_LORA ** -0.5),
        'rwkv_k_k': 0.85 + nrm((nr, d), 0.02),
        'rwkv_k_a': gain((nr, d)),
        'rwkv_r_k': nrm((nr, RWKV_HEADS, HEAD_DIM), 0.1),
        'rwkv_lnx_w': gain((nr, d)),
        'rwkv_lnx_b': nrm((nr, d), 0.02),
        'rwkv_w_o': nrm((nr, d, d), d ** -0.5),
        'rwkv_v0': nrm((nr - 1, d), 0.1),
        'rwkv_v1': nrm((nr - 1, d, MV_LORA), d ** -0.5),
        'rwkv_v2': nrm((nr - 1, MV_LORA, d), 0.1 * MV_LORA ** -0.5),
        'ffn_w_gate': nrm((DEPTH, d, D_FF), d ** -0.5),
        'ffn_w_up': nrm((DEPTH, d, D_FF), d ** -0.5),
        'ffn_w_down': nrm((DEPTH, D_FF, d), D_FF ** -0.5),
    }


def reference(x, positions, norm_mix, norm_ffn, norm_final, attn_w_in, attn_b_in, attn_sinks, attn_w_out,
              rwkv_mu, rwkv_w_rkv, rwkv_w0, rwkv_w1, rwkv_w2, rwkv_a0, rwkv_a1, rwkv_a2, rwkv_g1, rwkv_g2,
              rwkv_k_k, rwkv_k_a, rwkv_r_k, rwkv_lnx_w, rwkv_lnx_b, rwkv_w_o, rwkv_v0, rwkv_v1, rwkv_v2,
              ffn_w_gate, ffn_w_up, ffn_w_down):
    v_first = None
    for layer in range(DEPTH):
        h = rms_norm(x, norm_mix[layer])
        i = layer // 2
        if layer % 2 == 0:
            mix = parallel_attention(h, positions, attn_w_in[i], attn_b_in[i], attn_sinks[i], attn_w_out[i])
        else:
            v_lora = None if i == 0 else (rwkv_v0[i - 1], rwkv_v1[i - 1], rwkv_v2[i - 1])
            mix, v_first = rwkv_time_mix(h, v_first, v_lora, rwkv_mu[i], rwkv_w_rkv[i], rwkv_w0[i], rwkv_w1[i],
                                         rwkv_w2[i], rwkv_a0[i], rwkv_a1[i], rwkv_a2[i], rwkv_g1[i], rwkv_g2[i],
                                         rwkv_k_k[i], rwkv_k_a[i], rwkv_r_k[i], rwkv_lnx_w[i], rwkv_lnx_b[i],
                                         rwkv_w_o[i])
        x = x + mix
        h = rms_norm(x, norm_ffn[layer])
        x = x + swiglu(h, ffn_w_gate[layer], ffn_w_up[layer], ffn_w_down[layer])
    return rms_norm(x, norm_final)
```

```python
import functools

import jax
import jax.numpy as jnp
from jax import lax
from jax.experimental import pallas as pl
from jax.experimental.pallas import tpu as pltpu

F32 = jnp.float32
BF16 = jnp.bfloat16

HEAD_DIM = 64
ROT_DIM = HEAD_DIM // 4
ROT_HALF = ROT_DIM // 2
ROPE_THETA = 500000.0
BLOCK = 128
NORM_EPS = 1e-5
LNX_EPS = 64e-5

A_Q_HEADS = 12
A_KV_HEADS = 3
A_GROUP = A_Q_HEADS // A_KV_HEADS
A_WINDOW = 128
B_PATTERNS = ((128, 1), (512, 4), (2048, 16))
B_HEADS = 4

A_Q_W = A_Q_HEADS * HEAD_DIM
A_KV_W = A_KV_HEADS * HEAD_DIM
B_PW = B_HEADS * HEAD_DIM
B_W = len(B_PATTERNS) * B_PW
OFF_KA = A_Q_W
OFF_QB = A_Q_W + 2 * A_KV_W
OFF_KB = OFF_QB + B_W
OFF_VB = OFF_KB + B_W

LANES = 128
MXU_W = 256
HEADS_PER_MXU = MXU_W // HEAD_DIM
WKV_CHUNK = 64
NEG_BIG = -1e30
VMEM_LIMIT = 56 * 1024 * 1024


def _cparams(*sem):
    return pltpu.CompilerParams(dimension_semantics=sem, vmem_limit_bytes=VMEM_LIMIT)


def _rms(x, g):
    return x * lax.rsqrt(jnp.mean(x * x, axis=-1, keepdims=True) + NORM_EPS) * g


def _dot(a, b):
    return jnp.dot(a, b, preferred_element_type=F32)


def _dot_nt(a, b):
    return lax.dot_general(a, b, (((1,), (1,)), ((), ())), preferred_element_type=F32)


def _dot_tn(a, b):
    return lax.dot_general(a, b, (((0,), (0,)), ((), ())), preferred_element_type=F32)


def _sigmoid(z):
    return 1.0 / (1.0 + jnp.exp(-z))


def _head_mask(n):
    r = lax.broadcasted_iota(jnp.int32, (n, n), 0) // HEAD_DIM
    c = lax.broadcasted_iota(jnp.int32, (n, n), 1) // HEAD_DIM
    return r == c


def _head_sum(z, ones_bd):
    outs = []
    for j in range(z.shape[1] // MXU_W):
        zc = z[:, j * MXU_W:(j + 1) * MXU_W]
        hi = zc.astype(BF16)
        lo = (zc - hi.astype(F32)).astype(BF16)
        outs.append(_dot(hi, ones_bd) + _dot(lo, ones_bd))
    return jnp.concatenate(outs, axis=1)


def _rope_trig_kernel(pos_ref, invf_ref, cos_ref, sin_ref):
    ang = pos_ref[...].astype(F32)[None] * invf_ref[...]
    cos_ref[...] = jnp.cos(ang)
    sin_ref[...] = jnp.sin(ang)


def _rope_tables(positions):
    bsz, s = positions.shape
    rows = bsz * s // LANES
    inv_freq = jnp.power(ROPE_THETA, -2.0 * jnp.arange(ROT_HALF, dtype=F32) / ROT_DIM)
    invf = jnp.broadcast_to(inv_freq[:, None, None], (ROT_HALF, 1, LANES))
    cos, sin = pl.pallas_call(
        _rope_trig_kernel,
        out_shape=(jax.ShapeDtypeStruct((ROT_HALF, rows, LANES), F32),) * 2,
        name="rope_trig",
    )(positions.reshape(rows, LANES), invf)
    cos = cos.reshape(ROT_HALF, bsz, s).transpose(1, 2, 0)
    sin = sin.reshape(ROT_HALF, bsz, s).transpose(1, 2, 0)
    ones = jnp.ones((bsz, s, HEAD_DIM - ROT_DIM), F32)
    z8 = jnp.zeros((bsz, s, ROT_HALF), F32)
    zrest = jnp.zeros((bsz, s, HEAD_DIM - ROT_DIM), F32)
    c = jnp.concatenate([cos, cos, ones], axis=-1)
    sa = jnp.concatenate([-sin, z8, zrest], axis=-1)
    sb = jnp.concatenate([z8, sin, zrest], axis=-1)
    return tuple(jnp.tile(t, (1, 1, LANES // HEAD_DIM)) for t in (c, sa, sb))


def _attn_in_kernel(x_ref, g_ref, w_ref, b_ref, c_ref, sa_ref, sb_ref,
                    qa_ref, kva_ref, qb_ref, kb_ref, vb_ref):
    h = _rms(x_ref[...], g_ref[...]).astype(BF16)
    c = c_ref[...]
    sa = sa_ref[...]
    sb = sb_ref[...]

    def proj(lo, width):
        return _dot(h, w_ref[:, lo:lo + width]) + b_ref[:, lo:lo + width]

    def rope(z):
        return z * c + pltpu.roll(z, LANES - ROT_HALF, 1) * sa + pltpu.roll(z, ROT_HALF, 1) * sb

    def rope_all(z):
        return jnp.concatenate([rope(z[:, j * LANES:(j + 1) * LANES])
                                for j in range(z.shape[1] // LANES)], axis=1)

    scale = HEAD_DIM ** -0.5
    for j in range(A_Q_W // MXU_W):
        qa_ref[:, j * MXU_W:(j + 1) * MXU_W] = (rope_all(proj(j * MXU_W, MXU_W)) * scale).astype(BF16)
    z = proj(OFF_KA, 2 * A_KV_W)
    lane = lax.broadcasted_iota(jnp.int32, (z.shape[0], LANES), 1)
    mid = z[:, LANES:2 * LANES]
    kva_ref[...] = jnp.concatenate(
        [rope(z[:, :LANES]), jnp.where(lane < HEAD_DIM, rope(mid), mid), z[:, 2 * LANES:]],
        axis=1).astype(BF16)
    for j in range(B_W // MXU_W):
        sl = slice(j * MXU_W, (j + 1) * MXU_W)
        qb_ref[:, sl] = (rope_all(proj(OFF_QB + j * MXU_W, MXU_W)) * scale).astype(BF16)
        kb_ref[:, sl] = rope_all(proj(OFF_KB + j * MXU_W, MXU_W)).astype(BF16)
        vb_ref[:, sl] = proj(OFF_VB + j * MXU_W, MXU_W).astype(BF16)


def _attn_in(x, g, w_in, b_in, tables, tm=512):
    bsz, s, d = x.shape
    in_w = w_in.shape[1]
    tok = lambda w: pl.BlockSpec((None, tm, w), lambda b, i: (b, i, 0))
    full = lambda shape: pl.BlockSpec(shape, lambda b, i: (0,) * len(shape))
    widths = (A_Q_W, 2 * A_KV_W, B_W, B_W, B_W)
    return pl.pallas_call(
        _attn_in_kernel,
        grid=(bsz, s // tm),
        in_specs=[tok(d), full((1, d)), full((d, in_w)), full((1, in_w)),
                  tok(LANES), tok(LANES), tok(LANES)],
        out_specs=[tok(w) for w in widths],
        out_shape=[jax.ShapeDtypeStruct((bsz, s, w), BF16) for w in widths],
        compiler_params=_cparams("parallel", "parallel"),
        name="attn_in",
    )(x, g.reshape(1, d), w_in, b_in.reshape(1, in_w), *tables)


def _attn_a_kernel(sink_ref, q_ref, kvp_ref, kvc_ref, o_ref):
    i = pl.program_id(1)
    rows = A_GROUP * BLOCK
    qi = lax.broadcasted_iota(jnp.int32, (rows, 2 * BLOCK), 0) % BLOCK
    kj = lax.broadcasted_iota(jnp.int32, (rows, 2 * BLOCK), 1)
    dist = BLOCK + qi - kj
    valid = (dist >= 0) & (dist <= A_WINDOW - 1) & ((kj >= BLOCK) | (i > 0))
    rowg = lax.broadcasted_iota(jnp.int32, (rows, 1), 0) // BLOCK
    for hk in range(A_KV_HEADS):
        ksl = slice(hk * HEAD_DIM, (hk + 1) * HEAD_DIM)
        vsl = slice(A_KV_W + hk * HEAD_DIM, A_KV_W + (hk + 1) * HEAD_DIM)
        k = jnp.concatenate([kvp_ref[:, ksl], kvc_ref[:, ksl]], axis=0)
        v = jnp.concatenate([kvp_ref[:, vsl], kvc_ref[:, vsl]], axis=0)
        q = jnp.concatenate([q_ref[:, (hk * A_GROUP + g) * HEAD_DIM:(hk * A_GROUP + g + 1) * HEAD_DIM]
                             for g in range(A_GROUP)], axis=0)
        sk = jnp.zeros((rows, 1), F32)
        for g in range(A_GROUP):
            sk = jnp.where(rowg == g, sink_ref[hk * A_GROUP + g], sk)
        sc = jnp.where(valid, _dot_nt(q, k), NEG_BIG)
        m = jnp.maximum(jnp.max(sc, axis=-1, keepdims=True), sk)
        p = jnp.exp(sc - m)
        den = jnp.sum(p, axis=-1, keepdims=True) + jnp.exp(sk - m)
        o = _dot(p.astype(BF16), v) / den
        for g in range(A_GROUP):
            hq = hk * A_GROUP + g
            o_ref[:, hq * HEAD_DIM:(hq + 1) * HEAD_DIM] = o[g * BLOCK:(g + 1) * BLOCK].astype(BF16)


def _attn_a(qa, kva, sinks):
    bsz, s, _ = qa.shape
    return pl.pallas_call(
        _attn_a_kernel,
        grid=(bsz, s // BLOCK),
        in_specs=[pl.BlockSpec(memory_space=pltpu.SMEM),
                  pl.BlockSpec((None, BLOCK, A_Q_W), lambda b, i: (b, i, 0)),
                  pl.BlockSpec((None, BLOCK, 2 * A_KV_W), lambda b, i: (b, jnp.maximum(i - 1, 0), 0)),
                  pl.BlockSpec((None, BLOCK, 2 * A_KV_W), lambda b, i: (b, i, 0))],
        out_specs=pl.BlockSpec((None, BLOCK, A_Q_W), lambda b, i: (b, i, 0)),
        out_shape=jax.ShapeDtypeStruct((bsz, s, A_Q_W), BF16),
        compiler_params=_cparams("parallel", "parallel"),
        name="attn_a",
    )(sinks, qa, kva, kva)


def _attn_b_kernel(q_ref, kp_ref, kc_ref, vp_ref, vc_ref, o_ref, l_ref):
    i = pl.program_id(2)
    qi = lax.broadcasted_iota(jnp.int32, (BLOCK, 2 * BLOCK), 0)
    kj = lax.broadcasted_iota(jnp.int32, (BLOCK, 2 * BLOCK), 1)
    dist = BLOCK + qi - kj
    valid = (dist >= 0) & (dist <= BLOCK) & ((kj >= BLOCK) | (i > 0))
    for h in range(B_HEADS):
        sl = slice(h * HEAD_DIM, (h + 1) * HEAD_DIM)
        k = jnp.concatenate([kp_ref[:, sl], kc_ref[:, sl]], axis=0)
        v = jnp.concatenate([vp_ref[:, sl], vc_ref[:, sl]], axis=0)
        sc = jnp.where(valid, _dot_nt(q_ref[:, sl], k), NEG_BIG)
        m = jnp.max(sc, axis=-1, keepdims=True)
        p = jnp.exp(sc - m)
        den = jnp.sum(p, axis=-1, keepdims=True)
        o_ref[:, sl] = _dot(p.astype(BF16), v) / den
        l_ref[:, sl] = jnp.broadcast_to(m + jnp.log(den), (BLOCK, HEAD_DIM))


def _attn_b(qb, kb, vb, pat, dil):
    bsz, s, w = qb.shape
    npat = w // B_PW
    length = s // dil
    view = lambda z: z.reshape(bsz, length, dil * w)
    cur = pl.BlockSpec((None, BLOCK, B_PW), lambda b, r, i: (b, i, r * npat + pat))
    prev = pl.BlockSpec((None, BLOCK, B_PW), lambda b, r, i: (b, jnp.maximum(i - 1, 0), r * npat + pat))
    out = pl.BlockSpec((None, BLOCK, B_PW), lambda b, r, i: (b, i, r))
    o, lse = pl.pallas_call(
        _attn_b_kernel,
        grid=(bsz, dil, length // BLOCK),
        in_specs=[cur, prev, cur, prev, cur],
        out_specs=[out, out],
        out_shape=[jax.ShapeDtypeStruct((bsz, length, dil * B_PW), F32)] * 2,
        compiler_params=_cparams("parallel", "parallel", "parallel"),
        name=f"attn_b{pat}",
    )(view(qb), view(kb), view(kb), view(vb), view(vb))
    return o.reshape(bsz, s, B_PW), lse.reshape(bsz, s, B_PW)


def _attn_out_kernel(x_ref, oa_ref, o0_ref, o1_ref, o2_ref, l0_ref, l1_ref, l2_ref, w_ref, out_ref):
    l0, l1, l2 = l0_ref[...], l1_ref[...], l2_ref[...]
    m = jnp.maximum(jnp.maximum(l0, l1), l2)
    e0, e1, e2 = jnp.exp(l0 - m), jnp.exp(l1 - m), jnp.exp(l2 - m)
    ob = (e0 * o0_ref[...] + e1 * o1_ref[...] + e2 * o2_ref[...]) / (e0 + e1 + e2)
    mix = _dot(oa_ref[...], w_ref[:A_Q_W, :]) + _dot(ob.astype(BF16), w_ref[A_Q_W:, :])
    out_ref[...] = x_ref[...] + mix


def _attn_out(x, oa, obs, lses, w_out, tm=512):
    bsz, s, d = x.shape
    tok = lambda w: pl.BlockSpec((None, tm, w), lambda b, i: (b, i, 0))
    return pl.pallas_call(
        _attn_out_kernel,
        grid=(bsz, s // tm),
        in_specs=[tok(d), tok(A_Q_W)] + [tok(B_PW)] * 6
                 + [pl.BlockSpec(w_out.shape, lambda b, i: (0, 0))],
        out_specs=tok(d),
        out_shape=jax.ShapeDtypeStruct((bsz, s, d), F32),
        compiler_params=_cparams("parallel", "parallel"),
        name="attn_out",
    )(x, oa, *obs, *lses, w_out)


def _ffn_kernel(final_norm, ff_chunk, x_ref, g_ref, wg_ref, wu_ref, wd_ref, gf_ref, out_ref, acc_ref):
    x = x_ref[...]
    h = _rms(x, g_ref[...]).astype(BF16)
    d_ff = wg_ref.shape[1]
    for c in range(d_ff // ff_chunk):
        sl = slice(c * ff_chunk, (c + 1) * ff_chunk)
        gate = _dot(h, wg_ref[:, sl])
        up = _dot(h, wu_ref[:, sl])
        act = (gate * _sigmoid(gate) * up).astype(BF16)
        contrib = _dot(act, wd_ref[sl, :])
        if c == 0:
            acc_ref[...] = x + contrib
        else:
            acc_ref[...] += contrib
    y = acc_ref[...]
    if final_norm:
        y = _rms(y, gf_ref[...])
    out_ref[...] = y


def _ffn(x, g, wg, wu, wd, g_final, final_norm, tm=512, ff_chunk=256):
    bsz, s, d = x.shape
    d_ff = wg.shape[1]
    tok = pl.BlockSpec((None, tm, d), lambda b, i: (b, i, 0))
    full = lambda shape: pl.BlockSpec(shape, lambda b, i: (0,) * len(shape))
    return pl.pallas_call(
        functools.partial(_ffn_kernel, final_norm, ff_chunk),
        grid=(bsz, s // tm),
        in_specs=[tok, full((1, d)), full((d, d_ff)), full((d, d_ff)), full((d_ff, d)), full((1, d))],
        out_specs=tok,
        out_shape=jax.ShapeDtypeStruct((bsz, s, d), F32),
        scratch_shapes=[pltpu.VMEM((tm, d), F32)],
        compiler_params=_cparams("parallel", "parallel"),
        name="ffn",
    )(x, g.reshape(1, d), wg, wu, wd, g_final.reshape(1, d))


def _rwkv_in_kernel(has_vlora, *refs):
    (x_ref, gn_ref, mu_ref, wrkv_ref, w0_ref, w1_ref, w2_ref, a0_ref, a1_ref, a2_ref,
     g1_ref, g2_ref, kk_ref, ka_ref) = refs[:14]
    refs = refs[14:]
    if has_vlora:
        v0_ref, v1_ref, v2_ref, vf_ref = refs[:4]
        refs = refs[4:]
    r_o, lw_o, k_o, v_o, kk_o, ab_o, g_o, hbuf = refs
    tm = x_ref.shape[0]

    h = _rms(x_ref[...], gn_ref[...])

    @pl.when(pl.program_id(1) == 0)
    def _():
        hbuf[0:8, :] = jnp.zeros((8, hbuf.shape[1]), F32)

    hbuf[8:tm + 8, :] = h
    xx = hbuf[7:tm + 7, :] - h
    hbuf[0:8, :] = hbuf[tm:tm + 8, :]

    def mixed(i):
        return (h + xx * mu_ref[i:i + 1, :]).astype(BF16)

    def lora(z, wa_ref, wb_ref, act=None):
        mid = _dot(z, wa_ref[...])
        if act is not None:
            mid = act(mid)
        return _dot(mid.astype(BF16), wb_ref[...])

    r = _dot(mixed(0), wrkv_ref[0])
    zw = w0_ref[...] + lora(mixed(1), w1_ref, w2_ref, jnp.tanh)
    softplus = jnp.maximum(-zw, 0.0) + jnp.log(1.0 + jnp.exp(-jnp.abs(zw)))
    lw = -jnp.exp(-softplus - 0.5)
    k = _dot(mixed(2), wrkv_ref[1])
    xv = mixed(3)
    v = _dot(xv, wrkv_ref[2])
    if has_vlora:
        v = v + (vf_ref[...] - v) * _sigmoid(v0_ref[...] + lora(xv, v1_ref, v2_ref))
    a = _sigmoid(a0_ref[...] + lora(mixed(4), a1_ref, a2_ref))
    g = lora(mixed(5), g1_ref, g2_ref, _sigmoid)

    ones_bd = jnp.where(_head_mask(MXU_W), 1.0, 0.0).astype(BF16)
    kk = k * kk_ref[...]
    kk = kk / jnp.maximum(jnp.sqrt(_head_sum(kk * kk, ones_bd)), 1e-12)
    k = k * (1.0 + (a - 1.0) * ka_ref[...])

    r_o[...] = r
    lw_o[...] = lw
    k_o[...] = k
    v_o[...] = v
    kk_o[...] = kk
    ab_o[...] = kk * a
    g_o[...] = g


def _rwkv_in(x, gn, p, v_lora, v_first, tm=256):
    bsz, s, d = x.shape
    tok = pl.BlockSpec((None, tm, d), lambda b, i: (b, i, 0))
    full = lambda a: pl.BlockSpec(a.shape, lambda b, i: (0,) * a.ndim)
    row = lambda a: a.reshape(1, d)
    args = [x, row(gn), p["mu"], p["w_rkv"], row(p["w0"]), p["w1"], p["w2"], row(p["a0"]), p["a1"], p["a2"],
            p["g1"], p["g2"], row(p["k_k"]), row(p["k_a"])]
    specs = [tok] + [full(a) for a in args[1:]]
    if v_lora is not None:
        v0, v1, v2 = v_lora
        extra = [row(v0), v1, v2]
        args += extra + [v_first]
        specs += [full(a) for a in extra] + [tok]
    return pl.pallas_call(
        functools.partial(_rwkv_in_kernel, v_lora is not None),
        grid=(bsz, s // tm),
        in_specs=specs,
        out_specs=[tok] * 7,
        out_shape=[jax.ShapeDtypeStruct((bsz, s, d), F32)] * 7,
        scratch_shapes=[pltpu.VMEM((tm + 8, d), F32)],
        compiler_params=_cparams("parallel", "arbitrary"),
        name="rwkv_in",
    )(*args)


def _wkv_kernel(r_ref, lw_ref, k_ref, v_ref, kk_ref, ab_ref, y_ref, s_ref):
    cs = WKV_CHUNK

    @pl.when(pl.program_id(1) == 0)
    def _():
        s_ref[...] = jnp.zeros(s_ref.shape, F32)

    tri = jnp.where(lax.broadcasted_iota(jnp.int32, (cs, cs), 0) >= lax.broadcasted_iota(jnp.int32, (cs, cs), 1),
                    1.0, 0.0).astype(BF16)
    lw = lw_ref[...]
    hi = lw.astype(BF16)
    rem = lw - hi.astype(F32)
    mid = rem.astype(BF16)
    lo = (rem - mid.astype(F32)).astype(BF16)
    cum = _dot(tri, hi) + _dot(tri, mid) + _dot(tri, lo)
    total = cum[cs - 1:cs, :]
    e_neg = jnp.exp(-cum)
    e_tot = jnp.exp(total - cum)
    kk, ab, kx = kk_ref[...], ab_ref[...], k_ref[...]
    at = (-kk * jnp.exp(cum - lw)).astype(BF16)
    rt = (r_ref[...] * jnp.exp(cum)).astype(BF16)
    bt = (ab * e_neg).astype(BF16)
    kt = (kx * e_neg).astype(BF16)
    bh = (ab * e_tot).astype(BF16)
    kh = (kx * e_tot).astype(BF16)
    vb = v_ref[...].astype(BF16)
    decay_c = jnp.exp(total)

    bd_mask = _head_mask(MXU_W)

    def bd(z):
        return jnp.where(bd_mask, jnp.concatenate([z.astype(BF16)] * HEADS_PER_MXU, axis=0), 0.0)

    trow = lax.broadcasted_iota(jnp.int32, (cs, MXU_W), 0)
    tcol = lax.broadcasted_iota(jnp.int32, (cs, MXU_W), 1) % HEAD_DIM
    strict = trow > tcol
    incl = trow >= tcol
    eye = jnp.where(trow == tcol, 1.0, 0.0)

    for g in range(s_ref.shape[1] // MXU_W):
        sl = slice(g * MXU_W, (g + 1) * MXU_W)
        ar = jnp.concatenate([at[:, sl], rt[:, sl]], axis=0)
        g1 = _dot_nt(ar, bd(bt[:, sl]))
        g2 = _dot_nt(ar, bd(kt[:, sl]))
        a_ab = jnp.where(strict, g1[:cs], 0.0)
        a_rb = jnp.where(incl, g1[cs:], 0.0)
        a_ak = jnp.where(strict, g2[:cs], 0.0)
        a_rk = jnp.where(incl, g2[cs:], 0.0)
        x = eye + jnp.where((trow - tcol == 1) & (trow % 2 == 1), a_ab, 0.0)
        size = 2
        while size < cs:
            lower_left = (trow // size - tcol // size == 1) & ((trow // size) % 2 == 1)
            n_l = jnp.where(lower_left, a_ab, 0.0)
            p = _dot(n_l.astype(BF16), bd(x))
            x = x + _dot(x.astype(BF16), bd(p))
            size *= 2
        s4 = s_ref[:, sl]
        g3 = _dot_nt(ar, bd(s4))
        g4 = _dot(jnp.concatenate([a_ak, a_rk], axis=0).astype(BF16), bd(vb[:, sl]))
        u = _dot(x.astype(BF16), bd(g3[:cs] + g4[:cs]))
        y_ref[:, sl] = g3[cs:] + g4[cs:] + _dot(a_rb.astype(BF16), bd(u))
        uv = jnp.concatenate([u.astype(BF16), vb[:, sl]], axis=0)
        bk = jnp.concatenate([bh[:, sl], kh[:, sl]], axis=0)
        full = jnp.where(bd_mask, _dot_tn(uv, bk), 0.0)
        upd = full[0:HEAD_DIM]
        for hh in range(1, HEADS_PER_MXU):
            upd = upd + full[hh * HEAD_DIM:(hh + 1) * HEAD_DIM]
        s_ref[:, sl] = s4 * decay_c[:, sl] + upd


def _wkv(r, lw, k, v, kk, ab):
    bsz, s, d = r.shape
    blk = pl.BlockSpec((None, WKV_CHUNK, d), lambda b, c: (b, c, 0))
    return pl.pallas_call(
        _wkv_kernel,
        grid=(bsz, s // WKV_CHUNK),
        in_specs=[blk] * 6,
        out_specs=blk,
        out_shape=jax.ShapeDtypeStruct((bsz, s, d), F32),
        scratch_shapes=[pltpu.VMEM((HEAD_DIM, d), F32)],
        compiler_params=_cparams("parallel", "arbitrary"),
        name="wkv7",
    )(r, lw, k, v, kk, ab)


def _rwkv_out_kernel(x_ref, y_ref, r_ref, k_ref, v_ref, g_ref, lnw_ref, lnb_ref, rk_ref, wo_ref, out_ref):
    ones_bd = jnp.where(_head_mask(MXU_W), 1.0, 0.0).astype(BF16)
    y = y_ref[...]
    dev = y - _head_sum(y, ones_bd) * (1.0 / HEAD_DIM)
    var = _head_sum(dev * dev, ones_bd) * (1.0 / HEAD_DIM)
    yn = dev * lax.rsqrt(var + LNX_EPS) * lnw_ref[...] + lnb_ref[...]
    bonus = _head_sum(r_ref[...] * k_ref[...] * rk_ref[...], ones_bd) * v_ref[...]
    z = ((yn + bonus) * g_ref[...]).astype(BF16)
    out_ref[...] = x_ref[...] + _dot(z, wo_ref[...])


def _rwkv_out(x, y, r, k, v, g, lnw, lnb, rk, wo, tm=512):
    bsz, s, d = x.shape
    tok = pl.BlockSpec((None, tm, d), lambda b, i: (b, i, 0))
    row = pl.BlockSpec((1, d), lambda b, i: (0, 0))
    return pl.pallas_call(
        _rwkv_out_kernel,
        grid=(bsz, s // tm),
        in_specs=[tok] * 6 + [row] * 3 + [pl.BlockSpec((d, d), lambda b, i: (0, 0))],
        out_specs=tok,
        out_shape=jax.ShapeDtypeStruct((bsz, s, d), F32),
        compiler_params=_cparams("parallel", "parallel"),
        name="rwkv_out",
    )(x, y, r, k, v, g, lnw.reshape(1, d), lnb.reshape(1, d), rk.reshape(1, d), wo)


def kernel(x, positions, norm_mix, norm_ffn, norm_final, attn_w_in, attn_b_in, attn_sinks, attn_w_out, rwkv_mu, rwkv_w_rkv, rwkv_w0, rwkv_w1, rwkv_w2, rwkv_a0, rwkv_a1, rwkv_a2, rwkv_g1, rwkv_g2, rwkv_k_k, rwkv_k_a, rwkv_r_k, rwkv_lnx_w, rwkv_lnx_b, rwkv_w_o, rwkv_v0, rwkv_v1, rwkv_v2, ffn_w_gate, ffn_w_up, ffn_w_down):
    depth = norm_mix.shape[0]
    bf = lambda a: a.astype(BF16)
    tables = _rope_tables(positions)
    v_first = None
    for layer in range(depth):
        i = layer // 2
        if layer % 2 == 0:
            qa, kva, qb, kb, vb = _attn_in(x, norm_mix[layer], bf(attn_w_in[i]), attn_b_in[i], tables)
            oa = _attn_a(qa, kva, attn_sinks[i])
            obs, lses = zip(*[_attn_b(qb, kb, vb, pat, dil) for pat, (_, dil) in enumerate(B_PATTERNS)])
            x = _attn_out(x, oa, obs, lses, bf(attn_w_out[i]))
        else:
            p = dict(mu=rwkv_mu[i], w_rkv=bf(rwkv_w_rkv[i]), w0=rwkv_w0[i], w1=bf(rwkv_w1[i]), w2=bf(rwkv_w2[i]),
                     a0=rwkv_a0[i], a1=bf(rwkv_a1[i]), a2=bf(rwkv_a2[i]), g1=bf(rwkv_g1[i]), g2=bf(rwkv_g2[i]),
                     k_k=rwkv_k_k[i], k_a=rwkv_k_a[i])
            v_lora = None if i == 0 else (rwkv_v0[i - 1], bf(rwkv_v1[i - 1]), bf(rwkv_v2[i - 1]))
            r, lw, k, v, kk, ab, g = _rwkv_in(x, norm_mix[layer], p, v_lora, v_first)
            if i == 0:
                v_first = v
            y = _wkv(r, lw, k, v, kk, ab)
            x = _rwkv_out(x, y, r, k, v, g, rwkv_lnx_w[i], rwkv_lnx_b[i], rwkv_r_k[i], bf(rwkv_w_o[i]))
        x = _ffn(x, norm_ffn[layer], bf(ffn_w_gate[layer]), bf(ffn_w_up[layer]), bf(ffn_w_down[layer]),
                 norm_final, final_norm=(layer == depth - 1))
    return x
```

```python
import functools

import jax
import jax.numpy as jnp
from jax import lax
from jax.experimental import pallas as pl
from jax.experimental.pallas import tpu as pltpu

F32 = jnp.float32
BF16 = jnp.bfloat16

HEAD_DIM = 64
ROT_DIM = HEAD_DIM // 4
ROT_HALF = ROT_DIM // 2
ROPE_THETA = 500000.0
BLOCK = 128
NORM_EPS = 1e-5
LNX_EPS = 64e-5

A_Q_HEADS = 12
A_KV_HEADS = 3
A_GROUP = A_Q_HEADS // A_KV_HEADS
A_WINDOW = 128
B_PATTERNS = ((128, 1), (512, 4), (2048, 16))
B_HEADS = 4

A_Q_W = A_Q_HEADS * HEAD_DIM
A_KV_W = A_KV_HEADS * HEAD_DIM
B_PW = B_HEADS * HEAD_DIM
B_W = len(B_PATTERNS) * B_PW
KV_DUP_W = 2 * A_KV_W
OFF_KA = A_Q_W
OFF_VA = OFF_KA + KV_DUP_W
OFF_QB = OFF_VA + KV_DUP_W
OFF_KB = OFF_QB + B_W
OFF_VB = OFF_KB + B_W

LANES = 128
MXU_W = 256
WKV_CHUNK = 64
DECAY_SCALE = 0.6065306597126334
NEG_BIG = -1e30
VMEM_LIMIT = 56 * 1024 * 1024


def _cparams(*sem):
    return pltpu.CompilerParams(dimension_semantics=sem, vmem_limit_bytes=VMEM_LIMIT)


def _dot(a, b):
    return jnp.dot(a, b, preferred_element_type=F32)


def _dot_nt(a, b):
    return lax.dot_general(a, b, (((1,), (1,)), ((), ())), preferred_element_type=F32)


def _dot_tn(a, b):
    return lax.dot_general(a, b, (((0,), (0,)), ((), ())), preferred_element_type=F32)


def _row_sum(z):
    part = z[:, :LANES]
    for j in range(1, z.shape[1] // LANES):
        part = part + z[:, j * LANES:(j + 1) * LANES]
    ones = jnp.ones((LANES, LANES), BF16)
    hi = part.astype(BF16)
    lo = (part - hi.astype(F32)).astype(BF16)
    return _dot(hi, ones) + _dot(lo, ones)


def _rms(x, g):
    inv = lax.rsqrt(_row_sum(x * x) * (1.0 / x.shape[1]) + NORM_EPS)
    return x * jnp.concatenate([inv] * (x.shape[1] // LANES), axis=1) * g


def _sigmoid(z):
    return 1.0 / (1.0 + jnp.exp(-z))


def _head_mask(n):
    r = lax.broadcasted_iota(jnp.int32, (n, n), 0) // HEAD_DIM
    c = lax.broadcasted_iota(jnp.int32, (n, n), 1) // HEAD_DIM
    return r == c


def _head_sum(z, ones_bd):
    outs = []
    for j in range(z.shape[1] // MXU_W):
        zc = z[:, j * MXU_W:(j + 1) * MXU_W]
        hi = zc.astype(BF16)
        lo = (zc - hi.astype(F32)).astype(BF16)
        outs.append(_dot(hi, ones_bd) + _dot(lo, ones_bd))
    return jnp.concatenate(outs, axis=1)


def _rope_trig_kernel(pos_ref, invf_ref, cos_ref, sin_ref):
    ang = pos_ref[...].astype(F32)[None] * invf_ref[...]
    cos_ref[...] = jnp.cos(ang)
    sin_ref[...] = jnp.sin(ang)


def _rope_tables(positions):
    bsz, s = positions.shape
    rows = bsz * s // LANES
    inv_freq = jnp.power(ROPE_THETA, -2.0 * jnp.arange(ROT_HALF, dtype=F32) / ROT_DIM)
    invf = jnp.broadcast_to(inv_freq[:, None, None], (ROT_HALF, 1, LANES))
    cos, sin = pl.pallas_call(
        _rope_trig_kernel,
        out_shape=(jax.ShapeDtypeStruct((ROT_HALF, rows, LANES), F32),) * 2,
        name="rope_trig",
    )(positions.reshape(rows, LANES), invf)
    cos = cos.reshape(ROT_HALF, bsz, s).transpose(1, 2, 0)
    sin = sin.reshape(ROT_HALF, bsz, s).transpose(1, 2, 0)
    ones = jnp.ones((bsz, s, HEAD_DIM - ROT_DIM), F32)
    z8 = jnp.zeros((bsz, s, ROT_HALF), F32)
    zrest = jnp.zeros((bsz, s, HEAD_DIM - ROT_DIM), F32)
    c = jnp.concatenate([cos, cos, ones], axis=-1)
    sa = jnp.concatenate([-sin, z8, zrest], axis=-1)
    sb = jnp.concatenate([z8, sin, zrest], axis=-1)
    return tuple(jnp.tile(t, (1, 1, LANES // HEAD_DIM)) for t in (c, sa, sb))


def _residue_major(ref, dil):
    n = ref.shape[0] // dil
    return jnp.concatenate([ref[pl.ds(r, n, stride=dil), :] for r in range(dil)], axis=0)


def _attn_in_kernel(x_ref, g_ref, w_ref, b_ref, c_ref, sa_ref, sb_ref, qa_ref, kva_ref, *rest):
    b_refs, hbuf = rest[:-1], rest[-1]
    hf = _rms(x_ref[...], g_ref[...])
    tm, d = hf.shape
    for j in range(d // LANES):
        hbuf[j] = hf[:, j * LANES:(j + 1) * LANES]
    scale = HEAD_DIM ** -0.5

    def proj(h, lo, width):
        return _dot(h, w_ref[:, lo:lo + width]) + b_ref[:, lo:lo + width]

    def rope(z, tabs):
        c, sa, sb = tabs
        return z * c + pltpu.roll(z, LANES - ROT_HALF, 1) * sa + pltpu.roll(z, ROT_HALF, 1) * sb

    def rope_all(z, tabs):
        return jnp.concatenate([rope(z[:, j * LANES:(j + 1) * LANES], tabs)
                                for j in range(z.shape[1] // LANES)], axis=1)

    h = hf.astype(BF16)
    tabs = (c_ref[...], sa_ref[...], sb_ref[...])
    for j in range(A_Q_W // MXU_W):
        qa_ref[:, j * MXU_W:(j + 1) * MXU_W] = (rope_all(proj(h, j * MXU_W, MXU_W), tabs) * scale).astype(BF16)
    kva_ref[:, :KV_DUP_W] = rope_all(proj(h, OFF_KA, KV_DUP_W), tabs).astype(BF16)
    kva_ref[:, KV_DUP_W:] = proj(h, OFF_VA, KV_DUP_W).astype(BF16)
    for pat, (_, dil) in enumerate(B_PATTERNS):
        q_ref, k_ref, v_ref = b_refs[3 * pat:3 * pat + 3]
        if dil > 1:
            h = jnp.concatenate([_residue_major(hbuf.at[j], dil) for j in range(d // LANES)],
                                axis=1).astype(BF16)
            tabs = tuple(_residue_major(t, dil) for t in (c_ref, sa_ref, sb_ref))
        q = (rope_all(proj(h, OFF_QB + pat * B_PW, B_PW), tabs) * scale).astype(BF16)
        k = rope_all(proj(h, OFF_KB + pat * B_PW, B_PW), tabs).astype(BF16)
        v = proj(h, OFF_VB + pat * B_PW, B_PW).astype(BF16)
        n = tm // dil
        for r in range(dil):
            q_ref[:, r * B_PW:(r + 1) * B_PW] = q[r * n:(r + 1) * n]
            k_ref[:, r * B_PW:(r + 1) * B_PW] = k[r * n:(r + 1) * n]
            v_ref[:, r * B_PW:(r + 1) * B_PW] = v[r * n:(r + 1) * n]


def _widen_in_proj(w):
    head = lambda off, h: w[..., off + h * HEAD_DIM:off + (h + 1) * HEAD_DIM]
    dup = lambda off: [head(off, h) for h in range(A_KV_HEADS) for _ in range(2)]
    return jnp.concatenate([w[..., :A_Q_W]] + dup(A_Q_W) + dup(A_Q_W + A_KV_W) + [w[..., A_Q_W + 2 * A_KV_W:]], axis=-1)


def _attn_in(x, g, w_in, b_in, tables, tm=512):
    bsz, s, d = x.shape
    in_w = w_in.shape[1]
    tok = lambda w: pl.BlockSpec((None, tm, w), lambda b, i: (b, i, 0))
    full = lambda shape: pl.BlockSpec(shape, lambda b, i: (0,) * len(shape))
    out_specs = [tok(A_Q_W), tok(2 * KV_DUP_W)]
    out_shape = [jax.ShapeDtypeStruct((bsz, s, A_Q_W), BF16), jax.ShapeDtypeStruct((bsz, s, 2 * KV_DUP_W), BF16)]
    for _, dil in B_PATTERNS:
        out_specs += [pl.BlockSpec((None, tm // dil, dil * B_PW), lambda b, i: (b, i, 0))] * 3
        out_shape += [jax.ShapeDtypeStruct((bsz, s // dil, dil * B_PW), BF16)] * 3
    outs = pl.pallas_call(
        _attn_in_kernel,
        grid=(bsz, s // tm),
        in_specs=[tok(d), full((1, d)), full((d, in_w)), full((1, in_w)),
                  tok(LANES), tok(LANES), tok(LANES)],
        out_specs=out_specs,
        out_shape=out_shape,
        scratch_shapes=[pltpu.VMEM((d // LANES, tm, LANES), F32)],
        compiler_params=_cparams("parallel", "parallel"),
        name="attn_in",
    )(x, g.reshape(1, d), w_in, b_in.reshape(1, in_w), *tables)
    return outs[0], outs[1], [outs[2 + 3 * p:5 + 3 * p] for p in range(len(B_PATTERNS))]


def _band_bias(rows, max_dist, first_tile):
    qi = lax.broadcasted_iota(jnp.int32, (rows, 2 * BLOCK), 0) % BLOCK
    kj = lax.broadcasted_iota(jnp.int32, (rows, 2 * BLOCK), 1)
    dist = BLOCK + qi - kj
    band = (dist >= 0) & (dist <= max_dist)
    inner = jnp.where(band, 0.0, NEG_BIG)
    return jnp.where(band & ((kj >= BLOCK) | jnp.logical_not(first_tile)), 0.0, NEG_BIG), inner


def _window_rows(prev_ref, cur_ref, jb, lanes):
    if jb == 0:
        return jnp.concatenate([prev_ref[:, lanes], cur_ref[0:BLOCK, lanes]], axis=0)
    return cur_ref[(jb - 1) * BLOCK:(jb + 1) * BLOCK, lanes]


def _attn_a_kernel(sink_ref, q_ref, kvp_ref, kvc_ref, o_ref):
    nb = q_ref.shape[0] // BLOCK
    rows = A_GROUP * BLOCK
    bias_first, bias_inner = _band_bias(rows, A_WINDOW - 1, pl.program_id(1) == 0)
    low = lax.broadcasted_iota(jnp.int32, (BLOCK, LANES), 1) < HEAD_DIM
    rowg = lax.broadcasted_iota(jnp.int32, (rows, 1), 0) // BLOCK
    sinks = []
    for hk in range(A_KV_HEADS):
        sk = jnp.zeros((rows, 1), F32)
        for g in range(A_GROUP):
            sk = jnp.where(rowg == g, sink_ref[hk * A_GROUP + g], sk)
        sinks.append(sk)
    kvh = range(A_KV_HEADS)
    for jb in range(nb):
        rs = slice(jb * BLOCK, (jb + 1) * BLOCK)
        bias = bias_first if jb == 0 else bias_inner
        sc = []
        for hk in kvh:
            tiles = [q_ref[rs, (2 * hk + t) * LANES:(2 * hk + t + 1) * LANES] for t in range(2)]
            zero = jnp.zeros_like(tiles[0])
            q4 = jnp.concatenate([jnp.where(low, tiles[0], zero), jnp.where(low, zero, tiles[0]),
                                  jnp.where(low, tiles[1], zero), jnp.where(low, zero, tiles[1])], axis=0)
            k = _window_rows(kvp_ref, kvc_ref, jb, slice(hk * LANES, (hk + 1) * LANES))
            sc.append(_dot_nt(q4, k) + bias)
        m = [jnp.maximum(jnp.max(s, axis=-1, keepdims=True), sk) for s, sk in zip(sc, sinks)]
        p = [jnp.exp(s - mm) for s, mm in zip(sc, m)]
        den = [jnp.sum(pp, axis=-1, keepdims=True) + jnp.exp(sk - mm) for pp, sk, mm in zip(p, sinks, m)]
        o = []
        for hk in kvh:
            v = _window_rows(kvp_ref, kvc_ref, jb, slice((A_KV_HEADS + hk) * LANES, (A_KV_HEADS + hk + 1) * LANES))
            o.append(_dot(p[hk].astype(BF16), v) * (1.0 / den[hk]))
        for hk in kvh:
            for t in range(2):
                pair = jnp.where(low, o[hk][2 * t * BLOCK:(2 * t + 1) * BLOCK], o[hk][(2 * t + 1) * BLOCK:(2 * t + 2) * BLOCK])
                o_ref[rs, (2 * hk + t) * LANES:(2 * hk + t + 1) * LANES] = pair.astype(BF16)


def _attn_a(qa, kva, sinks, tile=512):
    bsz, s, _ = qa.shape
    tile = min(tile, s)
    kvw = kva.shape[2]
    per = tile // BLOCK
    return pl.pallas_call(
        _attn_a_kernel,
        grid=(bsz, s // tile),
        in_specs=[pl.BlockSpec(memory_space=pltpu.SMEM),
                  pl.BlockSpec((None, tile, A_Q_W), lambda b, i: (b, i, 0)),
                  pl.BlockSpec((None, BLOCK, kvw), lambda b, i: (b, jnp.maximum(i * per - 1, 0), 0)),
                  pl.BlockSpec((None, tile, kvw), lambda b, i: (b, i, 0))],
        out_specs=pl.BlockSpec((None, tile, A_Q_W), lambda b, i: (b, i, 0)),
        out_shape=jax.ShapeDtypeStruct((bsz, s, A_Q_W), BF16),
        compiler_params=_cparams("parallel", "parallel"),
        name="attn_a",
    )(sinks, qa, kva, kva)


def _attn_b_kernel(group, q_ref, kp_ref, kc_ref, vp_ref, vc_ref, o_ref, l_ref):
    nb = q_ref.shape[0] // BLOCK
    bias_first, bias_inner = _band_bias(BLOCK, BLOCK, pl.program_id(2) == 0)
    low = lax.broadcasted_iota(jnp.int32, (BLOCK, LANES), 1) < HEAD_DIM
    for j0 in range(0, nb, group):
        items = [(jb, h) for jb in range(j0, min(j0 + group, nb)) for h in range(B_HEADS)]
        sc = []
        for jb, h in items:
            lanes = slice((h // 2) * LANES, (h // 2 + 1) * LANES)
            qt = q_ref[jb * BLOCK:(jb + 1) * BLOCK, lanes]
            zero = jnp.zeros_like(qt)
            qh = jnp.where(low, qt, zero) if h % 2 == 0 else jnp.where(low, zero, qt)
            sc.append(_dot_nt(qh, _window_rows(kp_ref, kc_ref, jb, lanes)) + (bias_first if jb == 0 else bias_inner))
        m = [jnp.max(s, axis=-1, keepdims=True) for s in sc]
        p = [jnp.exp(s - mm) for s, mm in zip(sc, m)]
        den = [jnp.sum(pp, axis=-1, keepdims=True) for pp in p]
        o = [_dot(pp.astype(BF16), _window_rows(vp_ref, vc_ref, jb, slice((h // 2) * LANES, (h // 2 + 1) * LANES)))
             * (1.0 / dd) for pp, dd, (jb, h) in zip(p, den, items)]
        lse = [mm + jnp.log(dd) for mm, dd in zip(m, den)]
        for idx in range(0, len(items), 2):
            jb, h = items[idx]
            dst = (slice(jb * BLOCK, (jb + 1) * BLOCK), slice((h // 2) * LANES, (h // 2 + 1) * LANES))
            o_ref[dst] = jnp.where(low, o[idx], o[idx + 1])
            l_ref[dst] = jnp.where(low, lse[idx], lse[idx + 1])


def _attn_b(q, k, v, pat, dil, tile=512, group=2):
    bsz, length, _ = q.shape
    tile = min(tile, length)
    per = tile // BLOCK
    cur = pl.BlockSpec((None, tile, B_PW), lambda b, r, i: (b, i, r))
    prev = pl.BlockSpec((None, BLOCK, B_PW), lambda b, r, i: (b, jnp.maximum(i * per - 1, 0), r))
    return pl.pallas_call(
        functools.partial(_attn_b_kernel, group),
        grid=(bsz, dil, length // tile),
        in_specs=[cur, prev, cur, prev, cur],
        out_specs=[cur, cur],
        out_shape=[jax.ShapeDtypeStruct((bsz, length, dil * B_PW), F32)] * 2,
        compiler_params=_cparams("parallel", "parallel", "parallel"),
        name=f"attn_b{pat}",
    )(q, k, k, v, v)


def _attn_out_kernel(x_ref, oa_ref, o0_ref, o1_ref, o2_ref, l0_ref, l1_ref, l2_ref, w_ref, out_ref, buf):
    tm = x_ref.shape[0]

    def natural(ref, dil, slot):
        if dil == 1:
            return ref[...]
        n = tm // dil
        halves = B_PW // LANES
        for r in range(dil):
            for j in range(halves):
                buf[slot * halves + j, pl.ds(r, n, stride=dil), :] = ref[:, r * B_PW + j * LANES:r * B_PW + (j + 1) * LANES]
        return jnp.concatenate([buf[slot * halves + j] for j in range(halves)], axis=1)

    dils = [dil for _, dil in B_PATTERNS]
    o = [natural(ref, dil, i) for i, (ref, dil) in enumerate(zip((o0_ref, o1_ref, o2_ref), dils))]
    l = [natural(ref, dil, 3 + i) for i, (ref, dil) in enumerate(zip((l0_ref, l1_ref, l2_ref), dils))]
    m = jnp.maximum(jnp.maximum(l[0], l[1]), l[2])
    e = [jnp.exp(v - m) for v in l]
    ob = (e[0] * o[0] + e[1] * o[1] + e[2] * o[2]) / (e[0] + e[1] + e[2])
    mix = _dot(oa_ref[...], w_ref[:A_Q_W, :]) + _dot(ob.astype(BF16), w_ref[A_Q_W:, :])
    out_ref[...] = x_ref[...] + mix


def _attn_out(x, oa, obs, lses, w_out, tm=512):
    bsz, s, d = x.shape
    tok = lambda w: pl.BlockSpec((None, tm, w), lambda b, i: (b, i, 0))
    pat = [pl.BlockSpec((None, tm // dil, dil * B_PW), lambda b, i: (b, i, 0)) for _, dil in B_PATTERNS]
    return pl.pallas_call(
        _attn_out_kernel,
        grid=(bsz, s // tm),
        in_specs=[tok(d), tok(A_Q_W)] + pat + pat + [pl.BlockSpec(w_out.shape, lambda b, i: (0, 0))],
        out_specs=tok(d),
        out_shape=jax.ShapeDtypeStruct((bsz, s, d), F32),
        scratch_shapes=[pltpu.VMEM((2 * len(B_PATTERNS) * (B_PW // LANES), tm, LANES), F32)],
        compiler_params=_cparams("parallel", "parallel"),
        name="attn_out",
    )(x, oa, *obs, *lses, w_out)


def _ffn_kernel(final_norm, ff_chunk, x_ref, g_ref, wg_ref, wu_ref, wd_ref, gf_ref, out_ref, acc_ref):
    x = x_ref[...]
    h = _rms(x, g_ref[...]).astype(BF16)
    d_ff = wg_ref.shape[1]
    for c in range(d_ff // ff_chunk):
        sl = slice(c * ff_chunk, (c + 1) * ff_chunk)
        gate = _dot(h, wg_ref[:, sl])
        up = _dot(h, wu_ref[:, sl])
        act = (gate * _sigmoid(gate) * up).astype(BF16)
        contrib = _dot(act, wd_ref[sl, :])
        if c == 0:
            acc_ref[...] = x + contrib
        else:
            acc_ref[...] += contrib
    y = acc_ref[...]
    if final_norm:
        y = _rms(y, gf_ref[...])
    out_ref[...] = y


def _ffn(x, g, wg, wu, wd, g_final, final_norm, tm=512, ff_chunk=256):
    bsz, s, d = x.shape
    d_ff = wg.shape[1]
    tok = pl.BlockSpec((None, tm, d), lambda b, i: (b, i, 0))
    full = lambda shape: pl.BlockSpec(shape, lambda b, i: (0,) * len(shape))
    return pl.pallas_call(
        functools.partial(_ffn_kernel, final_norm, ff_chunk),
        grid=(bsz, s // tm),
        in_specs=[tok, full((1, d)), full((d, d_ff)), full((d, d_ff)), full((d_ff, d)), full((1, d))],
        out_specs=tok,
        out_shape=jax.ShapeDtypeStruct((bsz, s, d), F32),
        scratch_shapes=[pltpu.VMEM((tm, d), F32)],
        compiler_params=_cparams("parallel", "parallel"),
        name="ffn",
    )(x, g.reshape(1, d), wg, wu, wd, g_final.reshape(1, d))


def _rwkv_in_kernel(has_vlora, *refs):
    (x_ref, gn_ref, mu_ref, wr_ref, wk_ref, wv_ref, w0_ref, w1_ref, w2_ref, a0_ref, a1_ref, a2_ref,
     g1_ref, g2_ref, kk_ref, ka_ref) = refs[:16]
    refs = refs[16:]
    if has_vlora:
        v0_ref, v1_ref, v2_ref, vf_ref = refs[:4]
        refs = refs[4:]
    r_o, lw_o, k_o, v_o, kk_o, ab_o, g_o, carry, mix = refs
    tm = x_ref.shape[0]

    h = _rms(x_ref[...], gn_ref[...])

    @pl.when(pl.program_id(1) == 0)
    def _():
        carry[...] = jnp.zeros(carry.shape, F32)

    rolled = pltpu.roll(h, 1, 0)
    first = lax.broadcasted_iota(jnp.int32, (8, 1), 0) == 0
    hprev = jnp.concatenate([jnp.where(first, carry[7:8, :], rolled[0:8]), rolled[8:]], axis=0)
    carry[...] = h[tm - 8:tm, :]
    xx = hprev - h

    for i in range(mix.shape[0]):
        mix[i] = (h + xx * mu_ref[i:i + 1, :]).astype(BF16)

    def mixed(i):
        return mix[i]

    def lora(z, wa_ref, wb_ref, act=None):
        mid = _dot(z, wa_ref[...])
        if act is not None:
            mid = act(mid)
        return _dot(mid.astype(BF16), wb_ref[...])

    r = _dot(mixed(0), wr_ref[...])
    zw = w0_ref[...] + lora(mixed(1), w1_ref, w2_ref, jnp.tanh)
    lw = -DECAY_SCALE * _sigmoid(zw)
    k = _dot(mixed(2), wk_ref[...])
    xv = mixed(3)
    v = _dot(xv, wv_ref[...])
    if has_vlora:
        v = v + (vf_ref[...] - v) * _sigmoid(v0_ref[...] + lora(xv, v1_ref, v2_ref))
    a = _sigmoid(a0_ref[...] + lora(mixed(4), a1_ref, a2_ref))
    g = lora(mixed(5), g1_ref, g2_ref, _sigmoid)

    ones_bd = jnp.where(_head_mask(MXU_W), 1.0, 0.0).astype(BF16)
    kk = k * kk_ref[...]
    kk = kk / jnp.maximum(jnp.sqrt(_head_sum(kk * kk, ones_bd)), 1e-12)
    k = k * (1.0 + (a - 1.0) * ka_ref[...])

    r_o[...] = r
    lw_o[...] = lw
    k_o[...] = k
    v_o[...] = v
    kk_o[...] = kk
    ab_o[...] = kk * a
    g_o[...] = g


def _rwkv_in(x, gn, p, v_lora, v_first, tm=256):
    bsz, s, d = x.shape
    tok = pl.BlockSpec((None, tm, d), lambda b, i: (b, i, 0))
    full = lambda a: pl.BlockSpec(a.shape, lambda b, i: (0,) * a.ndim)
    row = lambda a: a.reshape(1, d)
    args = [x, row(gn), p["mu"], p["w_r"], p["w_k"], p["w_v"], row(p["w0"]), p["w1"], p["w2"], row(p["a0"]), p["a1"], p["a2"],
            p["g1"], p["g2"], row(p["k_k"]), row(p["k_a"])]
    specs = [tok] + [full(a) for a in args[1:]]
    if v_lora is not None:
        v0, v1, v2 = v_lora
        extra = [row(v0), v1, v2]
        args += extra + [v_first]
        specs += [full(a) for a in extra] + [tok]
    return pl.pallas_call(
        functools.partial(_rwkv_in_kernel, v_lora is not None),
        grid=(bsz, s // tm),
        in_specs=specs,
        out_specs=[tok] * 7,
        out_shape=[jax.ShapeDtypeStruct((bsz, s, d), F32)] * 7,
        scratch_shapes=[pltpu.VMEM((8, d), F32), pltpu.VMEM((p["mu"].shape[0], tm, d), BF16)],
        compiler_params=_cparams("parallel", "arbitrary"),
        name="rwkv_in",
    )(*args)


def _bd2(z):
    lane = lax.broadcasted_iota(jnp.int32, z.shape, 1)
    zero = jnp.zeros(z.shape, z.dtype)
    return jnp.concatenate([jnp.where(lane < HEAD_DIM, z, zero), jnp.where(lane >= HEAD_DIM, z, zero)], axis=0)


def _fold2(full):
    lane = lax.broadcasted_iota(jnp.int32, (HEAD_DIM, LANES), 1)
    return jnp.where(lane < HEAD_DIM, full[:HEAD_DIM], full[HEAD_DIM:])


def _wkv_prep_kernel(r_ref, lw_ref, k_ref, v_ref, kk_ref, ab_ref, q_ref, y0_ref, mc_ref, z_ref, dec_ref):
    cs = WKV_CHUNK
    rows, d = r_ref.shape
    nc = rows // cs
    ri = lax.broadcasted_iota(jnp.int32, (rows, rows), 0)
    ci = lax.broadcasted_iota(jnp.int32, (rows, rows), 1)
    tri = jnp.where((ri >= ci) & (ri // cs == ci // cs), 1.0, 0.0).astype(BF16)
    lw = lw_ref[...]
    hi = lw.astype(BF16)
    rem = lw - hi.astype(F32)
    mid = rem.astype(BF16)
    lo = (rem - mid.astype(F32)).astype(BF16)
    cum = _dot(tri, hi) + _dot(tri, mid) + _dot(tri, lo)
    totals = [cum[(j + 1) * cs - 1:(j + 1) * cs, :] for j in range(nc)]
    tot = jnp.concatenate([jnp.broadcast_to(t, (cs, d)) for t in totals], axis=0)
    e_neg = jnp.exp(-cum)
    e_tot = jnp.exp(tot - cum)
    kk, ab, kx = kk_ref[...], ab_ref[...], k_ref[...]
    at = (-kk * jnp.exp(cum - lw)).astype(BF16)
    rt = r_ref[...] * jnp.exp(cum)
    rtb = rt.astype(BF16)
    bt = (ab * e_neg).astype(BF16)
    kt = (kx * e_neg).astype(BF16)
    bh = (ab * e_tot).astype(BF16)
    kh = (kx * e_tot).astype(BF16)
    vb = v_ref[...].astype(BF16)

    trow = lax.broadcasted_iota(jnp.int32, (cs, LANES), 0)
    tcol = lax.broadcasted_iota(jnp.int32, (cs, LANES), 1) % HEAD_DIM
    strict = trow > tcol
    incl = trow >= tcol
    eye = jnp.where(trow == tcol, 1.0, 0.0)

    for j in range(nc):
        dec_ref[8 * j:8 * j + 8, :] = jnp.broadcast_to(jnp.exp(totals[j]), (8, d))

    chains = [(slice(j * cs, (j + 1) * cs), slice(p * LANES, (p + 1) * LANES))
              for j in range(nc) for p in range(d // LANES)]

    g12 = [_dot_nt(jnp.concatenate([at[c], rtb[c]], axis=0),
                   jnp.concatenate([_bd2(bt[c]), _bd2(kt[c])], axis=0)) for c in chains]
    a_ab = [jnp.where(strict, g[:cs, :LANES], 0.0) for g in g12]
    a_rb = [jnp.where(incl, g[cs:, :LANES], 0.0).astype(BF16) for g in g12]
    a_k = [jnp.concatenate([jnp.where(strict, g[:cs, LANES:], 0.0),
                            jnp.where(incl, g[cs:, LANES:], 0.0)], axis=0).astype(BF16) for g in g12]
    g4 = [_dot(a, _bd2(vb[c])) for a, c in zip(a_k, chains)]
    x = [eye + jnp.where((trow - tcol == 1) & (trow % 2 == 1), a, 0.0) for a in a_ab]
    size = 2
    while size < cs:
        lower_left = (trow // size - tcol // size == 1) & ((trow // size) % 2 == 1)
        xb = [v.astype(BF16) for v in x]
        xn = [_dot(v, _bd2(jnp.where(lower_left, a, 0.0).astype(BF16))) for v, a in zip(xb, a_ab)]
        x = [v + _dot(n.astype(BF16), _bd2(vb16)) for v, n, vb16 in zip(x, xn, xb)]
        size *= 2
    wu = [_dot(v.astype(BF16), jnp.concatenate([_bd2(at[c]), _bd2(g[:cs].astype(BF16))], axis=1))
          for v, g, c in zip(x, g4, chains)]
    wb = [v[:, :LANES].astype(BF16) for v in wu]
    ub = [v[:, LANES:].astype(BF16) for v in wu]
    qy = [_dot(a, jnp.concatenate([_bd2(w), _bd2(u)], axis=1)) for a, w, u in zip(a_rb, wb, ub)]
    mc = [_dot_tn(w, bh[c]) for w, c in zip(wb, chains)]
    zz = [_dot_tn(jnp.concatenate([u, vb[c]], axis=0), jnp.concatenate([bh[c], kh[c]], axis=0))
          for u, c in zip(ub, chains)]
    for i, c in enumerate(chains):
        q_ref[c] = (rt[c] + qy[i][:, :LANES]).astype(BF16)
        y0_ref[c] = g4[i][cs:] + qy[i][:, LANES:]
        mc_ref[c] = _fold2(mc[i]).astype(BF16)
        z_ref[c] = _fold2(zz[i])


def _wkv_scan_kernel(q_ref, y0_ref, mc_ref, z_ref, dec_ref, y_ref, s_ref):
    cs = WKV_CHUNK
    bsz, rows, d = q_ref.shape

    @pl.when(pl.program_id(0) == 0)
    def _():
        s_ref[...] = jnp.zeros(s_ref.shape, F32)

    seqs = [(b, slice(p * LANES, (p + 1) * LANES)) for b in range(bsz) for p in range(d // LANES)]
    state = [s_ref[b, :, ls] for b, ls in seqs]
    for j in range(rows // cs):
        rs = slice(j * cs, (j + 1) * cs)
        sb = [s.astype(BF16) for s in state]
        upd = [_dot(v, _bd2(mc_ref[b, rs, ls])) for v, (b, ls) in zip(sb, seqs)]
        for v, (b, ls) in zip(sb, seqs):
            y_ref[b, rs, ls] = _dot_nt(q_ref[b, rs, ls], _bd2(v)) + y0_ref[b, rs, ls]
        state = [s * dec_ref[b, 8 * j:8 * j + 1, ls] + u + z_ref[b, rs, ls]
                 for s, u, (b, ls) in zip(state, upd, seqs)]
    for s, (b, ls) in zip(state, seqs):
        s_ref[b, :, ls] = s


def _wkv(r, lw, k, v, kk, ab, prep_chunks=2, scan_chunks=2):
    bsz, s, d = r.shape
    cs = WKV_CHUNK
    rows = prep_chunks * cs
    blk = pl.BlockSpec((None, rows, d), lambda b, c: (b, c, 0))
    dec_blk = pl.BlockSpec((None, 8 * prep_chunks, d), lambda b, c: (b, c, 0))
    q, y0, mc, z, dec = pl.pallas_call(
        _wkv_prep_kernel,
        grid=(bsz, s // rows),
        in_specs=[blk] * 6,
        out_specs=[blk, blk, blk, blk, dec_blk],
        out_shape=[jax.ShapeDtypeStruct((bsz, s, d), BF16), jax.ShapeDtypeStruct((bsz, s, d), F32),
                   jax.ShapeDtypeStruct((bsz, s, d), BF16), jax.ShapeDtypeStruct((bsz, s, d), F32),
                   jax.ShapeDtypeStruct((bsz, 8 * s // cs, d), F32)],
        compiler_params=_cparams("parallel", "parallel"),
        name="wkv7_prep",
    )(r, lw, k, v, kk, ab)
    rows = scan_chunks * cs
    blk = pl.BlockSpec((bsz, rows, d), lambda c: (0, c, 0))
    dec_blk = pl.BlockSpec((bsz, 8 * scan_chunks, d), lambda c: (0, c, 0))
    return pl.pallas_call(
        _wkv_scan_kernel,
        grid=(s // rows,),
        in_specs=[blk, blk, blk, blk, dec_blk],
        out_specs=blk,
        out_shape=jax.ShapeDtypeStruct((bsz, s, d), F32),
        scratch_shapes=[pltpu.VMEM((bsz, HEAD_DIM, d), F32)],
        compiler_params=_cparams("arbitrary"),
        name="wkv7_scan",
    )(q, y0, mc, z, dec)


def _rwkv_out_kernel(x_ref, y_ref, r_ref, k_ref, v_ref, g_ref, lnw_ref, lnb_ref, rk_ref, wo_ref, out_ref):
    ones_bd = jnp.where(_head_mask(MXU_W), 1.0, 0.0).astype(BF16)
    y = y_ref[...]
    dev = y - _head_sum(y, ones_bd) * (1.0 / HEAD_DIM)
    var = _head_sum(dev * dev, ones_bd) * (1.0 / HEAD_DIM)
    yn = dev * lax.rsqrt(var + LNX_EPS) * lnw_ref[...] + lnb_ref[...]
    bonus = _head_sum(r_ref[...] * k_ref[...] * rk_ref[...], ones_bd) * v_ref[...]
    z = ((yn + bonus) * g_ref[...]).astype(BF16)
    out_ref[...] = x_ref[...] + _dot(z, wo_ref[...])


def _rwkv_out(x, y, r, k, v, g, lnw, lnb, rk, wo, tm=512):
    bsz, s, d = x.shape
    tok = pl.BlockSpec((None, tm, d), lambda b, i: (b, i, 0))
    row = pl.BlockSpec((1, d), lambda b, i: (0, 0))
    return pl.pallas_call(
        _rwkv_out_kernel,
        grid=(bsz, s // tm),
        in_specs=[tok] * 6 + [row] * 3 + [pl.BlockSpec((d, d), lambda b, i: (0, 0))],
        out_specs=tok,
        out_shape=jax.ShapeDtypeStruct((bsz, s, d), F32),
        compiler_params=_cparams("parallel", "parallel"),
        name="rwkv_out",
    )(x, y, r, k, v, g, lnw.reshape(1, d), lnb.reshape(1, d), rk.reshape(1, d), wo)


def kernel(x, positions, norm_mix, norm_ffn, norm_final, attn_w_in, attn_b_in, attn_sinks, attn_w_out, rwkv_mu, rwkv_w_rkv, rwkv_w0, rwkv_w1, rwkv_w2, rwkv_a0, rwkv_a1, rwkv_a2, rwkv_g1, rwkv_g2, rwkv_k_k, rwkv_k_a, rwkv_r_k, rwkv_lnx_w, rwkv_lnx_b, rwkv_w_o, rwkv_v0, rwkv_v1, rwkv_v2, ffn_w_gate, ffn_w_up, ffn_w_down):
    depth = norm_mix.shape[0]
    bf = lambda a: a.astype(BF16)
    tables = _rope_tables(positions)
    v_first = None
    for layer in range(depth):
        i = layer // 2
        if layer % 2 == 0:
            qa, kva, qkv_b = _attn_in(x, norm_mix[layer], _widen_in_proj(bf(attn_w_in[i])),
                                      _widen_in_proj(attn_b_in[i]), tables)
            oa = _attn_a(qa, kva, attn_sinks[i])
            obs, lses = zip(*[_attn_b(*qkv_b[pat], pat, dil) for pat, (_, dil) in enumerate(B_PATTERNS)])
            x = _attn_out(x, oa, obs, lses, bf(attn_w_out[i]))
        else:
            p = dict(mu=rwkv_mu[i], w_r=bf(rwkv_w_rkv[i, 0]), w_k=bf(rwkv_w_rkv[i, 1]), w_v=bf(rwkv_w_rkv[i, 2]),
                     w0=rwkv_w0[i], w1=bf(rwkv_w1[i]), w2=bf(rwkv_w2[i]),
                     a0=rwkv_a0[i], a1=bf(rwkv_a1[i]), a2=bf(rwkv_a2[i]), g1=bf(rwkv_g1[i]), g2=bf(rwkv_g2[i]),
                     k_k=rwkv_k_k[i], k_a=rwkv_k_a[i])
            v_lora = None if i == 0 else (rwkv_v0[i - 1], bf(rwkv_v1[i - 1]), bf(rwkv_v2[i - 1]))
            r, lw, k, v, kk, ab, g = _rwkv_in(x, norm_mix[layer], p, v_lora, v_first)
            if i == 0:
                v_first = v
            y = _wkv(r, lw, k, v, kk, ab)
            x = _rwkv_out(x, y, r, k, v, g, rwkv_lnx_w[i], rwkv_lnx_b[i], rwkv_r_k[i], bf(rwkv_w_o[i]))
        x = _ffn(x, norm_ffn[layer], bf(ffn_w_gate[layer]), bf(ffn_w_up[layer]), bf(ffn_w_down[layer]),
                 norm_final, final_norm=(layer == depth - 1))
    return x
```

```python
import functools

import jax
import jax.numpy as jnp
from jax import lax
from jax.experimental import pallas as pl
from jax.experimental.pallas import tpu as pltpu

F32 = jnp.float32
BF16 = jnp.bfloat16

HEAD_DIM = 64
ROT_DIM = HEAD_DIM // 4
ROT_HALF = ROT_DIM // 2
ROPE_THETA = 500000.0
BLOCK = 128
NORM_EPS = 1e-5
LNX_EPS = 64e-5

A_Q_HEADS = 12
A_KV_HEADS = 3
A_GROUP = A_Q_HEADS // A_KV_HEADS
A_WINDOW = 128
B_PATTERNS = ((128, 1), (512, 4), (2048, 16))
B_HEADS = 4

A_Q_W = A_Q_HEADS * HEAD_DIM
A_KV_W = A_KV_HEADS * HEAD_DIM
B_PW = B_HEADS * HEAD_DIM
B_W = len(B_PATTERNS) * B_PW
KV_DUP_W = 2 * A_KV_W
OFF_KA = A_Q_W
OFF_VA = OFF_KA + KV_DUP_W
OFF_QB = OFF_VA + KV_DUP_W
OFF_KB = OFF_QB + B_W
OFF_VB = OFF_KB + B_W

LANES = 128
MXU_W = 256
WKV_CHUNK = 64
DECAY_SCALE = 0.6065306597126334
NEG_BIG = -1e30
VMEM_LIMIT = 56 * 1024 * 1024


def _cparams(*sem):
    return pltpu.CompilerParams(dimension_semantics=sem, vmem_limit_bytes=VMEM_LIMIT)


def _dot(a, b):
    return jnp.dot(a, b, preferred_element_type=F32)


def _dot_nt(a, b):
    return lax.dot_general(a, b, (((1,), (1,)), ((), ())), preferred_element_type=F32)


def _dot_tn(a, b):
    return lax.dot_general(a, b, (((0,), (0,)), ((), ())), preferred_element_type=F32)


def _row_sum(z):
    part = z[:, :LANES]
    for j in range(1, z.shape[1] // LANES):
        part = part + z[:, j * LANES:(j + 1) * LANES]
    ones = jnp.ones((LANES, LANES), BF16)
    hi = part.astype(BF16)
    lo = (part - hi.astype(F32)).astype(BF16)
    return _dot(hi, ones) + _dot(lo, ones)


def _rms(x, g):
    inv = lax.rsqrt(_row_sum(x * x) * (1.0 / x.shape[1]) + NORM_EPS)
    return x * jnp.concatenate([inv] * (x.shape[1] // LANES), axis=1) * g


def _sigmoid(z):
    return 1.0 / (1.0 + jnp.exp(-z))


def _head_mask(n):
    r = lax.broadcasted_iota(jnp.int32, (n, n), 0) // HEAD_DIM
    c = lax.broadcasted_iota(jnp.int32, (n, n), 1) // HEAD_DIM
    return r == c


def _head_sum(z, ones_bd, split=True):
    outs = []
    for j in range(z.shape[1] // MXU_W):
        zc = z[:, j * MXU_W:(j + 1) * MXU_W]
        hi = zc.astype(BF16)
        acc = _dot(hi, ones_bd)
        if split:
            acc = acc + _dot((zc - hi.astype(F32)).astype(BF16), ones_bd)
        outs.append(acc)
    return jnp.concatenate(outs, axis=1)


def _rope_trig_kernel(pos_ref, invf_ref, cos_ref, sin_ref):
    ang = pos_ref[...].astype(F32)[None] * invf_ref[...]
    cos_ref[...] = jnp.cos(ang)
    sin_ref[...] = jnp.sin(ang)


def _rope_tables(positions):
    bsz, s = positions.shape
    rows = bsz * s // LANES
    inv_freq = jnp.power(ROPE_THETA, -2.0 * jnp.arange(ROT_HALF, dtype=F32) / ROT_DIM)
    invf = jnp.broadcast_to(inv_freq[:, None, None], (ROT_HALF, 1, LANES))
    cos, sin = pl.pallas_call(
        _rope_trig_kernel,
        out_shape=(jax.ShapeDtypeStruct((ROT_HALF, rows, LANES), F32),) * 2,
        name="rope_trig",
    )(positions.reshape(rows, LANES), invf)
    cos = cos.reshape(ROT_HALF, bsz, s).transpose(1, 2, 0)
    sin = sin.reshape(ROT_HALF, bsz, s).transpose(1, 2, 0)
    ones = jnp.ones((bsz, s, HEAD_DIM - ROT_DIM), F32)
    z8 = jnp.zeros((bsz, s, ROT_HALF), F32)
    zrest = jnp.zeros((bsz, s, HEAD_DIM - ROT_DIM), F32)
    c = jnp.concatenate([cos, cos, ones], axis=-1)
    sa = jnp.concatenate([-sin, z8, zrest], axis=-1)
    sb = jnp.concatenate([z8, sin, zrest], axis=-1)
    return tuple(jnp.tile(t, (1, 1, LANES // HEAD_DIM)) for t in (c, sa, sb))


def _residue_major(ref, dil):
    n = ref.shape[0] // dil
    return jnp.concatenate([ref[pl.ds(r, n, stride=dil), :] for r in range(dil)], axis=0)


def _attn_in_kernel(x_ref, g_ref, w_ref, b_ref, c_ref, sa_ref, sb_ref, qa_ref, kva_ref, *rest):
    b_refs, hbuf = rest[:-1], rest[-1]
    hf = _rms(x_ref[...], g_ref[...])
    tm, d = hf.shape
    for j in range(d // LANES):
        hbuf[j] = hf[:, j * LANES:(j + 1) * LANES]
    scale = HEAD_DIM ** -0.5

    def proj(h, lo, width):
        return _dot(h, w_ref[:, lo:lo + width]) + b_ref[:, lo:lo + width]

    def rope(z, tabs):
        c, sa, sb = tabs
        return z * c + pltpu.roll(z, LANES - ROT_HALF, 1) * sa + pltpu.roll(z, ROT_HALF, 1) * sb

    def rope_all(z, tabs):
        return jnp.concatenate([rope(z[:, j * LANES:(j + 1) * LANES], tabs)
                                for j in range(z.shape[1] // LANES)], axis=1)

    h = hf.astype(BF16)
    tabs = (c_ref[...], sa_ref[...], sb_ref[...])
    for j in range(A_Q_W // MXU_W):
        qa_ref[:, j * MXU_W:(j + 1) * MXU_W] = (rope_all(proj(h, j * MXU_W, MXU_W), tabs) * scale).astype(BF16)
    kva_ref[:, :KV_DUP_W] = rope_all(proj(h, OFF_KA, KV_DUP_W), tabs).astype(BF16)
    kva_ref[:, KV_DUP_W:] = proj(h, OFF_VA, KV_DUP_W).astype(BF16)
    for pat, (_, dil) in enumerate(B_PATTERNS):
        q_ref, k_ref, v_ref = b_refs[3 * pat:3 * pat + 3]
        if dil > 1:
            h = jnp.concatenate([_residue_major(hbuf.at[j], dil) for j in range(d // LANES)],
                                axis=1).astype(BF16)
            tabs = tuple(_residue_major(t, dil) for t in (c_ref, sa_ref, sb_ref))
        q = (rope_all(proj(h, OFF_QB + pat * B_PW, B_PW), tabs) * scale).astype(BF16)
        k = rope_all(proj(h, OFF_KB + pat * B_PW, B_PW), tabs).astype(BF16)
        v = proj(h, OFF_VB + pat * B_PW, B_PW).astype(BF16)
        n = tm // dil
        for r in range(dil):
            q_ref[:, r * B_PW:(r + 1) * B_PW] = q[r * n:(r + 1) * n]
            k_ref[:, r * B_PW:(r + 1) * B_PW] = k[r * n:(r + 1) * n]
            v_ref[:, r * B_PW:(r + 1) * B_PW] = v[r * n:(r + 1) * n]


def _widen_in_proj(w):
    head = lambda off, h: w[..., off + h * HEAD_DIM:off + (h + 1) * HEAD_DIM]
    dup = lambda off: [head(off, h) for h in range(A_KV_HEADS) for _ in range(2)]
    return jnp.concatenate([w[..., :A_Q_W]] + dup(A_Q_W) + dup(A_Q_W + A_KV_W) + [w[..., A_Q_W + 2 * A_KV_W:]], axis=-1)


def _attn_in(x, g, w_in, b_in, tables, tm=512):
    bsz, s, d = x.shape
    in_w = w_in.shape[1]
    tok = lambda w: pl.BlockSpec((None, tm, w), lambda b, i: (b, i, 0))
    full = lambda shape: pl.BlockSpec(shape, lambda b, i: (0,) * len(shape))
    out_specs = [tok(A_Q_W), tok(2 * KV_DUP_W)]
    out_shape = [jax.ShapeDtypeStruct((bsz, s, A_Q_W), BF16), jax.ShapeDtypeStruct((bsz, s, 2 * KV_DUP_W), BF16)]
    for _, dil in B_PATTERNS:
        out_specs += [pl.BlockSpec((None, tm // dil, dil * B_PW), lambda b, i: (b, i, 0))] * 3
        out_shape += [jax.ShapeDtypeStruct((bsz, s // dil, dil * B_PW), BF16)] * 3
    outs = pl.pallas_call(
        _attn_in_kernel,
        grid=(bsz, s // tm),
        in_specs=[tok(d), full((1, d)), full((d, in_w)), full((1, in_w)),
                  tok(LANES), tok(LANES), tok(LANES)],
        out_specs=out_specs,
        out_shape=out_shape,
        scratch_shapes=[pltpu.VMEM((d // LANES, tm, LANES), F32)],
        compiler_params=_cparams("parallel", "parallel"),
        name="attn_in",
    )(x, g.reshape(1, d), w_in, b_in.reshape(1, in_w), *tables)
    return outs[0], outs[1], [outs[2 + 3 * p:5 + 3 * p] for p in range(len(B_PATTERNS))]


def _band_bias(rows, max_dist, first_tile):
    qi = lax.broadcasted_iota(jnp.int32, (rows, 2 * BLOCK), 0) % BLOCK
    kj = lax.broadcasted_iota(jnp.int32, (rows, 2 * BLOCK), 1)
    dist = BLOCK + qi - kj
    band = (dist >= 0) & (dist <= max_dist)
    inner = jnp.where(band, 0.0, NEG_BIG)
    return jnp.where(band & ((kj >= BLOCK) | jnp.logical_not(first_tile)), 0.0, NEG_BIG), inner


def _window_rows(prev_ref, cur_ref, jb, lanes):
    if jb == 0:
        return jnp.concatenate([prev_ref[:, lanes], cur_ref[0:BLOCK, lanes]], axis=0)
    return cur_ref[(jb - 1) * BLOCK:(jb + 1) * BLOCK, lanes]


def _attn_a_kernel(sink_ref, q_ref, kvp_ref, kvc_ref, o_ref):
    nb = q_ref.shape[0] // BLOCK
    rows = A_GROUP * BLOCK
    bias_first, bias_inner = _band_bias(rows, A_WINDOW - 1, pl.program_id(1) == 0)
    low = lax.broadcasted_iota(jnp.int32, (BLOCK, LANES), 1) < HEAD_DIM
    rowg = lax.broadcasted_iota(jnp.int32, (rows, 1), 0) // BLOCK
    sinks = []
    for hk in range(A_KV_HEADS):
        sk = jnp.zeros((rows, 1), F32)
        for g in range(A_GROUP):
            sk = jnp.where(rowg == g, sink_ref[hk * A_GROUP + g], sk)
        sinks.append(sk)
    kvh = range(A_KV_HEADS)
    for jb in range(nb):
        rs = slice(jb * BLOCK, (jb + 1) * BLOCK)
        bias = bias_first if jb == 0 else bias_inner
        sc = []
        for hk in kvh:
            tiles = [q_ref[rs, (2 * hk + t) * LANES:(2 * hk + t + 1) * LANES] for t in range(2)]
            zero = jnp.zeros_like(tiles[0])
            q4 = jnp.concatenate([jnp.where(low, tiles[0], zero), jnp.where(low, zero, tiles[0]),
                                  jnp.where(low, tiles[1], zero), jnp.where(low, zero, tiles[1])], axis=0)
            k = _window_rows(kvp_ref, kvc_ref, jb, slice(hk * LANES, (hk + 1) * LANES))
            sc.append(_dot_nt(q4, k) + bias)
        m = [jnp.maximum(jnp.max(s, axis=-1, keepdims=True), sk) for s, sk in zip(sc, sinks)]
        p = [jnp.exp(s - mm) for s, mm in zip(sc, m)]
        den = [jnp.sum(pp, axis=-1, keepdims=True) + jnp.exp(sk - mm) for pp, sk, mm in zip(p, sinks, m)]
        o = []
        for hk in kvh:
            v = _window_rows(kvp_ref, kvc_ref, jb, slice((A_KV_HEADS + hk) * LANES, (A_KV_HEADS + hk + 1) * LANES))
            o.append(_dot(p[hk].astype(BF16), v) * (1.0 / den[hk]))
        for hk in kvh:
            for t in range(2):
                pair = jnp.where(low, o[hk][2 * t * BLOCK:(2 * t + 1) * BLOCK], o[hk][(2 * t + 1) * BLOCK:(2 * t + 2) * BLOCK])
                o_ref[rs, (2 * hk + t) * LANES:(2 * hk + t + 1) * LANES] = pair.astype(BF16)


def _attn_a(qa, kva, sinks, tile=512):
    bsz, s, _ = qa.shape
    tile = min(tile, s)
    kvw = kva.shape[2]
    per = tile // BLOCK
    return pl.pallas_call(
        _attn_a_kernel,
        grid=(bsz, s // tile),
        in_specs=[pl.BlockSpec(memory_space=pltpu.SMEM),
                  pl.BlockSpec((None, tile, A_Q_W), lambda b, i: (b, i, 0)),
                  pl.BlockSpec((None, BLOCK, kvw), lambda b, i: (b, jnp.maximum(i * per - 1, 0), 0)),
                  pl.BlockSpec((None, tile, kvw), lambda b, i: (b, i, 0))],
        out_specs=pl.BlockSpec((None, tile, A_Q_W), lambda b, i: (b, i, 0)),
        out_shape=jax.ShapeDtypeStruct((bsz, s, A_Q_W), BF16),
        compiler_params=_cparams("parallel", "parallel"),
        name="attn_a",
    )(sinks, qa, kva, kva)


def _attn_b_kernel(group, q_ref, kp_ref, kc_ref, vp_ref, vc_ref, o_ref, l_ref):
    nb = q_ref.shape[0] // BLOCK
    bias_first, bias_inner = _band_bias(BLOCK, BLOCK, pl.program_id(2) == 0)
    low = lax.broadcasted_iota(jnp.int32, (BLOCK, LANES), 1) < HEAD_DIM
    for j0 in range(0, nb, group):
        items = [(jb, h) for jb in range(j0, min(j0 + group, nb)) for h in range(B_HEADS)]
        sc = []
        for jb, h in items:
            lanes = slice((h // 2) * LANES, (h // 2 + 1) * LANES)
            qt = q_ref[jb * BLOCK:(jb + 1) * BLOCK, lanes]
            zero = jnp.zeros_like(qt)
            qh = jnp.where(low, qt, zero) if h % 2 == 0 else jnp.where(low, zero, qt)
            sc.append(_dot_nt(qh, _window_rows(kp_ref, kc_ref, jb, lanes)) + (bias_first if jb == 0 else bias_inner))
        m = [jnp.max(s, axis=-1, keepdims=True) for s in sc]
        p = [jnp.exp(s - mm) for s, mm in zip(sc, m)]
        den = [jnp.sum(pp, axis=-1, keepdims=True) for pp in p]
        o = [_dot(pp.astype(BF16), _window_rows(vp_ref, vc_ref, jb, slice((h // 2) * LANES, (h // 2 + 1) * LANES)))
             * (1.0 / dd) for pp, dd, (jb, h) in zip(p, den, items)]
        lse = [mm + jnp.log(dd) for mm, dd in zip(m, den)]
        for idx in range(0, len(items), 2):
            jb, h = items[idx]
            dst = (slice(jb * BLOCK, (jb + 1) * BLOCK), slice((h // 2) * LANES, (h // 2 + 1) * LANES))
            o_ref[dst] = jnp.where(low, o[idx], o[idx + 1])
            l_ref[dst] = jnp.where(low, lse[idx], lse[idx + 1])


def _attn_b(q, k, v, pat, dil, tile=512, group=2):
    bsz, length, _ = q.shape
    tile = min(tile, length)
    per = tile // BLOCK
    cur = pl.BlockSpec((None, tile, B_PW), lambda b, r, i: (b, i, r))
    prev = pl.BlockSpec((None, BLOCK, B_PW), lambda b, r, i: (b, jnp.maximum(i * per - 1, 0), r))
    return pl.pallas_call(
        functools.partial(_attn_b_kernel, group),
        grid=(bsz, dil, length // tile),
        in_specs=[cur, prev, cur, prev, cur],
        out_specs=[cur, cur],
        out_shape=[jax.ShapeDtypeStruct((bsz, length, dil * B_PW), F32)] * 2,
        compiler_params=_cparams("parallel", "parallel", "parallel"),
        name=f"attn_b{pat}",
    )(q, k, k, v, v)


def _attn_out_kernel(x_ref, oa_ref, o0_ref, o1_ref, o2_ref, l0_ref, l1_ref, l2_ref, w_ref, out_ref, buf):
    tm = x_ref.shape[0]

    def natural(ref, dil, slot):
        if dil == 1:
            return ref[...]
        n = tm // dil
        halves = B_PW // LANES
        for r in range(dil):
            for j in range(halves):
                buf[slot * halves + j, pl.ds(r, n, stride=dil), :] = ref[:, r * B_PW + j * LANES:r * B_PW + (j + 1) * LANES]
        return jnp.concatenate([buf[slot * halves + j] for j in range(halves)], axis=1)

    dils = [dil for _, dil in B_PATTERNS]
    o = [natural(ref, dil, i) for i, (ref, dil) in enumerate(zip((o0_ref, o1_ref, o2_ref), dils))]
    l = [natural(ref, dil, 3 + i) for i, (ref, dil) in enumerate(zip((l0_ref, l1_ref, l2_ref), dils))]
    m = jnp.maximum(jnp.maximum(l[0], l[1]), l[2])
    e = [jnp.exp(v - m) for v in l]
    ob = (e[0] * o[0] + e[1] * o[1] + e[2] * o[2]) / (e[0] + e[1] + e[2])
    mix = _dot(oa_ref[...], w_ref[:A_Q_W, :]) + _dot(ob.astype(BF16), w_ref[A_Q_W:, :])
    out_ref[...] = x_ref[...] + mix


def _attn_out(x, oa, obs, lses, w_out, tm=512):
    bsz, s, d = x.shape
    tok = lambda w: pl.BlockSpec((None, tm, w), lambda b, i: (b, i, 0))
    pat = [pl.BlockSpec((None, tm // dil, dil * B_PW), lambda b, i: (b, i, 0)) for _, dil in B_PATTERNS]
    return pl.pallas_call(
        _attn_out_kernel,
        grid=(bsz, s // tm),
        in_specs=[tok(d), tok(A_Q_W)] + pat + pat + [pl.BlockSpec(w_out.shape, lambda b, i: (0, 0))],
        out_specs=tok(d),
        out_shape=jax.ShapeDtypeStruct((bsz, s, d), F32),
        scratch_shapes=[pltpu.VMEM((2 * len(B_PATTERNS) * (B_PW // LANES), tm, LANES), F32)],
        compiler_params=_cparams("parallel", "parallel"),
        name="attn_out",
    )(x, oa, *obs, *lses, w_out)


def _ffn_kernel(final_norm, ff_chunk, x_ref, g_ref, wg_ref, wu_ref, wd_ref, gf_ref, out_ref, acc_ref):
    x = x_ref[...]
    h = _rms(x, g_ref[...]).astype(BF16)
    d_ff = wg_ref.shape[1]
    for c in range(d_ff // ff_chunk):
        sl = slice(c * ff_chunk, (c + 1) * ff_chunk)
        gate = _dot(h, wg_ref[:, sl])
        up = _dot(h, wu_ref[:, sl])
        act = (gate * _sigmoid(gate) * up).astype(BF16)
        contrib = _dot(act, wd_ref[sl, :])
        if c == 0:
            acc_ref[...] = x + contrib
        else:
            acc_ref[...] += contrib
    y = acc_ref[...]
    if final_norm:
        y = _rms(y, gf_ref[...])
    out_ref[...] = y


def _ffn(x, g, wg, wu, wd, g_final, final_norm, tm=512, ff_chunk=256):
    bsz, s, d = x.shape
    d_ff = wg.shape[1]
    tok = pl.BlockSpec((None, tm, d), lambda b, i: (b, i, 0))
    full = lambda shape: pl.BlockSpec(shape, lambda b, i: (0,) * len(shape))
    return pl.pallas_call(
        functools.partial(_ffn_kernel, final_norm, ff_chunk),
        grid=(bsz, s // tm),
        in_specs=[tok, full((1, d)), full((d, d_ff)), full((d, d_ff)), full((d_ff, d)), full((1, d))],
        out_specs=tok,
        out_shape=jax.ShapeDtypeStruct((bsz, s, d), F32),
        scratch_shapes=[pltpu.VMEM((tm, d), F32)],
        compiler_params=_cparams("parallel", "parallel"),
        name="ffn",
    )(x, g.reshape(1, d), wg, wu, wd, g_final.reshape(1, d))


def _rwkv_in_kernel(has_vlora, *refs):
    (x_ref, gn_ref, mu_ref, wr_ref, wk_ref, wv_ref, w0_ref, w1_ref, w2_ref, a0_ref, a1_ref, a2_ref,
     g1_ref, g2_ref, kk_ref, ka_ref) = refs[:16]
    refs = refs[16:]
    if has_vlora:
        v0_ref, v1_ref, v2_ref, vf_ref = refs[:4]
        refs = refs[4:]
    r_o, lw_o, k_o, v_o, kk_o, ab_o, g_o, carry, mix = refs
    tm = x_ref.shape[0]

    h = _rms(x_ref[...], gn_ref[...])

    @pl.when(pl.program_id(1) == 0)
    def _():
        carry[...] = jnp.zeros(carry.shape, F32)

    rolled = pltpu.roll(h, 1, 0)
    first = lax.broadcasted_iota(jnp.int32, (8, 1), 0) == 0
    hprev = jnp.concatenate([jnp.where(first, carry[7:8, :], rolled[0:8]), rolled[8:]], axis=0)
    carry[...] = h[tm - 8:tm, :]
    hb = h.astype(BF16)
    xxb = (hprev - h).astype(BF16)
    mub = mu_ref[...].astype(BF16)
    for i in range(mix.shape[0]):
        mix[i] = hb + xxb * mub[i:i + 1, :]

    def mixed(i):
        return mix[i]

    def lora(z, wa_ref, wb_ref, act=None):
        mid = _dot(z, wa_ref[...])
        if act is not None:
            mid = act(mid)
        return _dot(mid.astype(BF16), wb_ref[...])

    r = _dot(mixed(0), wr_ref[...])
    zw = w0_ref[...] + lora(mixed(1), w1_ref, w2_ref, jnp.tanh)
    lw = -DECAY_SCALE * _sigmoid(zw)
    k = _dot(mixed(2), wk_ref[...])
    xv = mixed(3)
    v = _dot(xv, wv_ref[...])
    if has_vlora:
        v = v + (vf_ref[...] - v) * _sigmoid(v0_ref[...] + lora(xv, v1_ref, v2_ref))
    a = _sigmoid(a0_ref[...] + lora(mixed(4), a1_ref, a2_ref))
    g = lora(mixed(5), g1_ref, g2_ref, _sigmoid)

    ones_bd = jnp.where(_head_mask(MXU_W), 1.0, 0.0).astype(BF16)
    kk = k * kk_ref[...]
    kk = kk / jnp.maximum(jnp.sqrt(_head_sum(kk * kk, ones_bd)), 1e-12)
    k = k * (1.0 + (a - 1.0) * ka_ref[...])

    r_o[...] = r
    lw_o[...] = lw
    k_o[...] = k
    v_o[...] = v
    kk_o[...] = kk
    ab_o[...] = kk * a
    g_o[...] = g.astype(BF16)


def _rwkv_in(x, gn, p, v_lora, v_first, tm=256):
    bsz, s, d = x.shape
    tok = pl.BlockSpec((None, tm, d), lambda b, i: (b, i, 0))
    full = lambda a: pl.BlockSpec(a.shape, lambda b, i: (0,) * a.ndim)
    row = lambda a: a.reshape(1, d)
    args = [x, row(gn), p["mu"], p["w_r"], p["w_k"], p["w_v"], row(p["w0"]), p["w1"], p["w2"], row(p["a0"]), p["a1"], p["a2"],
            p["g1"], p["g2"], row(p["k_k"]), row(p["k_a"])]
    specs = [tok] + [full(a) for a in args[1:]]
    if v_lora is not None:
        v0, v1, v2 = v_lora
        extra = [row(v0), v1, v2]
        args += extra + [v_first]
        specs += [full(a) for a in extra] + [tok]
    return pl.pallas_call(
        functools.partial(_rwkv_in_kernel, v_lora is not None),
        grid=(bsz, s // tm),
        in_specs=specs,
        out_specs=[tok] * 7,
        out_shape=[jax.ShapeDtypeStruct((bsz, s, d), F32)] * 6 + [jax.ShapeDtypeStruct((bsz, s, d), BF16)],
        scratch_shapes=[pltpu.VMEM((8, d), F32), pltpu.VMEM((p["mu"].shape[0], tm, d), BF16)],
        compiler_params=_cparams("parallel", "arbitrary"),
        name="rwkv_in",
    )(*args)


def _bd2(z):
    lane = lax.broadcasted_iota(jnp.int32, z.shape, 1)
    zero = jnp.zeros(z.shape, z.dtype)
    return jnp.concatenate([jnp.where(lane < HEAD_DIM, z, zero), jnp.where(lane >= HEAD_DIM, z, zero)], axis=0)


def _fold2(full):
    lane = lax.broadcasted_iota(jnp.int32, (HEAD_DIM, LANES), 1)
    return jnp.where(lane < HEAD_DIM, full[:HEAD_DIM], full[HEAD_DIM:])


def _wkv_prep_kernel(r_ref, lw_ref, k_ref, v_ref, kk_ref, ab_ref, rk_ref,
                     q_ref, y0_ref, mc_ref, z_ref, dec_ref, bonus_ref):
    cs = WKV_CHUNK
    rows, d = r_ref.shape
    nc = rows // cs
    ri = lax.broadcasted_iota(jnp.int32, (rows, rows), 0)
    ci = lax.broadcasted_iota(jnp.int32, (rows, rows), 1)
    tri = jnp.where((ri >= ci) & (ri // cs == ci // cs), 1.0, 0.0).astype(BF16)
    lw = lw_ref[...]
    hi = lw.astype(BF16)
    rem = lw - hi.astype(F32)
    mid = rem.astype(BF16)
    lo = (rem - mid.astype(F32)).astype(BF16)
    cum = _dot(tri, hi) + _dot(tri, mid) + _dot(tri, lo)
    totals = [cum[(j + 1) * cs - 1:(j + 1) * cs, :] for j in range(nc)]
    tot = jnp.concatenate([jnp.broadcast_to(t, (cs, d)) for t in totals], axis=0)
    e_neg = jnp.exp(-cum)
    e_tot = jnp.exp(tot - cum)
    kk, ab, kx = kk_ref[...], ab_ref[...], k_ref[...]
    at = (-kk * jnp.exp(cum - lw)).astype(BF16)
    rt = r_ref[...] * jnp.exp(cum)
    rtb = rt.astype(BF16)
    bt = (ab * e_neg).astype(BF16)
    kt = (kx * e_neg).astype(BF16)
    bh = (ab * e_tot).astype(BF16)
    kh = (kx * e_tot).astype(BF16)
    vb = v_ref[...].astype(BF16)
    ones_bd = jnp.where(_head_mask(MXU_W), 1.0, 0.0).astype(BF16)
    bonus_ref[...] = _head_sum(r_ref[...] * kx * rk_ref[...], ones_bd, split=False) * v_ref[...]

    trow = lax.broadcasted_iota(jnp.int32, (cs, LANES), 0)
    tcol = lax.broadcasted_iota(jnp.int32, (cs, LANES), 1) % HEAD_DIM
    strict = trow > tcol
    incl = trow >= tcol
    eye = jnp.where(trow == tcol, 1.0, 0.0)

    for j in range(nc):
        dec_ref[8 * j:8 * j + 8, :] = jnp.broadcast_to(jnp.exp(totals[j]), (8, d))

    chains = [(slice(j * cs, (j + 1) * cs), slice(p * LANES, (p + 1) * LANES))
              for j in range(nc) for p in range(d // LANES)]

    g12 = [_dot_nt(jnp.concatenate([at[c], rtb[c]], axis=0),
                   jnp.concatenate([_bd2(bt[c]), _bd2(kt[c])], axis=0)) for c in chains]
    a_ab = [jnp.where(strict, g[:cs, :LANES], 0.0) for g in g12]
    a_rb = [jnp.where(incl, g[cs:, :LANES], 0.0).astype(BF16) for g in g12]
    a_k = [jnp.concatenate([jnp.where(strict, g[:cs, LANES:], 0.0),
                            jnp.where(incl, g[cs:, LANES:], 0.0)], axis=0).astype(BF16) for g in g12]
    g4 = [_dot(a, _bd2(vb[c])) for a, c in zip(a_k, chains)]
    x = [eye + jnp.where((trow - tcol == 1) & (trow % 2 == 1), a, 0.0) for a in a_ab]
    size = 2
    while size < cs:
        lower_left = (trow // size - tcol // size == 1) & ((trow // size) % 2 == 1)
        xb = [v.astype(BF16) for v in x]
        xn = [_dot(v, _bd2(jnp.where(lower_left, a, 0.0).astype(BF16))) for v, a in zip(xb, a_ab)]
        x = [v + _dot(n.astype(BF16), _bd2(vb16)) for v, n, vb16 in zip(x, xn, xb)]
        size *= 2
    wu = [_dot(v.astype(BF16), jnp.concatenate([_bd2(at[c]), _bd2(g[:cs].astype(BF16))], axis=1))
          for v, g, c in zip(x, g4, chains)]
    wb = [v[:, :LANES].astype(BF16) for v in wu]
    ub = [v[:, LANES:].astype(BF16) for v in wu]
    qy = [_dot(a, jnp.concatenate([_bd2(w), _bd2(u)], axis=1)) for a, w, u in zip(a_rb, wb, ub)]
    mc = [_dot_tn(w, bh[c]) for w, c in zip(wb, chains)]
    zz = [_dot_tn(jnp.concatenate([u, vb[c]], axis=0), jnp.concatenate([bh[c], kh[c]], axis=0))
          for u, c in zip(ub, chains)]
    for i, c in enumerate(chains):
        q_ref[c] = (rt[c] + qy[i][:, :LANES]).astype(BF16)
        y0_ref[c] = g4[i][cs:] + qy[i][:, LANES:]
        mc_ref[c] = _fold2(mc[i]).astype(BF16)
        z_ref[c] = _fold2(zz[i])


def _wkv_out_kernel(q_ref, y0_ref, mc_ref, z_ref, dec_ref, x_ref, bonus_ref, g_ref, lnw_ref, lnb_ref, wo_ref,
                    out_ref, s_ref, y_buf):
    cs = WKV_CHUNK
    bsz, rows, d = q_ref.shape

    @pl.when(pl.program_id(0) == 0)
    def _():
        s_ref[...] = jnp.zeros(s_ref.shape, F32)

    seqs = [(b, slice(p * LANES, (p + 1) * LANES)) for b in range(bsz) for p in range(d // LANES)]
    state = [s_ref[b, :, ls] for b, ls in seqs]
    for j in range(rows // cs):
        rs = slice(j * cs, (j + 1) * cs)
        sb = [s.astype(BF16) for s in state]
        upd = [_dot(v, _bd2(mc_ref[b, rs, ls])) for v, (b, ls) in zip(sb, seqs)]
        for v, (b, ls) in zip(sb, seqs):
            y_buf[b * rows + j * cs:b * rows + (j + 1) * cs, ls] = _dot_nt(q_ref[b, rs, ls], _bd2(v)) + y0_ref[b, rs, ls]
        state = [s * dec_ref[b, 8 * j:8 * j + 1, ls] + u + z_ref[b, rs, ls]
                 for s, u, (b, ls) in zip(state, upd, seqs)]
    for s, (b, ls) in zip(state, seqs):
        s_ref[b, :, ls] = s

    ones_bd = jnp.where(_head_mask(MXU_W), 1.0, 0.0).astype(BF16)
    y = y_buf[...]
    dev = y - _head_sum(y, ones_bd, split=False) * (1.0 / HEAD_DIM)
    var = _head_sum(dev * dev, ones_bd, split=False) * (1.0 / HEAD_DIM)
    yn = dev * lax.rsqrt(var + LNX_EPS) * lnw_ref[...] + lnb_ref[...]
    gated = jnp.concatenate([((yn[b * rows:(b + 1) * rows] + bonus_ref[b]) * g_ref[b]).astype(BF16)
                             for b in range(bsz)], axis=0)
    proj = _dot(gated, wo_ref[...])
    for b in range(bsz):
        out_ref[b] = x_ref[b] + proj[b * rows:(b + 1) * rows]


def _wkv(x, r, lw, k, v, kk, ab, g, rk, lnw, lnb, wo, prep_chunks=2, scan_chunks=2):
    bsz, s, d = r.shape
    cs = WKV_CHUNK
    rows = prep_chunks * cs
    blk = pl.BlockSpec((None, rows, d), lambda b, c: (b, c, 0))
    dec_blk = pl.BlockSpec((None, 8 * prep_chunks, d), lambda b, c: (b, c, 0))
    act = lambda dt: jax.ShapeDtypeStruct((bsz, s, d), dt)
    q, y0, mc, z, dec, bonus = pl.pallas_call(
        _wkv_prep_kernel,
        grid=(bsz, s // rows),
        in_specs=[blk] * 6 + [pl.BlockSpec((1, d), lambda b, c: (0, 0))],
        out_specs=[blk, blk, blk, blk, dec_blk, blk],
        out_shape=[act(BF16), act(F32), act(BF16), act(F32),
                   jax.ShapeDtypeStruct((bsz, 8 * s // cs, d), F32), act(F32)],
        compiler_params=_cparams("parallel", "parallel"),
        name="wkv7_prep",
    )(r, lw, k, v, kk, ab, rk.reshape(1, d))
    rows = scan_chunks * cs
    blk = pl.BlockSpec((bsz, rows, d), lambda c: (0, c, 0))
    dec_blk = pl.BlockSpec((bsz, 8 * scan_chunks, d), lambda c: (0, c, 0))
    row = pl.BlockSpec((1, d), lambda c: (0, 0))
    return pl.pallas_call(
        _wkv_out_kernel,
        grid=(s // rows,),
        in_specs=[blk, blk, blk, blk, dec_blk, blk, blk, blk, row, row, pl.BlockSpec((d, d), lambda c: (0, 0))],
        out_specs=blk,
        out_shape=act(F32),
        scratch_shapes=[pltpu.VMEM((bsz, HEAD_DIM, d), F32), pltpu.VMEM((bsz * rows, d), F32)],
        compiler_params=_cparams("arbitrary"),
        name="wkv7_out",
    )(q, y0, mc, z, dec, x, bonus, g, lnw.reshape(1, d), lnb.reshape(1, d), wo)


def kernel(x, positions, norm_mix, norm_ffn, norm_final, attn_w_in, attn_b_in, attn_sinks, attn_w_out, rwkv_mu, rwkv_w_rkv, rwkv_w0, rwkv_w1, rwkv_w2, rwkv_a0, rwkv_a1, rwkv_a2, rwkv_g1, rwkv_g2, rwkv_k_k, rwkv_k_a, rwkv_r_k, rwkv_lnx_w, rwkv_lnx_b, rwkv_w_o, rwkv_v0, rwkv_v1, rwkv_v2, ffn_w_gate, ffn_w_up, ffn_w_down):
    depth = norm_mix.shape[0]
    bf = lambda a: a.astype(BF16)
    tables = _rope_tables(positions)
    v_first = None
    for layer in range(depth):
        i = layer // 2
        if layer % 2 == 0:
            qa, kva, qkv_b = _attn_in(x, norm_mix[layer], _widen_in_proj(bf(attn_w_in[i])),
                                      _widen_in_proj(attn_b_in[i]), tables)
            oa = _attn_a(qa, kva, attn_sinks[i])
            obs, lses = zip(*[_attn_b(*qkv_b[pat], pat, dil) for pat, (_, dil) in enumerate(B_PATTERNS)])
            x = _attn_out(x, oa, obs, lses, bf(attn_w_out[i]))
        else:
            p = dict(mu=rwkv_mu[i], w_r=bf(rwkv_w_rkv[i, 0]), w_k=bf(rwkv_w_rkv[i, 1]), w_v=bf(rwkv_w_rkv[i, 2]),
                     w0=rwkv_w0[i], w1=bf(rwkv_w1[i]), w2=bf(rwkv_w2[i]),
                     a0=rwkv_a0[i], a1=bf(rwkv_a1[i]), a2=bf(rwkv_a2[i]), g1=bf(rwkv_g1[i]), g2=bf(rwkv_g2[i]),
                     k_k=rwkv_k_k[i], k_a=rwkv_k_a[i])
            v_lora = None if i == 0 else (rwkv_v0[i - 1], bf(rwkv_v1[i - 1]), bf(rwkv_v2[i - 1]))
            r, lw, k, v, kk, ab, g = _rwkv_in(x, norm_mix[layer], p, v_lora, v_first)
            if i == 0:
                v_first = v
            x = _wkv(x, r, lw, k, v, kk, ab, g, rwkv_r_k[i], rwkv_lnx_w[i], rwkv_lnx_b[i], bf(rwkv_w_o[i]))
        x = _ffn(x, norm_ffn[layer], bf(ffn_w_gate[layer]), bf(ffn_w_up[layer]), bf(ffn_w_down[layer]),
                 norm_final, final_norm=(layer == depth - 1))
    return x
```

```python
import functools

import jax
import jax.numpy as jnp
from jax import lax
from jax.experimental import pallas as pl
from jax.experimental.pallas import tpu as pltpu

F32 = jnp.float32
BF16 = jnp.bfloat16

HEAD_DIM = 64
ROT_DIM = HEAD_DIM // 4
ROT_HALF = ROT_DIM // 2
ROPE_THETA = 500000.0
BLOCK = 128
NORM_EPS = 1e-5
LNX_EPS = 64e-5

A_Q_HEADS = 12
A_KV_HEADS = 3
A_GROUP = A_Q_HEADS // A_KV_HEADS
A_WINDOW = 128
B_PATTERNS = ((128, 1), (512, 4), (2048, 16))
B_HEADS = 4

A_Q_W = A_Q_HEADS * HEAD_DIM
A_KV_W = A_KV_HEADS * HEAD_DIM
B_PW = B_HEADS * HEAD_DIM
B_W = len(B_PATTERNS) * B_PW
KV_DUP_W = 2 * A_KV_W
OFF_KA = A_Q_W
OFF_VA = OFF_KA + KV_DUP_W
OFF_QB = OFF_VA + KV_DUP_W
OFF_KB = OFF_QB + B_W
OFF_VB = OFF_KB + B_W

LANES = 128
MXU_W = 256
WKV_CHUNK = 64
DECAY_SCALE = 0.6065306597126334
NEG_BIG = -1e30
VMEM_LIMIT = 56 * 1024 * 1024


def _cparams(*sem):
    return pltpu.CompilerParams(dimension_semantics=sem, vmem_limit_bytes=VMEM_LIMIT)


def _dot(a, b):
    return jnp.dot(a, b, preferred_element_type=F32)


def _dot_nt(a, b):
    return lax.dot_general(a, b, (((1,), (1,)), ((), ())), preferred_element_type=F32)


def _dot_tn(a, b):
    return lax.dot_general(a, b, (((0,), (0,)), ((), ())), preferred_element_type=F32)


def _row_sum(z):
    part = z[:, :LANES]
    for j in range(1, z.shape[1] // LANES):
        part = part + z[:, j * LANES:(j + 1) * LANES]
    ones = jnp.ones((LANES, LANES), BF16)
    hi = part.astype(BF16)
    lo = (part - hi.astype(F32)).astype(BF16)
    return _dot(hi, ones) + _dot(lo, ones)


def _rms(x, g):
    inv = lax.rsqrt(_row_sum(x * x) * (1.0 / x.shape[1]) + NORM_EPS)
    return x * jnp.concatenate([inv] * (x.shape[1] // LANES), axis=1) * g


def _sigmoid(z):
    return 1.0 / (1.0 + jnp.exp(-z))


def _head_mask(n):
    r = lax.broadcasted_iota(jnp.int32, (n, n), 0) // HEAD_DIM
    c = lax.broadcasted_iota(jnp.int32, (n, n), 1) // HEAD_DIM
    return r == c


def _head_sum(z, ones_bd, split=True):
    outs = []
    for j in range(z.shape[1] // MXU_W):
        zc = z[:, j * MXU_W:(j + 1) * MXU_W]
        hi = zc.astype(BF16)
        acc = _dot(hi, ones_bd)
        if split:
            acc = acc + _dot((zc - hi.astype(F32)).astype(BF16), ones_bd)
        outs.append(acc)
    return jnp.concatenate(outs, axis=1)


def _rope_trig_kernel(pos_ref, invf_ref, cos_ref, sin_ref):
    ang = pos_ref[...].astype(F32)[None] * invf_ref[...]
    cos_ref[...] = jnp.cos(ang)
    sin_ref[...] = jnp.sin(ang)


def _rope_tables(positions):
    bsz, s = positions.shape
    rows = bsz * s // LANES
    inv_freq = jnp.power(ROPE_THETA, -2.0 * jnp.arange(ROT_HALF, dtype=F32) / ROT_DIM)
    invf = jnp.broadcast_to(inv_freq[:, None, None], (ROT_HALF, 1, LANES))
    cos, sin = pl.pallas_call(
        _rope_trig_kernel,
        out_shape=(jax.ShapeDtypeStruct((ROT_HALF, rows, LANES), F32),) * 2,
        name="rope_trig",
    )(positions.reshape(rows, LANES), invf)
    cos = cos.reshape(ROT_HALF, bsz, s).transpose(1, 2, 0)
    sin = sin.reshape(ROT_HALF, bsz, s).transpose(1, 2, 0)
    ones = jnp.ones((bsz, s, HEAD_DIM - ROT_DIM), F32)
    z8 = jnp.zeros((bsz, s, ROT_HALF), F32)
    zrest = jnp.zeros((bsz, s, HEAD_DIM - ROT_DIM), F32)
    c = jnp.concatenate([cos, cos, ones], axis=-1)
    sa = jnp.concatenate([-sin, z8, zrest], axis=-1)
    sb = jnp.concatenate([z8, sin, zrest], axis=-1)
    return tuple(jnp.tile(t, (1, 1, LANES // HEAD_DIM)) for t in (c, sa, sb))


def _residue_major(ref, dil):
    n = ref.shape[0] // dil
    return jnp.concatenate([ref[pl.ds(r, n, stride=dil), :] for r in range(dil)], axis=0)


def _attn_in_kernel(x_ref, g_ref, w_ref, b_ref, c_ref, sa_ref, sb_ref, qa_ref, kva_ref, *rest):
    b_refs, hbuf = rest[:-1], rest[-1]
    hf = _rms(x_ref[...], g_ref[...])
    tm, d = hf.shape
    for j in range(d // LANES):
        hbuf[j] = hf[:, j * LANES:(j + 1) * LANES]
    scale = HEAD_DIM ** -0.5

    def proj(h, lo, width):
        return _dot(h, w_ref[:, lo:lo + width]) + b_ref[:, lo:lo + width]

    def rope(z, tabs):
        c, sa, sb = tabs
        return z * c + pltpu.roll(z, LANES - ROT_HALF, 1) * sa + pltpu.roll(z, ROT_HALF, 1) * sb

    def rope_all(z, tabs):
        return jnp.concatenate([rope(z[:, j * LANES:(j + 1) * LANES], tabs)
                                for j in range(z.shape[1] // LANES)], axis=1)

    h = hf.astype(BF16)
    tabs = (c_ref[...], sa_ref[...], sb_ref[...])
    for j in range(A_Q_W // MXU_W):
        qa_ref[:, j * MXU_W:(j + 1) * MXU_W] = (rope_all(proj(h, j * MXU_W, MXU_W), tabs) * scale).astype(BF16)
    kva_ref[:, :KV_DUP_W] = rope_all(proj(h, OFF_KA, KV_DUP_W), tabs).astype(BF16)
    kva_ref[:, KV_DUP_W:] = proj(h, OFF_VA, KV_DUP_W).astype(BF16)
    for pat, (_, dil) in enumerate(B_PATTERNS):
        q_ref, k_ref, v_ref = b_refs[3 * pat:3 * pat + 3]
        if dil > 1:
            h = jnp.concatenate([_residue_major(hbuf.at[j], dil) for j in range(d // LANES)],
                                axis=1).astype(BF16)
            tabs = tuple(_residue_major(t, dil) for t in (c_ref, sa_ref, sb_ref))
        q = (rope_all(proj(h, OFF_QB + pat * B_PW, B_PW), tabs) * scale).astype(BF16)
        k = rope_all(proj(h, OFF_KB + pat * B_PW, B_PW), tabs).astype(BF16)
        v = proj(h, OFF_VB + pat * B_PW, B_PW).astype(BF16)
        n = tm // dil
        for r in range(dil):
            q_ref[:, r * B_PW:(r + 1) * B_PW] = q[r * n:(r + 1) * n]
            k_ref[:, r * B_PW:(r + 1) * B_PW] = k[r * n:(r + 1) * n]
            v_ref[:, r * B_PW:(r + 1) * B_PW] = v[r * n:(r + 1) * n]


def _widen_in_proj(w):
    head = lambda off, h: w[..., off + h * HEAD_DIM:off + (h + 1) * HEAD_DIM]
    dup = lambda off: [head(off, h) for h in range(A_KV_HEADS) for _ in range(2)]
    return jnp.concatenate([w[..., :A_Q_W]] + dup(A_Q_W) + dup(A_Q_W + A_KV_W) + [w[..., A_Q_W + 2 * A_KV_W:]], axis=-1)


def _attn_in(x, g, w_in, b_in, tables, tm=512):
    bsz, s, d = x.shape
    in_w = w_in.shape[1]
    tok = lambda w: pl.BlockSpec((None, tm, w), lambda b, i: (b, i, 0))
    full = lambda shape: pl.BlockSpec(shape, lambda b, i: (0,) * len(shape))
    out_specs = [tok(A_Q_W), tok(2 * KV_DUP_W)]
    out_shape = [jax.ShapeDtypeStruct((bsz, s, A_Q_W), BF16), jax.ShapeDtypeStruct((bsz, s, 2 * KV_DUP_W), BF16)]
    for _, dil in B_PATTERNS:
        out_specs += [pl.BlockSpec((None, tm // dil, dil * B_PW), lambda b, i: (b, i, 0))] * 3
        out_shape += [jax.ShapeDtypeStruct((bsz, s // dil, dil * B_PW), BF16)] * 3
    outs = pl.pallas_call(
        _attn_in_kernel,
        grid=(bsz, s // tm),
        in_specs=[tok(d), full((1, d)), full((d, in_w)), full((1, in_w)),
                  tok(LANES), tok(LANES), tok(LANES)],
        out_specs=out_specs,
        out_shape=out_shape,
        scratch_shapes=[pltpu.VMEM((d // LANES, tm, LANES), F32)],
        compiler_params=_cparams("parallel", "parallel"),
        name="attn_in",
    )(x, g.reshape(1, d), w_in, b_in.reshape(1, in_w), *tables)
    return outs[0], outs[1], [outs[2 + 3 * p:5 + 3 * p] for p in range(len(B_PATTERNS))]


def _band_bias(rows, max_dist, first_tile):
    qi = lax.broadcasted_iota(jnp.int32, (rows, 2 * BLOCK), 0) % BLOCK
    kj = lax.broadcasted_iota(jnp.int32, (rows, 2 * BLOCK), 1)
    dist = BLOCK + qi - kj
    band = (dist >= 0) & (dist <= max_dist)
    inner = jnp.where(band, 0.0, NEG_BIG)
    return jnp.where(band & ((kj >= BLOCK) | jnp.logical_not(first_tile)), 0.0, NEG_BIG), inner


def _window_rows(prev_ref, cur_ref, jb, lanes):
    if jb == 0:
        return jnp.concatenate([prev_ref[:, lanes], cur_ref[0:BLOCK, lanes]], axis=0)
    return cur_ref[(jb - 1) * BLOCK:(jb + 1) * BLOCK, lanes]


def _attn_a_kernel(sink_ref, q_ref, kvp_ref, kvc_ref, o_ref):
    nb = q_ref.shape[0] // BLOCK
    rows = A_GROUP * BLOCK
    bias_first, bias_inner = _band_bias(rows, A_WINDOW - 1, pl.program_id(1) == 0)
    low = lax.broadcasted_iota(jnp.int32, (BLOCK, LANES), 1) < HEAD_DIM
    rowg = lax.broadcasted_iota(jnp.int32, (rows, 1), 0) // BLOCK
    sinks = []
    for hk in range(A_KV_HEADS):
        sk = jnp.zeros((rows, 1), F32)
        for g in range(A_GROUP):
            sk = jnp.where(rowg == g, sink_ref[hk * A_GROUP + g], sk)
        sinks.append(sk)
    kvh = range(A_KV_HEADS)
    for jb in range(nb):
        rs = slice(jb * BLOCK, (jb + 1) * BLOCK)
        bias = bias_first if jb == 0 else bias_inner
        sc = []
        for hk in kvh:
            tiles = [q_ref[rs, (2 * hk + t) * LANES:(2 * hk + t + 1) * LANES] for t in range(2)]
            zero = jnp.zeros_like(tiles[0])
            q4 = jnp.concatenate([jnp.where(low, tiles[0], zero), jnp.where(low, zero, tiles[0]),
                                  jnp.where(low, tiles[1], zero), jnp.where(low, zero, tiles[1])], axis=0)
            k = _window_rows(kvp_ref, kvc_ref, jb, slice(hk * LANES, (hk + 1) * LANES))
            sc.append(_dot_nt(q4, k) + bias)
        m = [jnp.maximum(jnp.max(s, axis=-1, keepdims=True), sk) for s, sk in zip(sc, sinks)]
        p = [jnp.exp(s - mm) for s, mm in zip(sc, m)]
        den = [jnp.sum(pp, axis=-1, keepdims=True) + jnp.exp(sk - mm) for pp, sk, mm in zip(p, sinks, m)]
        o = []
        for hk in kvh:
            v = _window_rows(kvp_ref, kvc_ref, jb, slice((A_KV_HEADS + hk) * LANES, (A_KV_HEADS + hk + 1) * LANES))
            o.append(_dot(p[hk].astype(BF16), v) * (1.0 / den[hk]))
        for hk in kvh:
            for t in range(2):
                pair = jnp.where(low, o[hk][2 * t * BLOCK:(2 * t + 1) * BLOCK], o[hk][(2 * t + 1) * BLOCK:(2 * t + 2) * BLOCK])
                o_ref[rs, (2 * hk + t) * LANES:(2 * hk + t + 1) * LANES] = pair.astype(BF16)


def _attn_a(qa, kva, sinks, tile=512):
    bsz, s, _ = qa.shape
    tile = min(tile, s)
    kvw = kva.shape[2]
    per = tile // BLOCK
    return pl.pallas_call(
        _attn_a_kernel,
        grid=(bsz, s // tile),
        in_specs=[pl.BlockSpec(memory_space=pltpu.SMEM),
                  pl.BlockSpec((None, tile, A_Q_W), lambda b, i: (b, i, 0)),
                  pl.BlockSpec((None, BLOCK, kvw), lambda b, i: (b, jnp.maximum(i * per - 1, 0), 0)),
                  pl.BlockSpec((None, tile, kvw), lambda b, i: (b, i, 0))],
        out_specs=pl.BlockSpec((None, tile, A_Q_W), lambda b, i: (b, i, 0)),
        out_shape=jax.ShapeDtypeStruct((bsz, s, A_Q_W), BF16),
        compiler_params=_cparams("parallel", "parallel"),
        name="attn_a",
    )(sinks, qa, kva, kva)


def _attn_b_kernel(group, q_ref, kp_ref, kc_ref, vp_ref, vc_ref, o_ref, l_ref):
    nb = q_ref.shape[0] // BLOCK
    bias_first, bias_inner = _band_bias(BLOCK, BLOCK, pl.program_id(2) == 0)
    low = lax.broadcasted_iota(jnp.int32, (BLOCK, LANES), 1) < HEAD_DIM
    for j0 in range(0, nb, group):
        items = [(jb, h) for jb in range(j0, min(j0 + group, nb)) for h in range(B_HEADS)]
        sc = []
        for jb, h in items:
            lanes = slice((h // 2) * LANES, (h // 2 + 1) * LANES)
            qt = q_ref[jb * BLOCK:(jb + 1) * BLOCK, lanes]
            zero = jnp.zeros_like(qt)
            qh = jnp.where(low, qt, zero) if h % 2 == 0 else jnp.where(low, zero, qt)
            sc.append(_dot_nt(qh, _window_rows(kp_ref, kc_ref, jb, lanes)) + (bias_first if jb == 0 else bias_inner))
        m = [jnp.max(s, axis=-1, keepdims=True) for s in sc]
        p = [jnp.exp(s - mm) for s, mm in zip(sc, m)]
        den = [jnp.sum(pp, axis=-1, keepdims=True) for pp in p]
        o = [_dot(pp.astype(BF16), _window_rows(vp_ref, vc_ref, jb, slice((h // 2) * LANES, (h // 2 + 1) * LANES)))
             * (1.0 / dd) for pp, dd, (jb, h) in zip(p, den, items)]
        lse = [mm + jnp.log(dd) for mm, dd in zip(m, den)]
        for idx in range(0, len(items), 2):
            jb, h = items[idx]
            dst = (slice(jb * BLOCK, (jb + 1) * BLOCK), slice((h // 2) * LANES, (h // 2 + 1) * LANES))
            o_ref[dst] = jnp.where(low, o[idx], o[idx + 1])
            l_ref[dst] = jnp.where(low, lse[idx], lse[idx + 1])


def _attn_b(q, k, v, pat, dil, tile=512, group=2):
    bsz, length, _ = q.shape
    tile = min(tile, length)
    per = tile // BLOCK
    cur = pl.BlockSpec((None, tile, B_PW), lambda b, r, i: (b, i, r))
    prev = pl.BlockSpec((None, BLOCK, B_PW), lambda b, r, i: (b, jnp.maximum(i * per - 1, 0), r))
    return pl.pallas_call(
        functools.partial(_attn_b_kernel, group),
        grid=(bsz, dil, length // tile),
        in_specs=[cur, prev, cur, prev, cur],
        out_specs=[cur, cur],
        out_shape=[jax.ShapeDtypeStruct((bsz, length, dil * B_PW), F32)] * 2,
        compiler_params=_cparams("parallel", "parallel", "parallel"),
        name=f"attn_b{pat}",
    )(q, k, k, v, v)


def _attn_out_kernel(x_ref, oa_ref, o0_ref, o1_ref, o2_ref, l0_ref, l1_ref, l2_ref, w_ref,
                     gf_ref, wg_ref, wu_ref, wd_ref, out_ref, buf, acc_ref):
    tm = x_ref.shape[0]

    def natural(ref, dil, slot):
        if dil == 1:
            return ref[...]
        n = tm // dil
        halves = B_PW // LANES
        for r in range(dil):
            for j in range(halves):
                buf[slot * halves + j, pl.ds(r, n, stride=dil), :] = ref[:, r * B_PW + j * LANES:r * B_PW + (j + 1) * LANES]
        return jnp.concatenate([buf[slot * halves + j] for j in range(halves)], axis=1)

    dils = [dil for _, dil in B_PATTERNS]
    o = [natural(ref, dil, i) for i, (ref, dil) in enumerate(zip((o0_ref, o1_ref, o2_ref), dils))]
    l = [natural(ref, dil, 3 + i) for i, (ref, dil) in enumerate(zip((l0_ref, l1_ref, l2_ref), dils))]
    m = jnp.maximum(jnp.maximum(l[0], l[1]), l[2])
    e = [jnp.exp(v - m) for v in l]
    ob = (e[0] * o[0] + e[1] * o[1] + e[2] * o[2]) / (e[0] + e[1] + e[2])
    mix = _dot(oa_ref[...], w_ref[:A_Q_W, :]) + _dot(ob.astype(BF16), w_ref[A_Q_W:, :])
    out_ref[...] = _ffn_block(x_ref[...] + mix, gf_ref, wg_ref, wu_ref, wd_ref, acc_ref)


def _attn_out(x, oa, obs, lses, w_out, ffn, tm=512):
    bsz, s, d = x.shape
    tok = lambda w: pl.BlockSpec((None, tm, w), lambda b, i: (b, i, 0))
    pat = [pl.BlockSpec((None, tm // dil, dil * B_PW), lambda b, i: (b, i, 0)) for _, dil in B_PATTERNS]
    return pl.pallas_call(
        _attn_out_kernel,
        grid=(bsz, s // tm),
        in_specs=[tok(d), tok(A_Q_W)] + pat + pat + [_resident(a.shape) for a in (w_out,) + ffn],
        out_specs=tok(d),
        out_shape=jax.ShapeDtypeStruct((bsz, s, d), F32),
        scratch_shapes=[pltpu.VMEM((2 * len(B_PATTERNS) * (B_PW // LANES), tm, LANES), F32),
                        pltpu.VMEM((tm, d), F32)],
        compiler_params=_cparams("parallel", "parallel"),
        name="attn_out_ffn",
    )(x, oa, *obs, *lses, w_out, *ffn)


FF_CHUNK = 256


def _ffn_block(x, g_ref, wg_ref, wu_ref, wd_ref, acc_ref):
    h = _rms(x, g_ref[...]).astype(BF16)
    for c in range(wg_ref.shape[1] // FF_CHUNK):
        sl = slice(c * FF_CHUNK, (c + 1) * FF_CHUNK)
        gate = _dot(h, wg_ref[:, sl])
        up = _dot(h, wu_ref[:, sl])
        contrib = _dot((gate * _sigmoid(gate) * up).astype(BF16), wd_ref[sl, :])
        if c == 0:
            acc_ref[...] = x + contrib
        else:
            acc_ref[...] += contrib
    return acc_ref[...]


def _resident(shape):
    return pl.BlockSpec(shape, lambda *_: (0,) * len(shape), pipeline_mode=pl.Buffered(1))


def _rwkv_in_kernel(has_vlora, halves, *refs):
    (x_ref, gn_ref, mu_ref, wr_ref, wk_ref, wv_ref, w0_ref, w1_ref, w2_ref, a0_ref, a1_ref, a2_ref,
     g1_ref, g2_ref, kk_ref, ka_ref) = refs[:16]
    refs = refs[16:]
    if has_vlora:
        v0_ref, v1_ref, v2_ref, vf_ref = refs[:4]
        refs = refs[4:]
    r_o, lw_o, k_o, v_o, kk_o, ab_o, g_o, carry, mix = refs
    hr = x_ref.shape[0] // halves
    parts = [slice(hh * hr, (hh + 1) * hr) for hh in range(halves)]

    @pl.when(pl.program_id(1) == 0)
    def _():
        carry[...] = jnp.zeros(carry.shape, F32)

    first = lax.broadcasted_iota(jnp.int32, (8, 1), 0) == 0
    mub = mu_ref[...].astype(BF16)
    last8 = carry[...]
    for rs in parts:
        h = _rms(x_ref[rs, :], gn_ref[...])
        rolled = pltpu.roll(h, 1, 0)
        hprev = jnp.concatenate([jnp.where(first, last8[7:8, :], rolled[0:8]), rolled[8:]], axis=0)
        last8 = h[hr - 8:hr, :]
        hb = h.astype(BF16)
        xxb = (hprev - h).astype(BF16)
        for i in range(mix.shape[0]):
            mix[i, rs, :] = hb + xxb * mub[i:i + 1, :]
    carry[...] = last8

    def lora(z, wa_ref, wb_ref, act=None):
        mid = _dot(z, wa_ref[...])
        if act is not None:
            mid = act(mid)
        return _dot(mid.astype(BF16), wb_ref[...])

    for rs in parts:
        r_o[rs, :] = _dot(mix[0, rs, :], wr_ref[...])
        zw = w0_ref[...] + lora(mix[1, rs, :], w1_ref, w2_ref, jnp.tanh)
        lw_o[rs, :] = -DECAY_SCALE * _sigmoid(zw)
        k_o[rs, :] = _dot(mix[2, rs, :], wk_ref[...])
        xv = mix[3, rs, :]
        v = _dot(xv, wv_ref[...])
        if has_vlora:
            v = v + (vf_ref[rs, :] - v) * _sigmoid(v0_ref[...] + lora(xv, v1_ref, v2_ref))
        v_o[rs, :] = v
        ab_o[rs, :] = _sigmoid(a0_ref[...] + lora(mix[4, rs, :], a1_ref, a2_ref))
        g_o[rs, :] = lora(mix[5, rs, :], g1_ref, g2_ref, _sigmoid).astype(BF16)

    ones_bd = jnp.where(_head_mask(MXU_W), 1.0, 0.0).astype(BF16)
    for rs in parts:
        k = k_o[rs, :]
        a = ab_o[rs, :]
        kk = k * kk_ref[...]
        kk = kk / jnp.maximum(jnp.sqrt(_head_sum(kk * kk, ones_bd)), 1e-12)
        kk_o[rs, :] = kk
        ab_o[rs, :] = kk * a
        k_o[rs, :] = k * (1.0 + (a - 1.0) * ka_ref[...])


def _rwkv_in(x, gn, p, v_lora, v_first, tm=512, halves=1):
    bsz, s, d = x.shape
    tok = pl.BlockSpec((None, tm, d), lambda b, i: (b, i, 0))
    row = lambda a: a.reshape(1, d)
    args = [x, row(gn), p["mu"], p["w_r"], p["w_k"], p["w_v"], row(p["w0"]), p["w1"], p["w2"], row(p["a0"]), p["a1"], p["a2"],
            p["g1"], p["g2"], row(p["k_k"]), row(p["k_a"])]
    specs = [tok] + [_resident(a.shape) for a in args[1:]]
    if v_lora is not None:
        v0, v1, v2 = v_lora
        extra = [row(v0), v1, v2]
        args += extra + [v_first]
        specs += [_resident(a.shape) for a in extra] + [tok]
    return pl.pallas_call(
        functools.partial(_rwkv_in_kernel, v_lora is not None, halves),
        grid=(bsz, s // tm),
        in_specs=specs,
        out_specs=[tok] * 7,
        out_shape=[jax.ShapeDtypeStruct((bsz, s, d), F32)] * 6 + [jax.ShapeDtypeStruct((bsz, s, d), BF16)],
        scratch_shapes=[pltpu.VMEM((8, d), F32), pltpu.VMEM((p["mu"].shape[0], tm, d), BF16)],
        compiler_params=_cparams("parallel", "arbitrary"),
        name="rwkv_in",
    )(*args)


def _bd2(z):
    lane = lax.broadcasted_iota(jnp.int32, z.shape, 1)
    zero = jnp.zeros(z.shape, z.dtype)
    return jnp.concatenate([jnp.where(lane < HEAD_DIM, z, zero), jnp.where(lane >= HEAD_DIM, z, zero)], axis=0)


def _fold2(full):
    lane = lax.broadcasted_iota(jnp.int32, (HEAD_DIM, LANES), 1)
    return jnp.where(lane < HEAD_DIM, full[:HEAD_DIM], full[HEAD_DIM:])


def _wkv_prep_kernel(r_ref, lw_ref, k_ref, v_ref, kk_ref, ab_ref, rk_ref,
                     q_ref, y0_ref, mc_ref, z_ref, dec_ref, bonus_ref):
    cs = WKV_CHUNK
    rows, d = r_ref.shape
    nc = rows // cs
    ri = lax.broadcasted_iota(jnp.int32, (rows, rows), 0)
    ci = lax.broadcasted_iota(jnp.int32, (rows, rows), 1)
    tri = jnp.where((ri >= ci) & (ri // cs == ci // cs), 1.0, 0.0).astype(BF16)
    lw = lw_ref[...]
    hi = lw.astype(BF16)
    rem = lw - hi.astype(F32)
    mid = rem.astype(BF16)
    lo = (rem - mid.astype(F32)).astype(BF16)
    cum = _dot(tri, hi) + _dot(tri, mid) + _dot(tri, lo)
    totals = [cum[(j + 1) * cs - 1:(j + 1) * cs, :] for j in range(nc)]
    tot = jnp.concatenate([jnp.broadcast_to(t, (cs, d)) for t in totals], axis=0)
    e_neg = jnp.exp(-cum)
    e_tot = jnp.exp(tot - cum)
    kk, ab, kx = kk_ref[...], ab_ref[...], k_ref[...]
    at = (-kk * jnp.exp(cum - lw)).astype(BF16)
    rt = r_ref[...] * jnp.exp(cum)
    rtb = rt.astype(BF16)
    bt = (ab * e_neg).astype(BF16)
    kt = (kx * e_neg).astype(BF16)
    bh = (ab * e_tot).astype(BF16)
    kh = (kx * e_tot).astype(BF16)
    vb = v_ref[...].astype(BF16)
    ones_bd = jnp.where(_head_mask(MXU_W), 1.0, 0.0).astype(BF16)
    bonus_ref[...] = _head_sum(r_ref[...] * kx * rk_ref[...], ones_bd, split=False) * v_ref[...]

    trow = lax.broadcasted_iota(jnp.int32, (cs, LANES), 0)
    tcol = lax.broadcasted_iota(jnp.int32, (cs, LANES), 1) % HEAD_DIM
    strict = trow > tcol
    incl = trow >= tcol
    eye = jnp.where(trow == tcol, 1.0, 0.0)

    for j in range(nc):
        dec_ref[8 * j:8 * j + 8, :] = jnp.broadcast_to(jnp.exp(totals[j]), (8, d))

    chains = [(slice(j * cs, (j + 1) * cs), slice(p * LANES, (p + 1) * LANES))
              for j in range(nc) for p in range(d // LANES)]

    g12 = [_dot_nt(jnp.concatenate([at[c], rtb[c]], axis=0),
                   jnp.concatenate([_bd2(bt[c]), _bd2(kt[c])], axis=0)) for c in chains]
    a_ab = [jnp.where(strict, g[:cs, :LANES], 0.0) for g in g12]
    a_rb = [jnp.where(incl, g[cs:, :LANES], 0.0).astype(BF16) for g in g12]
    a_k = [jnp.concatenate([jnp.where(strict, g[:cs, LANES:], 0.0),
                            jnp.where(incl, g[cs:, LANES:], 0.0)], axis=0).astype(BF16) for g in g12]
    g4 = [_dot(a, _bd2(vb[c])) for a, c in zip(a_k, chains)]
    x = [eye + jnp.where((trow - tcol == 1) & (trow % 2 == 1), a, 0.0) for a in a_ab]
    size = 2
    while size < cs:
        lower_left = (trow // size - tcol // size == 1) & ((trow // size) % 2 == 1)
        xb = [v.astype(BF16) for v in x]
        xn = [_dot(v, _bd2(jnp.where(lower_left, a, 0.0).astype(BF16))) for v, a in zip(xb, a_ab)]
        x = [v + _dot(n.astype(BF16), _bd2(vb16)) for v, n, vb16 in zip(x, xn, xb)]
        size *= 2
    wu = [_dot(v.astype(BF16), jnp.concatenate([_bd2(at[c]), _bd2(g[:cs].astype(BF16))], axis=1))
          for v, g, c in zip(x, g4, chains)]
    wb = [v[:, :LANES].astype(BF16) for v in wu]
    ub = [v[:, LANES:].astype(BF16) for v in wu]
    qy = [_dot(a, jnp.concatenate([_bd2(w), _bd2(u)], axis=1)) for a, w, u in zip(a_rb, wb, ub)]
    mc = [_dot_tn(w, bh[c]) for w, c in zip(wb, chains)]
    zz = [_dot_tn(jnp.concatenate([u, vb[c]], axis=0), jnp.concatenate([bh[c], kh[c]], axis=0))
          for u, c in zip(ub, chains)]
    for i, c in enumerate(chains):
        q_ref[c] = (rt[c] + qy[i][:, :LANES]).astype(BF16)
        y0_ref[c] = g4[i][cs:] + qy[i][:, LANES:]
        mc_ref[c] = _fold2(mc[i]).astype(BF16)
        z_ref[c] = _fold2(zz[i])


def _wkv_out_kernel(final_norm, q_ref, y0_ref, mc_ref, z_ref, dec_ref, x_ref, bonus_ref, g_ref, lnw_ref, lnb_ref,
                    wo_ref, gf_ref, wg_ref, wu_ref, wd_ref, gl_ref, out_ref, s_ref, y_buf, acc_ref):
    cs = WKV_CHUNK
    bsz, rows, d = q_ref.shape

    @pl.when(pl.program_id(0) == 0)
    def _():
        s_ref[...] = jnp.zeros(s_ref.shape, F32)

    seqs = [(b, slice(p * LANES, (p + 1) * LANES)) for b in range(bsz) for p in range(d // LANES)]
    state = [s_ref[b, :, ls] for b, ls in seqs]
    for j in range(rows // cs):
        rs = slice(j * cs, (j + 1) * cs)
        sb = [s.astype(BF16) for s in state]
        upd = [_dot(v, _bd2(mc_ref[b, rs, ls])) for v, (b, ls) in zip(sb, seqs)]
        for v, (b, ls) in zip(sb, seqs):
            y_buf[b * rows + j * cs:b * rows + (j + 1) * cs, ls] = _dot_nt(q_ref[b, rs, ls], _bd2(v)) + y0_ref[b, rs, ls]
        state = [s * dec_ref[b, 8 * j:8 * j + 1, ls] + u + z_ref[b, rs, ls]
                 for s, u, (b, ls) in zip(state, upd, seqs)]
    for s, (b, ls) in zip(state, seqs):
        s_ref[b, :, ls] = s

    ones_bd = jnp.where(_head_mask(MXU_W), 1.0, 0.0).astype(BF16)
    y = y_buf[...]
    dev = y - _head_sum(y, ones_bd, split=False) * (1.0 / HEAD_DIM)
    var = _head_sum(dev * dev, ones_bd, split=False) * (1.0 / HEAD_DIM)
    yn = dev * lax.rsqrt(var + LNX_EPS) * lnw_ref[...] + lnb_ref[...]
    gated = jnp.concatenate([((yn[b * rows:(b + 1) * rows] + bonus_ref[b]) * g_ref[b]).astype(BF16)
                             for b in range(bsz)], axis=0)
    x1 = jnp.concatenate([x_ref[b] for b in range(bsz)], axis=0) + _dot(gated, wo_ref[...])
    out = _ffn_block(x1, gf_ref, wg_ref, wu_ref, wd_ref, acc_ref)
    if final_norm:
        out = _rms(out, gl_ref[...])
    for b in range(bsz):
        out_ref[b] = out[b * rows:(b + 1) * rows]


def _wkv(x, r, lw, k, v, kk, ab, g, rk, lnw, lnb, wo, ffn, g_last, final_norm, prep_chunks=2, scan_chunks=2):
    bsz, s, d = r.shape
    cs = WKV_CHUNK
    rows = prep_chunks * cs
    blk = pl.BlockSpec((None, rows, d), lambda b, c: (b, c, 0))
    dec_blk = pl.BlockSpec((None, 8 * prep_chunks, d), lambda b, c: (b, c, 0))
    act = lambda dt: jax.ShapeDtypeStruct((bsz, s, d), dt)
    q, y0, mc, z, dec, bonus = pl.pallas_call(
        _wkv_prep_kernel,
        grid=(bsz, s // rows),
        in_specs=[blk] * 6 + [pl.BlockSpec((1, d), lambda b, c: (0, 0))],
        out_specs=[blk, blk, blk, blk, dec_blk, blk],
        out_shape=[act(BF16), act(F32), act(BF16), act(F32),
                   jax.ShapeDtypeStruct((bsz, 8 * s // cs, d), F32), act(F32)],
        compiler_params=_cparams("parallel", "parallel"),
        name="wkv7_prep",
    )(r, lw, k, v, kk, ab, rk.reshape(1, d))
    rows = scan_chunks * cs
    blk = pl.BlockSpec((bsz, rows, d), lambda c: (0, c, 0))
    dec_blk = pl.BlockSpec((bsz, 8 * scan_chunks, d), lambda c: (0, c, 0))
    row = pl.BlockSpec((1, d), lambda c: (0, 0))
    return pl.pallas_call(
        functools.partial(_wkv_out_kernel, final_norm),
        grid=(s // rows,),
        in_specs=[blk, blk, blk, blk, dec_blk, blk, blk, blk, row, row]
                 + [_resident(a.shape) for a in (wo,) + ffn] + [row],
        out_specs=blk,
        out_shape=act(F32),
        scratch_shapes=[pltpu.VMEM((bsz, HEAD_DIM, d), F32), pltpu.VMEM((bsz * rows, d), F32),
                        pltpu.VMEM((bsz * rows, d), F32)],
        compiler_params=_cparams("arbitrary"),
        name="wkv7_out_ffn",
    )(q, y0, mc, z, dec, x, bonus, g, lnw.reshape(1, d), lnb.reshape(1, d), wo, *ffn, g_last.reshape(1, d))


def kernel(x, positions, norm_mix, norm_ffn, norm_final, attn_w_in, attn_b_in, attn_sinks, attn_w_out, rwkv_mu, rwkv_w_rkv, rwkv_w0, rwkv_w1, rwkv_w2, rwkv_a0, rwkv_a1, rwkv_a2, rwkv_g1, rwkv_g2, rwkv_k_k, rwkv_k_a, rwkv_r_k, rwkv_lnx_w, rwkv_lnx_b, rwkv_w_o, rwkv_v0, rwkv_v1, rwkv_v2, ffn_w_gate, ffn_w_up, ffn_w_down):
    depth = norm_mix.shape[0]
    bf = lambda a: a.astype(BF16)
    tables = _rope_tables(positions)
    v_first = None
    for layer in range(depth):
        i = layer // 2
        ffn = (norm_ffn[layer].reshape(1, -1), bf(ffn_w_gate[layer]), bf(ffn_w_up[layer]), bf(ffn_w_down[layer]))
        if layer % 2 == 0:
            qa, kva, qkv_b = _attn_in(x, norm_mix[layer], _widen_in_proj(bf(attn_w_in[i])),
                                      _widen_in_proj(attn_b_in[i]), tables)
            oa = _attn_a(qa, kva, attn_sinks[i])
            obs, lses = zip(*[_attn_b(*qkv_b[pat], pat, dil) for pat, (_, dil) in enumerate(B_PATTERNS)])
            x = _attn_out(x, oa, obs, lses, bf(attn_w_out[i]), ffn)
        else:
            p = dict(mu=rwkv_mu[i], w_r=bf(rwkv_w_rkv[i, 0]), w_k=bf(rwkv_w_rkv[i, 1]), w_v=bf(rwkv_w_rkv[i, 2]),
                     w0=rwkv_w0[i], w1=bf(rwkv_w1[i]), w2=bf(rwkv_w2[i]),
                     a0=rwkv_a0[i], a1=bf(rwkv_a1[i]), a2=bf(rwkv_a2[i]), g1=bf(rwkv_g1[i]), g2=bf(rwkv_g2[i]),
                     k_k=rwkv_k_k[i], k_a=rwkv_k_a[i])
            v_lora = None if i == 0 else (rwkv_v0[i - 1], bf(rwkv_v1[i - 1]), bf(rwkv_v2[i - 1]))
            r, lw, k, v, kk, ab, g = _rwkv_in(x, norm_mix[layer], p, v_lora, v_first)
            if i == 0:
                v_first = v
            x = _wkv(x, r, lw, k, v, kk, ab, g, rwkv_r_k[i], rwkv_lnx_w[i], rwkv_lnx_b[i], bf(rwkv_w_o[i]),
                     ffn, norm_final, final_norm=(layer == depth - 1))
    return x
```

```python
import functools

import jax
import jax.numpy as jnp
from jax import lax
from jax.experimental import pallas as pl
from jax.experimental.pallas import tpu as pltpu

F32 = jnp.float32
BF16 = jnp.bfloat16

HEAD_DIM = 64
ROT_DIM = HEAD_DIM // 4
ROT_HALF = ROT_DIM // 2
ROPE_THETA = 500000.0
BLOCK = 128
NORM_EPS = 1e-5
LNX_EPS = 64e-5

A_Q_HEADS = 12
A_KV_HEADS = 3
A_GROUP = A_Q_HEADS // A_KV_HEADS
A_WINDOW = 128
B_PATTERNS = ((128, 1), (512, 4), (2048, 16))
B_HEADS = 4

A_Q_W = A_Q_HEADS * HEAD_DIM
A_KV_W = A_KV_HEADS * HEAD_DIM
B_PW = B_HEADS * HEAD_DIM
B_W = len(B_PATTERNS) * B_PW
KV_DUP_W = 2 * A_KV_W
OFF_KA = A_Q_W
OFF_VA = OFF_KA + KV_DUP_W
OFF_QB = OFF_VA + KV_DUP_W
OFF_KB = OFF_QB + B_W
OFF_VB = OFF_KB + B_W

LANES = 128
MXU_W = 256
WKV_CHUNK = 64
DECAY_SCALE = 0.6065306597126334
NEG_BIG = -1e30
VMEM_LIMIT = 56 * 1024 * 1024


def _cparams(*sem):
    return pltpu.CompilerParams(dimension_semantics=sem, vmem_limit_bytes=VMEM_LIMIT)


def _dot(a, b):
    return jnp.dot(a, b, preferred_element_type=F32)


def _dot_nt(a, b):
    return lax.dot_general(a, b, (((1,), (1,)), ((), ())), preferred_element_type=F32)


def _dot_tn(a, b):
    return lax.dot_general(a, b, (((0,), (0,)), ((), ())), preferred_element_type=F32)


def _row_sum(z):
    part = z[:, :LANES]
    for j in range(1, z.shape[1] // LANES):
        part = part + z[:, j * LANES:(j + 1) * LANES]
    ones = jnp.ones((LANES, LANES), BF16)
    hi = part.astype(BF16)
    lo = (part - hi.astype(F32)).astype(BF16)
    return _dot(hi, ones) + _dot(lo, ones)


def _rms(x, g):
    inv = lax.rsqrt(_row_sum(x * x) * (1.0 / x.shape[1]) + NORM_EPS)
    return x * jnp.concatenate([inv] * (x.shape[1] // LANES), axis=1) * g


def _sigmoid(z):
    return 1.0 / (1.0 + jnp.exp(-z))


def _head_mask(n):
    r = lax.broadcasted_iota(jnp.int32, (n, n), 0) // HEAD_DIM
    c = lax.broadcasted_iota(jnp.int32, (n, n), 1) // HEAD_DIM
    return r == c


def _head_sum(z, ones_bd, split=True):
    outs = []
    for j in range(z.shape[1] // MXU_W):
        zc = z[:, j * MXU_W:(j + 1) * MXU_W]
        hi = zc.astype(BF16)
        acc = _dot(hi, ones_bd)
        if split:
            acc = acc + _dot((zc - hi.astype(F32)).astype(BF16), ones_bd)
        outs.append(acc)
    return jnp.concatenate(outs, axis=1)


def _rope_trig_kernel(pos_ref, invf_ref, cos_ref, sin_ref):
    ang = pos_ref[...].astype(F32)[None] * invf_ref[...]
    cos_ref[...] = jnp.cos(ang)
    sin_ref[...] = jnp.sin(ang)


def _rope_tables(positions):
    bsz, s = positions.shape
    rows = bsz * s // LANES
    inv_freq = jnp.power(ROPE_THETA, -2.0 * jnp.arange(ROT_HALF, dtype=F32) / ROT_DIM)
    invf = jnp.broadcast_to(inv_freq[:, None, None], (ROT_HALF, 1, LANES))
    cos, sin = pl.pallas_call(
        _rope_trig_kernel,
        out_shape=(jax.ShapeDtypeStruct((ROT_HALF, rows, LANES), F32),) * 2,
        name="rope_trig",
    )(positions.reshape(rows, LANES), invf)
    cos = cos.reshape(ROT_HALF, bsz, s).transpose(1, 2, 0)
    sin = sin.reshape(ROT_HALF, bsz, s).transpose(1, 2, 0)
    ones = jnp.ones((bsz, s, HEAD_DIM - ROT_DIM), F32)
    z8 = jnp.zeros((bsz, s, ROT_HALF), F32)
    zrest = jnp.zeros((bsz, s, HEAD_DIM - ROT_DIM), F32)
    c = jnp.concatenate([cos, cos, ones], axis=-1)
    sa = jnp.concatenate([-sin, z8, zrest], axis=-1)
    sb = jnp.concatenate([z8, sin, zrest], axis=-1)
    return tuple(jnp.tile(t, (1, 1, LANES // HEAD_DIM)) for t in (c, sa, sb))


def _residue_major(ref, dil):
    n = ref.shape[0] // dil
    return jnp.concatenate([ref[pl.ds(r, n, stride=dil), :] for r in range(dil)], axis=0)


def _attn_in_kernel(x_ref, g_ref, w_ref, b_ref, c_ref, sa_ref, sb_ref, qa_ref, kva_ref, *rest):
    b_refs, hbuf = rest[:-1], rest[-1]
    hf = _rms(x_ref[...], g_ref[...])
    tm, d = hf.shape
    for j in range(d // LANES):
        hbuf[j] = hf[:, j * LANES:(j + 1) * LANES]
    scale = HEAD_DIM ** -0.5

    def proj(h, lo, width):
        return _dot(h, w_ref[:, lo:lo + width]) + b_ref[:, lo:lo + width]

    def rope(z, tabs):
        c, sa, sb = tabs
        return z * c + pltpu.roll(z, LANES - ROT_HALF, 1) * sa + pltpu.roll(z, ROT_HALF, 1) * sb

    def rope_all(z, tabs):
        return jnp.concatenate([rope(z[:, j * LANES:(j + 1) * LANES], tabs)
                                for j in range(z.shape[1] // LANES)], axis=1)

    h = hf.astype(BF16)
    tabs = (c_ref[...], sa_ref[...], sb_ref[...])
    for j in range(A_Q_W // MXU_W):
        qa_ref[:, j * MXU_W:(j + 1) * MXU_W] = (rope_all(proj(h, j * MXU_W, MXU_W), tabs) * scale).astype(BF16)
    kva_ref[:, :KV_DUP_W] = rope_all(proj(h, OFF_KA, KV_DUP_W), tabs).astype(BF16)
    kva_ref[:, KV_DUP_W:] = proj(h, OFF_VA, KV_DUP_W).astype(BF16)
    for pat, (_, dil) in enumerate(B_PATTERNS):
        q_ref, k_ref, v_ref = b_refs[3 * pat:3 * pat + 3]
        if dil > 1:
            h = jnp.concatenate([_residue_major(hbuf.at[j], dil) for j in range(d // LANES)],
                                axis=1).astype(BF16)
            tabs = tuple(_residue_major(t, dil) for t in (c_ref, sa_ref, sb_ref))
        q = (rope_all(proj(h, OFF_QB + pat * B_PW, B_PW), tabs) * scale).astype(BF16)
        k = rope_all(proj(h, OFF_KB + pat * B_PW, B_PW), tabs).astype(BF16)
        v = proj(h, OFF_VB + pat * B_PW, B_PW).astype(BF16)
        n = tm // dil
        for r in range(dil):
            q_ref[:, r * B_PW:(r + 1) * B_PW] = q[r * n:(r + 1) * n]
            k_ref[:, r * B_PW:(r + 1) * B_PW] = k[r * n:(r + 1) * n]
            v_ref[:, r * B_PW:(r + 1) * B_PW] = v[r * n:(r + 1) * n]


def _widen_in_proj(w):
    head = lambda off, h: w[..., off + h * HEAD_DIM:off + (h + 1) * HEAD_DIM]
    dup = lambda off: [head(off, h) for h in range(A_KV_HEADS) for _ in range(2)]
    return jnp.concatenate([w[..., :A_Q_W]] + dup(A_Q_W) + dup(A_Q_W + A_KV_W) + [w[..., A_Q_W + 2 * A_KV_W:]], axis=-1)


def _attn_in(x, g, w_in, b_in, tables, tm=512):
    bsz, s, d = x.shape
    in_w = w_in.shape[1]
    tok = lambda w: pl.BlockSpec((None, tm, w), lambda b, i: (b, i, 0))
    full = lambda shape: pl.BlockSpec(shape, lambda b, i: (0,) * len(shape))
    out_specs = [tok(A_Q_W), tok(2 * KV_DUP_W)]
    out_shape = [jax.ShapeDtypeStruct((bsz, s, A_Q_W), BF16), jax.ShapeDtypeStruct((bsz, s, 2 * KV_DUP_W), BF16)]
    for _, dil in B_PATTERNS:
        out_specs += [pl.BlockSpec((None, tm // dil, dil * B_PW), lambda b, i: (b, i, 0))] * 3
        out_shape += [jax.ShapeDtypeStruct((bsz, s // dil, dil * B_PW), BF16)] * 3
    outs = pl.pallas_call(
        _attn_in_kernel,
        grid=(bsz, s // tm),
        in_specs=[tok(d), full((1, d)), full((d, in_w)), full((1, in_w)),
                  tok(LANES), tok(LANES), tok(LANES)],
        out_specs=out_specs,
        out_shape=out_shape,
        scratch_shapes=[pltpu.VMEM((d // LANES, tm, LANES), F32)],
        compiler_params=_cparams("parallel", "parallel"),
        name="attn_in",
    )(x, g.reshape(1, d), w_in, b_in.reshape(1, in_w), *tables)
    return outs[0], outs[1], [outs[2 + 3 * p:5 + 3 * p] for p in range(len(B_PATTERNS))]


def _band_bias(rows, max_dist, first_tile):
    qi = lax.broadcasted_iota(jnp.int32, (rows, 2 * BLOCK), 0) % BLOCK
    kj = lax.broadcasted_iota(jnp.int32, (rows, 2 * BLOCK), 1)
    dist = BLOCK + qi - kj
    band = (dist >= 0) & (dist <= max_dist)
    inner = jnp.where(band, 0.0, NEG_BIG)
    return jnp.where(band & ((kj >= BLOCK) | jnp.logical_not(first_tile)), 0.0, NEG_BIG), inner


def _window_rows(prev_ref, cur_ref, jb, lanes):
    if jb == 0:
        return jnp.concatenate([prev_ref[:, lanes], cur_ref[0:BLOCK, lanes]], axis=0)
    return cur_ref[(jb - 1) * BLOCK:(jb + 1) * BLOCK, lanes]


def _attn_a_kernel(sink_ref, q_ref, kvp_ref, kvc_ref, o_ref):
    nb = q_ref.shape[0] // BLOCK
    rows = A_GROUP * BLOCK
    first_tile = pl.program_id(1) == 0
    qi = lax.broadcasted_iota(jnp.int32, (rows, BLOCK), 0) % BLOCK
    from_prev = lax.broadcasted_iota(jnp.int32, (rows, BLOCK), 1) > qi
    low = lax.broadcasted_iota(jnp.int32, (BLOCK, LANES), 1) < HEAD_DIM
    rowg = lax.broadcasted_iota(jnp.int32, (rows, 1), 0) // BLOCK
    sinks = []
    for hk in range(A_KV_HEADS):
        sk = jnp.zeros((rows, 1), F32)
        for g in range(A_GROUP):
            sk = jnp.where(rowg == g, sink_ref[hk * A_GROUP + g], sk)
        sinks.append(sk)
    kvh = range(A_KV_HEADS)
    for jb in range(nb):
        rs = slice(jb * BLOCK, (jb + 1) * BLOCK)
        sc = []
        for hk in kvh:
            tiles = [q_ref[rs, (2 * hk + t) * LANES:(2 * hk + t + 1) * LANES] for t in range(2)]
            zero = jnp.zeros_like(tiles[0])
            q4 = jnp.concatenate([jnp.where(low, tiles[0], zero), jnp.where(low, zero, tiles[0]),
                                  jnp.where(low, tiles[1], zero), jnp.where(low, zero, tiles[1])], axis=0)
            k = _window_rows(kvp_ref, kvc_ref, jb, slice(hk * LANES, (hk + 1) * LANES))
            both = _dot_nt(q4, k)
            prev = both[:, :BLOCK]
            if jb == 0:
                prev = jnp.where(first_tile, NEG_BIG, prev)
            sc.append(jnp.where(from_prev, prev, both[:, BLOCK:]))
        m = [jnp.maximum(jnp.max(s, axis=-1, keepdims=True), sk) for s, sk in zip(sc, sinks)]
        p = [jnp.exp(s - mm) for s, mm in zip(sc, m)]
        den = [jnp.sum(pp, axis=-1, keepdims=True) + jnp.exp(sk - mm) for pp, sk, mm in zip(p, sinks, m)]
        o = []
        for hk in kvh:
            v = _window_rows(kvp_ref, kvc_ref, jb, slice((A_KV_HEADS + hk) * LANES, (A_KV_HEADS + hk + 1) * LANES))
            pb = p[hk].astype(BF16)
            zero = jnp.zeros_like(pb)
            unfolded = jnp.concatenate([jnp.where(from_prev, pb, zero), jnp.where(from_prev, zero, pb)], axis=1)
            o.append(_dot(unfolded, v) * (1.0 / den[hk]))
        for hk in kvh:
            for t in range(2):
                pair = jnp.where(low, o[hk][2 * t * BLOCK:(2 * t + 1) * BLOCK], o[hk][(2 * t + 1) * BLOCK:(2 * t + 2) * BLOCK])
                o_ref[rs, (2 * hk + t) * LANES:(2 * hk + t + 1) * LANES] = pair.astype(BF16)


def _attn_a(qa, kva, sinks, tile=1024):
    bsz, s, _ = qa.shape
    tile = min(tile, s)
    kvw = kva.shape[2]
    per = tile // BLOCK
    return pl.pallas_call(
        _attn_a_kernel,
        grid=(bsz, s // tile),
        in_specs=[pl.BlockSpec(memory_space=pltpu.SMEM),
                  pl.BlockSpec((None, tile, A_Q_W), lambda b, i: (b, i, 0)),
                  pl.BlockSpec((None, BLOCK, kvw), lambda b, i: (b, jnp.maximum(i * per - 1, 0), 0)),
                  pl.BlockSpec((None, tile, kvw), lambda b, i: (b, i, 0))],
        out_specs=pl.BlockSpec((None, tile, A_Q_W), lambda b, i: (b, i, 0)),
        out_shape=jax.ShapeDtypeStruct((bsz, s, A_Q_W), BF16),
        compiler_params=_cparams("parallel", "parallel"),
        name="attn_a",
    )(sinks, qa, kva, kva)


def _attn_b_kernel(group, q_ref, kp_ref, kc_ref, vp_ref, vc_ref, o_ref, l_ref):
    nb = q_ref.shape[0] // BLOCK
    bias_first, bias_inner = _band_bias(BLOCK, BLOCK, pl.program_id(2) == 0)
    low = lax.broadcasted_iota(jnp.int32, (BLOCK, LANES), 1) < HEAD_DIM
    for j0 in range(0, nb, group):
        items = [(jb, h) for jb in range(j0, min(j0 + group, nb)) for h in range(B_HEADS)]
        sc = []
        for jb, h in items:
            lanes = slice((h // 2) * LANES, (h // 2 + 1) * LANES)
            qt = q_ref[jb * BLOCK:(jb + 1) * BLOCK, lanes]
            zero = jnp.zeros_like(qt)
            qh = jnp.where(low, qt, zero) if h % 2 == 0 else jnp.where(low, zero, qt)
            sc.append(_dot_nt(qh, _window_rows(kp_ref, kc_ref, jb, lanes)) + (bias_first if jb == 0 else bias_inner))
        m = [jnp.max(s, axis=-1, keepdims=True) for s in sc]
        p = [jnp.exp(s - mm) for s, mm in zip(sc, m)]
        den = [jnp.sum(pp, axis=-1, keepdims=True) for pp in p]
        o = [_dot(pp.astype(BF16), _window_rows(vp_ref, vc_ref, jb, slice((h // 2) * LANES, (h // 2 + 1) * LANES)))
             * (1.0 / dd) for pp, dd, (jb, h) in zip(p, den, items)]
        lse = [mm + jnp.log(dd) for mm, dd in zip(m, den)]
        for idx in range(0, len(items), 2):
            jb, h = items[idx]
            dst = (slice(jb * BLOCK, (jb + 1) * BLOCK), slice((h // 2) * LANES, (h // 2 + 1) * LANES))
            o_ref[dst] = jnp.where(low, o[idx], o[idx + 1])
            l_ref[dst] = jnp.where(low, lse[idx], lse[idx + 1])


def _attn_b(q, k, v, pat, dil, tile=1024, group=2):
    bsz, length, _ = q.shape
    tile = min(tile, length)
    per = tile // BLOCK
    cur = pl.BlockSpec((None, tile, B_PW), lambda b, r, i: (b, i, r))
    prev = pl.BlockSpec((None, BLOCK, B_PW), lambda b, r, i: (b, jnp.maximum(i * per - 1, 0), r))
    return pl.pallas_call(
        functools.partial(_attn_b_kernel, group),
        grid=(bsz, dil, length // tile),
        in_specs=[cur, prev, cur, prev, cur],
        out_specs=[cur, cur],
        out_shape=[jax.ShapeDtypeStruct((bsz, length, dil * B_PW), F32)] * 2,
        compiler_params=_cparams("parallel", "parallel", "parallel"),
        name=f"attn_b{pat}",
    )(q, k, k, v, v)


def _attn_out_kernel(x_ref, oa_ref, o0_ref, o1_ref, o2_ref, l0_ref, l1_ref, l2_ref, w_ref,
                     gf_ref, wg_ref, wu_ref, wd_ref, out_ref, buf, acc_ref):
    tm = x_ref.shape[0]

    def natural(ref, dil, slot):
        if dil == 1:
            return ref[...]
        n = tm // dil
        halves = B_PW // LANES
        for r in range(dil):
            for j in range(halves):
                buf[slot * halves + j, pl.ds(r, n, stride=dil), :] = ref[:, r * B_PW + j * LANES:r * B_PW + (j + 1) * LANES]
        return jnp.concatenate([buf[slot * halves + j] for j in range(halves)], axis=1)

    dils = [dil for _, dil in B_PATTERNS]
    o = [natural(ref, dil, i) for i, (ref, dil) in enumerate(zip((o0_ref, o1_ref, o2_ref), dils))]
    l = [natural(ref, dil, 3 + i) for i, (ref, dil) in enumerate(zip((l0_ref, l1_ref, l2_ref), dils))]
    m = jnp.maximum(jnp.maximum(l[0], l[1]), l[2])
    e = [jnp.exp(v - m) for v in l]
    ob = (e[0] * o[0] + e[1] * o[1] + e[2] * o[2]) / (e[0] + e[1] + e[2])
    mix = _dot(oa_ref[...], w_ref[:A_Q_W, :]) + _dot(ob.astype(BF16), w_ref[A_Q_W:, :])
    out_ref[...] = _ffn_block(x_ref[...] + mix, gf_ref, wg_ref, wu_ref, wd_ref, acc_ref)


def _attn_out(x, oa, obs, lses, w_out, ffn, tm=512):
    bsz, s, d = x.shape
    tok = lambda w: pl.BlockSpec((None, tm, w), lambda b, i: (b, i, 0))
    pat = [pl.BlockSpec((None, tm // dil, dil * B_PW), lambda b, i: (b, i, 0)) for _, dil in B_PATTERNS]
    return pl.pallas_call(
        _attn_out_kernel,
        grid=(bsz, s // tm),
        in_specs=[tok(d), tok(A_Q_W)] + pat + pat + [_resident(a.shape) for a in (w_out,) + ffn],
        out_specs=tok(d),
        out_shape=jax.ShapeDtypeStruct((bsz, s, d), F32),
        scratch_shapes=[pltpu.VMEM((2 * len(B_PATTERNS) * (B_PW // LANES), tm, LANES), F32),
                        pltpu.VMEM((tm, d), F32)],
        compiler_params=_cparams("parallel", "parallel"),
        name="attn_out_ffn",
    )(x, oa, *obs, *lses, w_out, *ffn)


FF_CHUNK = 256


def _ffn_block(x, g_ref, wg_ref, wu_ref, wd_ref, acc_ref):
    h = _rms(x, g_ref[...]).astype(BF16)
    for c in range(wg_ref.shape[1] // FF_CHUNK):
        sl = slice(c * FF_CHUNK, (c + 1) * FF_CHUNK)
        gate = _dot(h, wg_ref[:, sl])
        up = _dot(h, wu_ref[:, sl])
        contrib = _dot((gate * _sigmoid(gate) * up).astype(BF16), wd_ref[sl, :])
        if c == 0:
            acc_ref[...] = x + contrib
        else:
            acc_ref[...] += contrib
    return acc_ref[...]


def _resident(shape):
    return pl.BlockSpec(shape, lambda *_: (0,) * len(shape), pipeline_mode=pl.Buffered(1))


def _rwkv_in_kernel(has_vlora, *refs):
    (x_ref, gn_ref, mu_ref, wr_ref, wk_ref, wv_ref, w0_ref, w1_ref, w2_ref, a0_ref, a1_ref, a2_ref,
     g1_ref, g2_ref, kk_ref, ka_ref) = refs[:16]
    refs = refs[16:]
    if has_vlora:
        v0_ref, v1_ref, v2_ref, vf_ref = refs[:4]
        refs = refs[4:]
    r_o, lw_o, k_o, v_o, kk_o, ab_o, g_o, carry, mix = refs
    tm, d = x_ref.shape

    @pl.when(pl.program_id(1) == 0)
    def _():
        carry[...] = jnp.zeros(carry.shape, F32)

    h = _rms(x_ref[...], gn_ref[...])
    rolled = pltpu.roll(h, 1, 0)
    first = lax.broadcasted_iota(jnp.int32, (8, 1), 0) == 0
    hprev = jnp.concatenate([jnp.where(first, carry[7:8, :], rolled[0:8]), rolled[8:]], axis=0)
    carry[...] = h[tm - 8:tm, :]
    hb = h.astype(BF16)
    xxb = (hprev - h).astype(BF16)
    mub = mu_ref[...].astype(BF16)
    for i in range(mix.shape[0]):
        mix[i] = hb + xxb * mub[i:i + 1, :]

    mid_w = jnp.tanh(_dot(mix[1], w1_ref[...])).astype(BF16)
    mid_a = _dot(mix[4], a1_ref[...]).astype(BF16)
    mid_g = _sigmoid(_dot(mix[5], g1_ref[...])).astype(BF16)
    if has_vlora:
        mid_v = _dot(mix[3], v1_ref[...]).astype(BF16)

    ones_bd = jnp.where(_head_mask(MXU_W), 1.0, 0.0).astype(BF16)
    for c in range(d // MXU_W):
        cs = slice(c * MXU_W, (c + 1) * MXU_W)
        r_o[:, cs] = _dot(mix[0], wr_ref[:, cs])
        lw_o[:, cs] = -DECAY_SCALE * _sigmoid(w0_ref[:, cs] + _dot(mid_w, w2_ref[:, cs]))
        v = _dot(mix[3], wv_ref[:, cs])
        if has_vlora:
            v = v + (vf_ref[:, cs] - v) * _sigmoid(v0_ref[:, cs] + _dot(mid_v, v2_ref[:, cs]))
        v_o[:, cs] = v
        g_o[:, cs] = _dot(mid_g, g2_ref[:, cs]).astype(BF16)
        k = _dot(mix[2], wk_ref[:, cs])
        a = _sigmoid(a0_ref[:, cs] + _dot(mid_a, a2_ref[:, cs]))
        kk = k * kk_ref[:, cs]
        kk = kk / jnp.maximum(jnp.sqrt(_head_sum(kk * kk, ones_bd)), 1e-12)
        kk_o[:, cs] = kk
        ab_o[:, cs] = kk * a
        k_o[:, cs] = k * (1.0 + (a - 1.0) * ka_ref[:, cs])


def _rwkv_in(x, gn, p, v_lora, v_first, tm=512):
    bsz, s, d = x.shape
    tok = pl.BlockSpec((None, tm, d), lambda b, i: (b, i, 0))
    row = lambda a: a.reshape(1, d)
    args = [x, row(gn), p["mu"], p["w_r"], p["w_k"], p["w_v"], row(p["w0"]), p["w1"], p["w2"], row(p["a0"]), p["a1"], p["a2"],
            p["g1"], p["g2"], row(p["k_k"]), row(p["k_a"])]
    specs = [tok] + [_resident(a.shape) for a in args[1:]]
    if v_lora is not None:
        v0, v1, v2 = v_lora
        extra = [row(v0), v1, v2]
        args += extra + [v_first]
        specs += [_resident(a.shape) for a in extra] + [tok]
    return pl.pallas_call(
        functools.partial(_rwkv_in_kernel, v_lora is not None),
        grid=(bsz, s // tm),
        in_specs=specs,
        out_specs=[tok] * 7,
        out_shape=[jax.ShapeDtypeStruct((bsz, s, d), F32)] * 6 + [jax.ShapeDtypeStruct((bsz, s, d), BF16)],
        scratch_shapes=[pltpu.VMEM((8, d), F32), pltpu.VMEM((p["mu"].shape[0], tm, d), BF16)],
        compiler_params=_cparams("parallel", "arbitrary"),
        name="rwkv_in",
    )(*args)


def _bd2(z):
    lane = lax.broadcasted_iota(jnp.int32, z.shape, 1)
    zero = jnp.zeros(z.shape, z.dtype)
    return jnp.concatenate([jnp.where(lane < HEAD_DIM, z, zero), jnp.where(lane >= HEAD_DIM, z, zero)], axis=0)


def _fold2(full):
    lane = lax.broadcasted_iota(jnp.int32, (HEAD_DIM, LANES), 1)
    return jnp.where(lane < HEAD_DIM, full[:HEAD_DIM], full[HEAD_DIM:])


def _wkv_prep_kernel(r_ref, lw_ref, k_ref, v_ref, kk_ref, ab_ref, rk_ref,
                     q_ref, y0_ref, mc_ref, z_ref, dec_ref, bonus_ref):
    cs = WKV_CHUNK
    rows, d = r_ref.shape
    nc = rows // cs
    ri = lax.broadcasted_iota(jnp.int32, (rows, rows), 0)
    ci = lax.broadcasted_iota(jnp.int32, (rows, rows), 1)
    tri = jnp.where((ri >= ci) & (ri // cs == ci // cs), 1.0, 0.0).astype(BF16)
    lw = lw_ref[...]
    hi = lw.astype(BF16)
    rem = lw - hi.astype(F32)
    mid = rem.astype(BF16)
    lo = (rem - mid.astype(F32)).astype(BF16)
    cum = _dot(tri, hi) + _dot(tri, mid) + _dot(tri, lo)
    totals = [cum[(j + 1) * cs - 1:(j + 1) * cs, :] for j in range(nc)]
    tot = jnp.concatenate([jnp.broadcast_to(t, (cs, d)) for t in totals], axis=0)
    e_neg = jnp.exp(-cum)
    e_tot = jnp.exp(tot - cum)
    kk, ab, kx = kk_ref[...], ab_ref[...], k_ref[...]
    at = (-kk * jnp.exp(cum - lw)).astype(BF16)
    rt = r_ref[...] * jnp.exp(cum)
    rtb = rt.astype(BF16)
    bt = (ab * e_neg).astype(BF16)
    kt = (kx * e_neg).astype(BF16)
    bh = (ab * e_tot).astype(BF16)
    kh = (kx * e_tot).astype(BF16)
    vb = v_ref[...].astype(BF16)
    ones_bd = jnp.where(_head_mask(MXU_W), 1.0, 0.0).astype(BF16)
    bonus_ref[...] = _head_sum(r_ref[...] * kx * rk_ref[...], ones_bd, split=False) * v_ref[...]

    trow = lax.broadcasted_iota(jnp.int32, (cs, LANES), 0)
    tcol = lax.broadcasted_iota(jnp.int32, (cs, LANES), 1) % HEAD_DIM
    strict = trow > tcol
    incl = trow >= tcol
    eye = jnp.where(trow == tcol, 1.0, 0.0)

    for j in range(nc):
        dec_ref[8 * j:8 * j + 8, :] = jnp.broadcast_to(jnp.exp(totals[j]), (8, d))

    chains = [(slice(j * cs, (j + 1) * cs), slice(p * LANES, (p + 1) * LANES))
              for j in range(nc) for p in range(d // LANES)]

    g12 = [_dot_nt(jnp.concatenate([at[c], rtb[c]], axis=0),
                   jnp.concatenate([_bd2(bt[c]), _bd2(kt[c])], axis=0)) for c in chains]
    a_ab = [jnp.where(strict, g[:cs, :LANES], 0.0) for g in g12]
    a_rb = [jnp.where(incl, g[cs:, :LANES], 0.0).astype(BF16) for g in g12]
    a_k = [jnp.concatenate([jnp.where(strict, g[:cs, LANES:], 0.0),
                            jnp.where(incl, g[cs:, LANES:], 0.0)], axis=0).astype(BF16) for g in g12]
    g4 = [_dot(a, _bd2(vb[c])) for a, c in zip(a_k, chains)]
    x = [eye + jnp.where((trow - tcol == 1) & (trow % 2 == 1), a, 0.0) for a in a_ab]
    size = 2
    while size < cs:
        lower_left = (trow // size - tcol // size == 1) & ((trow // size) % 2 == 1)
        xb = [v.astype(BF16) for v in x]
        xn = [_dot(v, _bd2(jnp.where(lower_left, a, 0.0).astype(BF16))) for v, a in zip(xb, a_ab)]
        x = [v + _dot(n.astype(BF16), _bd2(vb16)) for v, n, vb16 in zip(x, xn, xb)]
        size *= 2
    wu = [_dot(v.astype(BF16), jnp.concatenate([_bd2(at[c]), _bd2(g[:cs].astype(BF16))], axis=1))
          for v, g, c in zip(x, g4, chains)]
    wb = [v[:, :LANES].astype(BF16) for v in wu]
    ub = [v[:, LANES:].astype(BF16) for v in wu]
    qy = [_dot(a, jnp.concatenate([_bd2(w), _bd2(u)], axis=1)) for a, w, u in zip(a_rb, wb, ub)]
    mc = [_dot_tn(w, bh[c]) for w, c in zip(wb, chains)]
    zz = [_dot_tn(jnp.concatenate([u, vb[c]], axis=0), jnp.concatenate([bh[c], kh[c]], axis=0))
          for u, c in zip(ub, chains)]
    for i, c in enumerate(chains):
        q_ref[c] = (rt[c] + qy[i][:, :LANES]).astype(BF16)
        y0_ref[c] = g4[i][cs:] + qy[i][:, LANES:]
        mc_ref[c] = _fold2(mc[i]).astype(BF16)
        z_ref[c] = _fold2(zz[i])


def _wkv_out_kernel(final_norm, q_ref, y0_ref, mc_ref, z_ref, dec_ref, x_ref, bonus_ref, g_ref, lnw_ref, lnb_ref,
                    wo_ref, gf_ref, wg_ref, wu_ref, wd_ref, gl_ref, out_ref, s_ref, y_buf, acc_ref):
    cs = WKV_CHUNK
    bsz, rows, d = q_ref.shape

    @pl.when(pl.program_id(0) == 0)
    def _():
        s_ref[...] = jnp.zeros(s_ref.shape, F32)

    seqs = [(b, slice(p * LANES, (p + 1) * LANES)) for b in range(bsz) for p in range(d // LANES)]
    state = [s_ref[b, :, ls] for b, ls in seqs]
    for j in range(rows // cs):
        rs = slice(j * cs, (j + 1) * cs)
        sb = [s.astype(BF16) for s in state]
        upd = [_dot(v, _bd2(mc_ref[b, rs, ls])) for v, (b, ls) in zip(sb, seqs)]
        for v, (b, ls) in zip(sb, seqs):
            y_buf[b * rows + j * cs:b * rows + (j + 1) * cs, ls] = _dot_nt(q_ref[b, rs, ls], _bd2(v)) + y0_ref[b, rs, ls]
        state = [s * dec_ref[b, 8 * j:8 * j + 1, ls] + u + z_ref[b, rs, ls]
                 for s, u, (b, ls) in zip(state, upd, seqs)]
    for s, (b, ls) in zip(state, seqs):
        s_ref[b, :, ls] = s

    ones_bd = jnp.where(_head_mask(MXU_W), 1.0, 0.0).astype(BF16)
    y = y_buf[...]
    dev = y - _head_sum(y, ones_bd, split=False) * (1.0 / HEAD_DIM)
    var = _head_sum(dev * dev, ones_bd, split=False) * (1.0 / HEAD_DIM)
    yn = dev * lax.rsqrt(var + LNX_EPS) * lnw_ref[...] + lnb_ref[...]
    gated = jnp.concatenate([((yn[b * rows:(b + 1) * rows] + bonus_ref[b]) * g_ref[b]).astype(BF16)
                             for b in range(bsz)], axis=0)
    x1 = jnp.concatenate([x_ref[b] for b in range(bsz)], axis=0) + _dot(gated, wo_ref[...])
    out = _ffn_block(x1, gf_ref, wg_ref, wu_ref, wd_ref, acc_ref)
    if final_norm:
        out = _rms(out, gl_ref[...])
    for b in range(bsz):
        out_ref[b] = out[b * rows:(b + 1) * rows]


def _wkv(x, r, lw, k, v, kk, ab, g, rk, lnw, lnb, wo, ffn, g_last, final_norm, prep_chunks=2, scan_chunks=2):
    bsz, s, d = r.shape
    cs = WKV_CHUNK
    rows = prep_chunks * cs
    blk = pl.BlockSpec((None, rows, d), lambda b, c: (b, c, 0))
    dec_blk = pl.BlockSpec((None, 8 * prep_chunks, d), lambda b, c: (b, c, 0))
    act = lambda dt: jax.ShapeDtypeStruct((bsz, s, d), dt)
    q, y0, mc, z, dec, bonus = pl.pallas_call(
        _wkv_prep_kernel,
        grid=(bsz, s // rows),
        in_specs=[blk] * 6 + [pl.BlockSpec((1, d), lambda b, c: (0, 0))],
        out_specs=[blk, blk, blk, blk, dec_blk, blk],
        out_shape=[act(BF16), act(F32), act(BF16), act(F32),
                   jax.ShapeDtypeStruct((bsz, 8 * s // cs, d), F32), act(F32)],
        compiler_params=_cparams("parallel", "parallel"),
        name="wkv7_prep",
    )(r, lw, k, v, kk, ab, rk.reshape(1, d))
    rows = scan_chunks * cs
    blk = pl.BlockSpec((bsz, rows, d), lambda c: (0, c, 0))
    dec_blk = pl.BlockSpec((bsz, 8 * scan_chunks, d), lambda c: (0, c, 0))
    row = pl.BlockSpec((1, d), lambda c: (0, 0))
    return pl.pallas_call(
        functools.partial(_wkv_out_kernel, final_norm),
        grid=(s // rows,),
        in_specs=[blk, blk, blk, blk, dec_blk, blk, blk, blk, row, row]
                 + [_resident(a.shape) for a in (wo,) + ffn] + [row],
        out_specs=blk,
        out_shape=act(F32),
        scratch_shapes=[pltpu.VMEM((bsz, HEAD_DIM, d), F32), pltpu.VMEM((bsz * rows, d), F32),
                        pltpu.VMEM((bsz * rows, d), F32)],
        compiler_params=_cparams("arbitrary"),
        name="wkv7_out_ffn",
    )(q, y0, mc, z, dec, x, bonus, g, lnw.reshape(1, d), lnb.reshape(1, d), wo, *ffn, g_last.reshape(1, d))


def kernel(x, positions, norm_mix, norm_ffn, norm_final, attn_w_in, attn_b_in, attn_sinks, attn_w_out, rwkv_mu, rwkv_w_rkv, rwkv_w0, rwkv_w1, rwkv_w2, rwkv_a0, rwkv_a1, rwkv_a2, rwkv_g1, rwkv_g2, rwkv_k_k, rwkv_k_a, rwkv_r_k, rwkv_lnx_w, rwkv_lnx_b, rwkv_w_o, rwkv_v0, rwkv_v1, rwkv_v2, ffn_w_gate, ffn_w_up, ffn_w_down):
    depth = norm_mix.shape[0]
    bf = lambda a: a.astype(BF16)
    tables = _rope_tables(positions)
    v_first = None
    for layer in range(depth):
        i = layer // 2
        ffn = (norm_ffn[layer].reshape(1, -1), bf(ffn_w_gate[layer]), bf(ffn_w_up[layer]), bf(ffn_w_down[layer]))
        if layer % 2 == 0:
            qa, kva, qkv_b = _attn_in(x, norm_mix[layer], _widen_in_proj(bf(attn_w_in[i])),
                                      _widen_in_proj(attn_b_in[i]), tables)
            oa = _attn_a(qa, kva, attn_sinks[i])
            obs, lses = zip(*[_attn_b(*qkv_b[pat], pat, dil) for pat, (_, dil) in enumerate(B_PATTERNS)])
            x = _attn_out(x, oa, obs, lses, bf(attn_w_out[i]), ffn)
        else:
            p = dict(mu=rwkv_mu[i], w_r=bf(rwkv_w_rkv[i, 0]), w_k=bf(rwkv_w_rkv[i, 1]), w_v=bf(rwkv_w_rkv[i, 2]),
                     w0=rwkv_w0[i], w1=bf(rwkv_w1[i]), w2=bf(rwkv_w2[i]),
                     a0=rwkv_a0[i], a1=bf(rwkv_a1[i]), a2=bf(rwkv_a2[i]), g1=bf(rwkv_g1[i]), g2=bf(rwkv_g2[i]),
                     k_k=rwkv_k_k[i], k_a=rwkv_k_a[i])
            v_lora = None if i == 0 else (rwkv_v0[i - 1], bf(rwkv_v1[i - 1]), bf(rwkv_v2[i - 1]))
            r, lw, k, v, kk, ab, g = _rwkv_in(x, norm_mix[layer], p, v_lora, v_first)
            if i == 0:
                v_first = v
            x = _wkv(x, r, lw, k, v, kk, ab, g, rwkv_r_k[i], rwkv_lnx_w[i], rwkv_lnx_b[i], bf(rwkv_w_o[i]),
                     ffn, norm_final, final_norm=(layer == depth - 1))
    return x
```

```python
import functools

import jax
import jax.numpy as jnp
from jax import lax
from jax.experimental import pallas as pl
from jax.experimental.pallas import tpu as pltpu

F32 = jnp.float32
BF16 = jnp.bfloat16

HEAD_DIM = 64
ROT_DIM = HEAD_DIM // 4
ROT_HALF = ROT_DIM // 2
ROPE_THETA = 500000.0
BLOCK = 128
NORM_EPS = 1e-5
LNX_EPS = 64e-5

A_Q_HEADS = 12
A_KV_HEADS = 3
A_GROUP = A_Q_HEADS // A_KV_HEADS
A_WINDOW = 128
B_PATTERNS = ((128, 1), (512, 4), (2048, 16))
B_HEADS = 4

A_Q_W = A_Q_HEADS * HEAD_DIM
A_KV_W = A_KV_HEADS * HEAD_DIM
B_PW = B_HEADS * HEAD_DIM
B_W = len(B_PATTERNS) * B_PW
KV_DUP_W = 2 * A_KV_W
OFF_KA = A_Q_W
OFF_VA = OFF_KA + KV_DUP_W
OFF_QB = OFF_VA + KV_DUP_W
OFF_KB = OFF_QB + B_W
OFF_VB = OFF_KB + B_W

LANES = 128
MXU_W = 256
WKV_CHUNK = 64
DECAY_SCALE = 0.6065306597126334
NEG_BIG = -1e30
VMEM_LIMIT = 56 * 1024 * 1024


def _cparams(*sem):
    return pltpu.CompilerParams(dimension_semantics=sem, vmem_limit_bytes=VMEM_LIMIT)


def _dot(a, b):
    return jnp.dot(a, b, preferred_element_type=F32)


def _dot_nt(a, b):
    return lax.dot_general(a, b, (((1,), (1,)), ((), ())), preferred_element_type=F32)


def _dot_tn(a, b):
    return lax.dot_general(a, b, (((0,), (0,)), ((), ())), preferred_element_type=F32)


def _row_sum(z):
    part = z[:, :LANES]
    for j in range(1, z.shape[1] // LANES):
        part = part + z[:, j * LANES:(j + 1) * LANES]
    ones = jnp.ones((LANES, LANES), BF16)
    hi = part.astype(BF16)
    lo = (part - hi.astype(F32)).astype(BF16)
    return _dot(hi, ones) + _dot(lo, ones)


def _rms(x, g):
    inv = lax.rsqrt(_row_sum(x * x) * (1.0 / x.shape[1]) + NORM_EPS)
    return x * jnp.concatenate([inv] * (x.shape[1] // LANES), axis=1) * g


def _sigmoid(z):
    return 1.0 / (1.0 + jnp.exp(-z))


def _head_mask(n):
    r = lax.broadcasted_iota(jnp.int32, (n, n), 0) // HEAD_DIM
    c = lax.broadcasted_iota(jnp.int32, (n, n), 1) // HEAD_DIM
    return r == c


def _head_sum(z, ones_bd, split=True):
    outs = []
    for j in range(z.shape[1] // MXU_W):
        zc = z[:, j * MXU_W:(j + 1) * MXU_W]
        hi = zc.astype(BF16)
        acc = _dot(hi, ones_bd)
        if split:
            acc = acc + _dot((zc - hi.astype(F32)).astype(BF16), ones_bd)
        outs.append(acc)
    return jnp.concatenate(outs, axis=1)


def _rope_trig_kernel(pos_ref, invf_ref, cos_ref, sin_ref):
    ang = pos_ref[...].astype(F32)[None] * invf_ref[...]
    cos_ref[...] = jnp.cos(ang)
    sin_ref[...] = jnp.sin(ang)


def _rope_tables(positions):
    bsz, s = positions.shape
    rows = bsz * s // LANES
    inv_freq = jnp.power(ROPE_THETA, -2.0 * jnp.arange(ROT_HALF, dtype=F32) / ROT_DIM)
    invf = jnp.broadcast_to(inv_freq[:, None, None], (ROT_HALF, 1, LANES))
    cos, sin = pl.pallas_call(
        _rope_trig_kernel,
        out_shape=(jax.ShapeDtypeStruct((ROT_HALF, rows, LANES), F32),) * 2,
        name="rope_trig",
    )(positions.reshape(rows, LANES), invf)
    cos = jnp.tile(cos.reshape(ROT_HALF, bsz, s).transpose(1, 2, 0), (1, 1, LANES // ROT_HALF))
    sin = jnp.tile(sin.reshape(ROT_HALF, bsz, s).transpose(1, 2, 0), (1, 1, LANES // ROT_HALF))
    dim = jnp.arange(LANES) % HEAD_DIM
    c = jnp.where(dim < ROT_DIM, cos, 1.0)
    sa = jnp.where(dim < ROT_HALF, -sin, 0.0)
    sb = jnp.where((dim >= ROT_HALF) & (dim < ROT_DIM), sin, 0.0)
    return c, sa, sb


def _residue_major(ref, dil):
    n = ref.shape[0] // dil
    return jnp.concatenate([ref[pl.ds(r, n, stride=dil), :] for r in range(dil)], axis=0)


def _attn_in_kernel(x_ref, g_ref, w_ref, b_ref, c_ref, sa_ref, sb_ref, qa_ref, kva_ref, *rest):
    b_refs, hbuf = rest[:-1], rest[-1]
    hf = _rms(x_ref[...], g_ref[...])
    tm, d = hf.shape
    for j in range(d // LANES):
        hbuf[j] = hf[:, j * LANES:(j + 1) * LANES]
    scale = HEAD_DIM ** -0.5

    def proj(h, lo, width):
        return _dot(h, w_ref[:, lo:lo + width]) + b_ref[:, lo:lo + width]

    def rope(z, tabs):
        c, sa, sb = tabs
        return z * c + pltpu.roll(z, LANES - ROT_HALF, 1) * sa + pltpu.roll(z, ROT_HALF, 1) * sb

    def rope_all(z, tabs):
        return jnp.concatenate([rope(z[:, j * LANES:(j + 1) * LANES], tabs)
                                for j in range(z.shape[1] // LANES)], axis=1)

    h = hf.astype(BF16)
    tabs = (c_ref[...], sa_ref[...], sb_ref[...])
    for j in range(A_Q_W // MXU_W):
        qa_ref[:, j * MXU_W:(j + 1) * MXU_W] = (rope_all(proj(h, j * MXU_W, MXU_W), tabs) * scale).astype(BF16)
    kva_ref[:, :KV_DUP_W] = rope_all(proj(h, OFF_KA, KV_DUP_W), tabs).astype(BF16)
    kva_ref[:, KV_DUP_W:] = proj(h, OFF_VA, KV_DUP_W).astype(BF16)
    for pat, (_, dil) in enumerate(B_PATTERNS):
        q_ref, k_ref, v_ref = b_refs[3 * pat:3 * pat + 3]
        if dil > 1:
            h = jnp.concatenate([_residue_major(hbuf.at[j], dil) for j in range(d // LANES)],
                                axis=1).astype(BF16)
            tabs = tuple(_residue_major(t, dil) for t in (c_ref, sa_ref, sb_ref))
        q = (rope_all(proj(h, OFF_QB + pat * B_PW, B_PW), tabs) * scale).astype(BF16)
        k = rope_all(proj(h, OFF_KB + pat * B_PW, B_PW), tabs).astype(BF16)
        v = proj(h, OFF_VB + pat * B_PW, B_PW).astype(BF16)
        n = tm // dil
        for r in range(dil):
            q_ref[:, r * B_PW:(r + 1) * B_PW] = q[r * n:(r + 1) * n]
            k_ref[:, r * B_PW:(r + 1) * B_PW] = k[r * n:(r + 1) * n]
            v_ref[:, r * B_PW:(r + 1) * B_PW] = v[r * n:(r + 1) * n]


def _widen_in_proj(w):
    head = lambda off, h: w[..., off + h * HEAD_DIM:off + (h + 1) * HEAD_DIM]
    dup = lambda off: [head(off, h) for h in range(A_KV_HEADS) for _ in range(2)]
    return jnp.concatenate([w[..., :A_Q_W]] + dup(A_Q_W) + dup(A_Q_W + A_KV_W) + [w[..., A_Q_W + 2 * A_KV_W:]], axis=-1)


def _attn_in(x, g, w_in, b_in, tables, tm=1024):
    bsz, s, d = x.shape
    in_w = w_in.shape[1]
    tok = lambda w: pl.BlockSpec((None, tm, w), lambda b, i: (b, i, 0))
    full = _resident
    out_specs = [tok(A_Q_W), tok(2 * KV_DUP_W)]
    out_shape = [jax.ShapeDtypeStruct((bsz, s, A_Q_W), BF16), jax.ShapeDtypeStruct((bsz, s, 2 * KV_DUP_W), BF16)]
    for _, dil in B_PATTERNS:
        out_specs += [pl.BlockSpec((None, tm // dil, dil * B_PW), lambda b, i: (b, i, 0))] * 3
        out_shape += [jax.ShapeDtypeStruct((bsz, s // dil, dil * B_PW), BF16)] * 3
    outs = pl.pallas_call(
        _attn_in_kernel,
        grid=(bsz, s // tm),
        in_specs=[tok(d), full((1, d)), full((d, in_w)), full((1, in_w)),
                  tok(LANES), tok(LANES), tok(LANES)],
        out_specs=out_specs,
        out_shape=out_shape,
        scratch_shapes=[pltpu.VMEM((d // LANES, tm, LANES), F32)],
        compiler_params=_cparams("parallel", "parallel"),
        name="attn_in",
    )(x, g.reshape(1, d), w_in, b_in.reshape(1, in_w), *tables)
    return outs[0], outs[1], [outs[2 + 3 * p:5 + 3 * p] for p in range(len(B_PATTERNS))]


def _band_bias(rows, max_dist, first_tile):
    qi = lax.broadcasted_iota(jnp.int32, (rows, 2 * BLOCK), 0) % BLOCK
    kj = lax.broadcasted_iota(jnp.int32, (rows, 2 * BLOCK), 1)
    dist = BLOCK + qi - kj
    band = (dist >= 0) & (dist <= max_dist)
    inner = jnp.where(band, 0.0, NEG_BIG)
    return jnp.where(band & ((kj >= BLOCK) | jnp.logical_not(first_tile)), 0.0, NEG_BIG), inner


def _window_rows(prev_ref, cur_ref, jb, lanes):
    if jb == 0:
        return jnp.concatenate([prev_ref[:, lanes], cur_ref[0:BLOCK, lanes]], axis=0)
    return cur_ref[(jb - 1) * BLOCK:(jb + 1) * BLOCK, lanes]


def _attn_a_kernel(sink_ref, q_ref, kvp_ref, kvc_ref, o_ref):
    nb = q_ref.shape[0] // BLOCK
    rows = A_GROUP * BLOCK
    first_tile = pl.program_id(1) == 0
    qi = lax.broadcasted_iota(jnp.int32, (rows, BLOCK), 0) % BLOCK
    from_prev = lax.broadcasted_iota(jnp.int32, (rows, BLOCK), 1) > qi
    low = lax.broadcasted_iota(jnp.int32, (BLOCK, LANES), 1) < HEAD_DIM
    rowg = lax.broadcasted_iota(jnp.int32, (rows, 1), 0) // BLOCK
    sinks = []
    for hk in range(A_KV_HEADS):
        sk = jnp.zeros((rows, 1), F32)
        for g in range(A_GROUP):
            sk = jnp.where(rowg == g, sink_ref[hk * A_GROUP + g], sk)
        sinks.append(sk)
    kvh = range(A_KV_HEADS)
    for jb in range(nb):
        rs = slice(jb * BLOCK, (jb + 1) * BLOCK)
        sc = []
        for hk in kvh:
            tiles = [q_ref[rs, (2 * hk + t) * LANES:(2 * hk + t + 1) * LANES] for t in range(2)]
            zero = jnp.zeros_like(tiles[0])
            q4 = jnp.concatenate([jnp.where(low, tiles[0], zero), jnp.where(low, zero, tiles[0]),
                                  jnp.where(low, tiles[1], zero), jnp.where(low, zero, tiles[1])], axis=0)
            k = _window_rows(kvp_ref, kvc_ref, jb, slice(hk * LANES, (hk + 1) * LANES))
            both = _dot_nt(q4, k)
            prev = both[:, :BLOCK]
            if jb == 0:
                prev = jnp.where(first_tile, NEG_BIG, prev)
            sc.append(jnp.where(from_prev, prev, both[:, BLOCK:]))
        m = [jnp.maximum(jnp.max(s, axis=-1, keepdims=True), sk) for s, sk in zip(sc, sinks)]
        p = [jnp.exp(s - mm) for s, mm in zip(sc, m)]
        den = [jnp.sum(pp, axis=-1, keepdims=True) + jnp.exp(sk - mm) for pp, sk, mm in zip(p, sinks, m)]
        o = []
        for hk in kvh:
            v = _window_rows(kvp_ref, kvc_ref, jb, slice((A_KV_HEADS + hk) * LANES, (A_KV_HEADS + hk + 1) * LANES))
            pb = p[hk].astype(BF16)
            zero = jnp.zeros_like(pb)
            unfolded = jnp.concatenate([jnp.where(from_prev, pb, zero), jnp.where(from_prev, zero, pb)], axis=1)
            o.append(_dot(unfolded, v) * (1.0 / den[hk]))
        for hk in kvh:
            for t in range(2):
                pair = jnp.where(low, o[hk][2 * t * BLOCK:(2 * t + 1) * BLOCK], o[hk][(2 * t + 1) * BLOCK:(2 * t + 2) * BLOCK])
                o_ref[rs, (2 * hk + t) * LANES:(2 * hk + t + 1) * LANES] = pair.astype(BF16)


def _attn_a(qa, kva, sinks, tile=1024):
    bsz, s, _ = qa.shape
    tile = min(tile, s)
    kvw = kva.shape[2]
    per = tile // BLOCK
    return pl.pallas_call(
        _attn_a_kernel,
        grid=(bsz, s // tile),
        in_specs=[pl.BlockSpec(memory_space=pltpu.SMEM),
                  pl.BlockSpec((None, tile, A_Q_W), lambda b, i: (b, i, 0)),
                  pl.BlockSpec((None, BLOCK, kvw), lambda b, i: (b, jnp.maximum(i * per - 1, 0), 0)),
                  pl.BlockSpec((None, tile, kvw), lambda b, i: (b, i, 0))],
        out_specs=pl.BlockSpec((None, tile, A_Q_W), lambda b, i: (b, i, 0)),
        out_shape=jax.ShapeDtypeStruct((bsz, s, A_Q_W), BF16),
        compiler_params=_cparams("parallel", "parallel"),
        name="attn_a",
    )(sinks, qa, kva, kva)


def _attn_b_kernel(group, q_ref, kp_ref, kc_ref, vp_ref, vc_ref, o_ref, l_ref):
    nb = q_ref.shape[0] // BLOCK
    bias_first, bias_inner = _band_bias(BLOCK, BLOCK, pl.program_id(2) == 0)
    low = lax.broadcasted_iota(jnp.int32, (BLOCK, LANES), 1) < HEAD_DIM
    work = [(res, jb) for res in range(q_ref.shape[1] // B_PW) for jb in range(nb)]
    for w0 in range(0, len(work), group):
        items = [(res, jb, h) for res, jb in work[w0:w0 + group] for h in range(B_HEADS)]
        lanes = [slice(res * B_PW + (h // 2) * LANES, res * B_PW + (h // 2 + 1) * LANES) for res, _, h in items]
        sc = []
        for (_, jb, h), ls in zip(items, lanes):
            qt = q_ref[jb * BLOCK:(jb + 1) * BLOCK, ls]
            zero = jnp.zeros_like(qt)
            qh = jnp.where(low, qt, zero) if h % 2 == 0 else jnp.where(low, zero, qt)
            sc.append(_dot_nt(qh, _window_rows(kp_ref, kc_ref, jb, ls)) + (bias_first if jb == 0 else bias_inner))
        m = [jnp.max(s, axis=-1, keepdims=True) for s in sc]
        p = [jnp.exp(s - mm) for s, mm in zip(sc, m)]
        den = [jnp.sum(pp, axis=-1, keepdims=True) for pp in p]
        o = [_dot(pp.astype(BF16), _window_rows(vp_ref, vc_ref, jb, ls)) * (1.0 / dd)
             for pp, dd, (_, jb, _), ls in zip(p, den, items, lanes)]
        lse = [mm + jnp.log(dd) for mm, dd in zip(m, den)]
        for idx in range(0, len(items), 2):
            _, jb, _ = items[idx]
            dst = (slice(jb * BLOCK, (jb + 1) * BLOCK), lanes[idx])
            o_ref[dst] = jnp.where(low, o[idx], o[idx + 1])
            l_ref[dst] = jnp.where(low, lse[idx], lse[idx + 1])


def _attn_b(q, k, v, pat, dil, tile=1024, group=2):
    bsz, length, _ = q.shape
    res = max(1, min(tile // length, dil))
    tile = min(tile, length)
    per = tile // BLOCK
    cur = pl.BlockSpec((None, tile, res * B_PW), lambda b, r, i: (b, i, r))
    prev = pl.BlockSpec((None, BLOCK, res * B_PW), lambda b, r, i: (b, jnp.maximum(i * per - 1, 0), r))
    return pl.pallas_call(
        functools.partial(_attn_b_kernel, group),
        grid=(bsz, dil // res, length // tile),
        in_specs=[cur, prev, cur, prev, cur],
        out_specs=[cur, cur],
        out_shape=[jax.ShapeDtypeStruct((bsz, length, dil * B_PW), F32)] * 2,
        compiler_params=_cparams("parallel", "parallel", "parallel"),
        name=f"attn_b{pat}",
    )(q, k, k, v, v)


def _attn_out_kernel(x_ref, oa_ref, o0_ref, o1_ref, o2_ref, l0_ref, l1_ref, l2_ref, w_ref,
                     gf_ref, wg_ref, wu_ref, wd_ref, out_ref, buf, acc_ref):
    tm = x_ref.shape[0]

    def natural(ref, dil, slot):
        if dil == 1:
            return ref[...]
        n = tm // dil
        halves = B_PW // LANES
        for r in range(dil):
            for j in range(halves):
                buf[slot * halves + j, pl.ds(r, n, stride=dil), :] = ref[:, r * B_PW + j * LANES:r * B_PW + (j + 1) * LANES]
        return jnp.concatenate([buf[slot * halves + j] for j in range(halves)], axis=1)

    dils = [dil for _, dil in B_PATTERNS]
    o = [natural(ref, dil, i) for i, (ref, dil) in enumerate(zip((o0_ref, o1_ref, o2_ref), dils))]
    l = [natural(ref, dil, 3 + i) for i, (ref, dil) in enumerate(zip((l0_ref, l1_ref, l2_ref), dils))]
    m = jnp.maximum(jnp.maximum(l[0], l[1]), l[2])
    e = [jnp.exp(v - m) for v in l]
    ob = (e[0] * o[0] + e[1] * o[1] + e[2] * o[2]) / (e[0] + e[1] + e[2])
    mix = _dot(oa_ref[...], w_ref[:A_Q_W, :]) + _dot(ob.astype(BF16), w_ref[A_Q_W:, :])
    out_ref[...] = _ffn_block(x_ref[...] + mix, gf_ref, wg_ref, wu_ref, wd_ref, acc_ref)


def _attn_out(x, oa, obs, lses, w_out, ffn, tm=512):
    bsz, s, d = x.shape
    tok = lambda w: pl.BlockSpec((None, tm, w), lambda b, i: (b, i, 0))
    pat = [pl.BlockSpec((None, tm // dil, dil * B_PW), lambda b, i: (b, i, 0)) for _, dil in B_PATTERNS]
    return pl.pallas_call(
        _attn_out_kernel,
        grid=(bsz, s // tm),
        in_specs=[tok(d), tok(A_Q_W)] + pat + pat + [_resident(a.shape) for a in (w_out,) + ffn],
        out_specs=tok(d),
        out_shape=jax.ShapeDtypeStruct((bsz, s, d), F32),
        scratch_shapes=[pltpu.VMEM((2 * len(B_PATTERNS) * (B_PW // LANES), tm, LANES), F32),
                        pltpu.VMEM((tm, d), F32)],
        compiler_params=_cparams("parallel", "parallel"),
        name="attn_out_ffn",
    )(x, oa, *obs, *lses, w_out, *ffn)


FF_CHUNK = 256


def _ffn_block(x, g_ref, wg_ref, wu_ref, wd_ref, acc_ref):
    h = _rms(x, g_ref[...]).astype(BF16)
    for c in range(wg_ref.shape[1] // FF_CHUNK):
        sl = slice(c * FF_CHUNK, (c + 1) * FF_CHUNK)
        gate = _dot(h, wg_ref[:, sl])
        up = _dot(h, wu_ref[:, sl])
        contrib = _dot((gate * _sigmoid(gate) * up).astype(BF16), wd_ref[sl, :])
        if c == 0:
            acc_ref[...] = x + contrib
        else:
            acc_ref[...] += contrib
    return acc_ref[...]


def _resident(shape):
    return pl.BlockSpec(shape, lambda *_: (0,) * len(shape), pipeline_mode=pl.Buffered(1))


def _rwkv_in_kernel(has_vlora, *refs):
    (x_ref, gn_ref, mu_ref, wr_ref, wk_ref, wv_ref, w0_ref, w1_ref, w2_ref, a0_ref, a1_ref, a2_ref,
     g1_ref, g2_ref, kk_ref, ka_ref) = refs[:16]
    refs = refs[16:]
    if has_vlora:
        v0_ref, v1_ref, v2_ref, vf_ref = refs[:4]
        refs = refs[4:]
    r_o, lw_o, k_o, v_o, kk_o, ab_o, g_o, carry, mix = refs
    tm, d = x_ref.shape

    @pl.when(pl.program_id(1) == 0)
    def _():
        carry[...] = jnp.zeros(carry.shape, F32)

    h = _rms(x_ref[...], gn_ref[...])
    rolled = pltpu.roll(h, 1, 0)
    first = lax.broadcasted_iota(jnp.int32, (8, 1), 0) == 0
    hprev = jnp.concatenate([jnp.where(first, carry[7:8, :], rolled[0:8]), rolled[8:]], axis=0)
    carry[...] = h[tm - 8:tm, :]
    hb = h.astype(BF16)
    xxb = (hprev - h).astype(BF16)
    mub = mu_ref[...].astype(BF16)
    for i in range(mix.shape[0]):
        mix[i] = hb + xxb * mub[i:i + 1, :]

    mid_w = jnp.tanh(_dot(mix[1], w1_ref[...])).astype(BF16)
    mid_a = _dot(mix[4], a1_ref[...]).astype(BF16)
    mid_g = _sigmoid(_dot(mix[5], g1_ref[...])).astype(BF16)
    if has_vlora:
        mid_v = _dot(mix[3], v1_ref[...]).astype(BF16)

    ones_bd = jnp.where(_head_mask(MXU_W), 1.0, 0.0).astype(BF16)
    for c in range(d // MXU_W):
        cs = slice(c * MXU_W, (c + 1) * MXU_W)
        r_o[:, cs] = _dot(mix[0], wr_ref[:, cs])
        lw_o[:, cs] = -DECAY_SCALE * _sigmoid(w0_ref[:, cs] + _dot(mid_w, w2_ref[:, cs]))
        v = _dot(mix[3], wv_ref[:, cs])
        if has_vlora:
            v = v + (vf_ref[:, cs] - v) * _sigmoid(v0_ref[:, cs] + _dot(mid_v, v2_ref[:, cs]))
        v_o[:, cs] = v
        g_o[:, cs] = _dot(mid_g, g2_ref[:, cs]).astype(BF16)
        k = _dot(mix[2], wk_ref[:, cs])
        a = _sigmoid(a0_ref[:, cs] + _dot(mid_a, a2_ref[:, cs]))
        kk = k * kk_ref[:, cs]
        kk = kk / jnp.maximum(jnp.sqrt(_head_sum(kk * kk, ones_bd)), 1e-12)
        kk_o[:, cs] = kk
        ab_o[:, cs] = kk * a
        k_o[:, cs] = k * (1.0 + (a - 1.0) * ka_ref[:, cs])


def _rwkv_in(x, gn, p, v_lora, v_first, tm=512):
    bsz, s, d = x.shape
    tok = pl.BlockSpec((None, tm, d), lambda b, i: (b, i, 0))
    row = lambda a: a.reshape(1, d)
    args = [x, row(gn), p["mu"], p["w_r"], p["w_k"], p["w_v"], row(p["w0"]), p["w1"], p["w2"], row(p["a0"]), p["a1"], p["a2"],
            p["g1"], p["g2"], row(p["k_k"]), row(p["k_a"])]
    specs = [tok] + [_resident(a.shape) for a in args[1:]]
    if v_lora is not None:
        v0, v1, v2 = v_lora
        extra = [row(v0), v1, v2]
        args += extra + [v_first]
        specs += [_resident(a.shape) for a in extra] + [tok]
    return pl.pallas_call(
        functools.partial(_rwkv_in_kernel, v_lora is not None),
        grid=(bsz, s // tm),
        in_specs=specs,
        out_specs=[tok] * 7,
        out_shape=[jax.ShapeDtypeStruct((bsz, s, d), F32)] * 6 + [jax.ShapeDtypeStruct((bsz, s, d), BF16)],
        scratch_shapes=[pltpu.VMEM((8, d), F32), pltpu.VMEM((p["mu"].shape[0], tm, d), BF16)],
        compiler_params=_cparams("parallel", "arbitrary"),
        name="rwkv_in",
    )(*args)


def _bd2(z):
    lane = lax.broadcasted_iota(jnp.int32, z.shape, 1)
    zero = jnp.zeros(z.shape, z.dtype)
    return jnp.concatenate([jnp.where(lane < HEAD_DIM, z, zero), jnp.where(lane >= HEAD_DIM, z, zero)], axis=0)


def _fold2(full):
    lane = lax.broadcasted_iota(jnp.int32, (HEAD_DIM, LANES), 1)
    return jnp.where(lane < HEAD_DIM, full[:HEAD_DIM], full[HEAD_DIM:])


def _wkv_prep_kernel(r_ref, lw_ref, k_ref, v_ref, kk_ref, ab_ref, rk_ref,
                     q_ref, y0_ref, mc_ref, z_ref, dec_ref, bonus_ref):
    cs = WKV_CHUNK
    rows, d = r_ref.shape
    nc = rows // cs
    ri = lax.broadcasted_iota(jnp.int32, (rows, rows), 0)
    ci = lax.broadcasted_iota(jnp.int32, (rows, rows), 1)
    tri = jnp.where((ri >= ci) & (ri // cs == ci // cs), 1.0, 0.0).astype(BF16)
    lw = lw_ref[...]
    hi = lw.astype(BF16)
    rem = lw - hi.astype(F32)
    mid = rem.astype(BF16)
    lo = (rem - mid.astype(F32)).astype(BF16)
    cum = _dot(tri, hi) + _dot(tri, mid) + _dot(tri, lo)
    totals = [cum[(j + 1) * cs - 1:(j + 1) * cs, :] for j in range(nc)]
    tot = jnp.concatenate([jnp.broadcast_to(t, (cs, d)) for t in totals], axis=0)
    e_neg = jnp.exp(-cum)
    e_tot = jnp.exp(tot - cum)
    kk, ab, kx = kk_ref[...], ab_ref[...], k_ref[...]
    at = (-kk * jnp.exp(cum - lw)).astype(BF16)
    rt = r_ref[...] * jnp.exp(cum)
    rtb = rt.astype(BF16)
    bt = (ab * e_neg).astype(BF16)
    kt = (kx * e_neg).astype(BF16)
    bh = (ab * e_tot).astype(BF16)
    kh = (kx * e_tot).astype(BF16)
    vb = v_ref[...].astype(BF16)
    ones_bd = jnp.where(_head_mask(MXU_W), 1.0, 0.0).astype(BF16)
    bonus_ref[...] = _head_sum(r_ref[...] * kx * rk_ref[...], ones_bd, split=False) * v_ref[...]

    trow = lax.broadcasted_iota(jnp.int32, (cs, LANES), 0)
    tcol = lax.broadcasted_iota(jnp.int32, (cs, LANES), 1) % HEAD_DIM
    strict = trow > tcol
    incl = trow >= tcol
    eye = jnp.where(trow == tcol, 1.0, 0.0)

    for j in range(nc):
        dec_ref[8 * j:8 * j + 8, :] = jnp.broadcast_to(jnp.exp(totals[j]), (8, d))

    chains = [(slice(j * cs, (j + 1) * cs), slice(p * LANES, (p + 1) * LANES))
              for j in range(nc) for p in range(d // LANES)]

    g12 = [_dot_nt(jnp.concatenate([at[c], rtb[c]], axis=0),
                   jnp.concatenate([_bd2(bt[c]), _bd2(kt[c])], axis=0)) for c in chains]
    a_ab = [jnp.where(strict, g[:cs, :LANES], 0.0) for g in g12]
    a_rb = [jnp.where(incl, g[cs:, :LANES], 0.0).astype(BF16) for g in g12]
    a_k = [jnp.concatenate([jnp.where(strict, g[:cs, LANES:], 0.0),
                            jnp.where(incl, g[cs:, LANES:], 0.0)], axis=0).astype(BF16) for g in g12]
    g4 = [_dot(a, _bd2(vb[c])) for a, c in zip(a_k, chains)]
    x = [eye + jnp.where((trow - tcol == 1) & (trow % 2 == 1), a, 0.0) for a in a_ab]
    size = 2
    while size < cs:
        lower_left = (trow // size - tcol // size == 1) & ((trow // size) % 2 == 1)
        xb = [v.astype(BF16) for v in x]
        xn = [_dot(v, _bd2(jnp.where(lower_left, a, 0.0).astype(BF16))) for v, a in zip(xb, a_ab)]
        x = [v + _dot(n.astype(BF16), _bd2(vb16)) for v, n, vb16 in zip(x, xn, xb)]
        size *= 2
    wu = [_dot(v.astype(BF16), jnp.concatenate([_bd2(at[c]), _bd2(g[:cs].astype(BF16))], axis=1))
          for v, g, c in zip(x, g4, chains)]
    wb = [v[:, :LANES].astype(BF16) for v in wu]
    ub = [v[:, LANES:].astype(BF16) for v in wu]
    qy = [_dot(a, jnp.concatenate([_bd2(w), _bd2(u)], axis=1)) for a, w, u in zip(a_rb, wb, ub)]
    mc = [_dot_tn(w, bh[c]) for w, c in zip(wb, chains)]
    zz = [_dot_tn(jnp.concatenate([u, vb[c]], axis=0), jnp.concatenate([bh[c], kh[c]], axis=0))
          for u, c in zip(ub, chains)]
    for i, c in enumerate(chains):
        q_ref[c] = (rt[c] + qy[i][:, :LANES]).astype(BF16)
        y0_ref[c] = g4[i][cs:] + qy[i][:, LANES:]
        mc_ref[c] = _fold2(mc[i]).astype(BF16)
        z_ref[c] = _fold2(zz[i])


def _wkv_out_kernel(final_norm, q_ref, y0_ref, mc_ref, z_ref, dec_ref, x_ref, bonus_ref, g_ref, lnw_ref, lnb_ref,
                    wo_ref, gf_ref, wg_ref, wu_ref, wd_ref, gl_ref, out_ref, s_ref, y_buf, acc_ref):
    cs = WKV_CHUNK
    bsz, rows, d = q_ref.shape

    @pl.when(pl.program_id(0) == 0)
    def _():
        s_ref[...] = jnp.zeros(s_ref.shape, F32)

    seqs = [(b, slice(p * LANES, (p + 1) * LANES)) for b in range(bsz) for p in range(d // LANES)]
    state = [s_ref[b, :, ls] for b, ls in seqs]
    for j in range(rows // cs):
        rs = slice(j * cs, (j + 1) * cs)
        sb = [s.astype(BF16) for s in state]
        upd = [_dot(v, _bd2(mc_ref[b, rs, ls])) for v, (b, ls) in zip(sb, seqs)]
        for v, (b, ls) in zip(sb, seqs):
            y_buf[b * rows + j * cs:b * rows + (j + 1) * cs, ls] = _dot_nt(q_ref[b, rs, ls], _bd2(v)) + y0_ref[b, rs, ls]
        state = [s * dec_ref[b, 8 * j:8 * j + 1, ls] + u + z_ref[b, rs, ls]
                 for s, u, (b, ls) in zip(state, upd, seqs)]
    for s, (b, ls) in zip(state, seqs):
        s_ref[b, :, ls] = s

    ones_bd = jnp.where(_head_mask(MXU_W), 1.0, 0.0).astype(BF16)
    y = y_buf[...]
    dev = y - _head_sum(y, ones_bd, split=False) * (1.0 / HEAD_DIM)
    var = _head_sum(dev * dev, ones_bd, split=False) * (1.0 / HEAD_DIM)
    yn = dev * lax.rsqrt(var + LNX_EPS) * lnw_ref[...] + lnb_ref[...]
    gated = jnp.concatenate([((yn[b * rows:(b + 1) * rows] + bonus_ref[b]) * g_ref[b]).astype(BF16)
                             for b in range(bsz)], axis=0)
    x1 = jnp.concatenate([x_ref[b] for b in range(bsz)], axis=0) + _dot(gated, wo_ref[...])
    out = _ffn_block(x1, gf_ref, wg_ref, wu_ref, wd_ref, acc_ref)
    if final_norm:
        out = _rms(out, gl_ref[...])
    for b in range(bsz):
        out_ref[b] = out[b * rows:(b + 1) * rows]


def _wkv(x, r, lw, k, v, kk, ab, g, rk, lnw, lnb, wo, ffn, g_last, final_norm, prep_chunks=4, scan_chunks=2):
    bsz, s, d = r.shape
    cs = WKV_CHUNK
    rows = prep_chunks * cs
    blk = pl.BlockSpec((None, rows, d), lambda b, c: (b, c, 0))
    dec_blk = pl.BlockSpec((None, 8 * prep_chunks, d), lambda b, c: (b, c, 0))
    act = lambda dt: jax.ShapeDtypeStruct((bsz, s, d), dt)
    q, y0, mc, z, dec, bonus = pl.pallas_call(
        _wkv_prep_kernel,
        grid=(bsz, s // rows),
        in_specs=[blk] * 6 + [pl.BlockSpec((1, d), lambda b, c: (0, 0))],
        out_specs=[blk, blk, blk, blk, dec_blk, blk],
        out_shape=[act(BF16), act(F32), act(BF16), act(F32),
                   jax.ShapeDtypeStruct((bsz, 8 * s // cs, d), F32), act(F32)],
        compiler_params=_cparams("parallel", "parallel"),
        name="wkv7_prep",
    )(r, lw, k, v, kk, ab, rk.reshape(1, d))
    rows = scan_chunks * cs
    blk = pl.BlockSpec((bsz, rows, d), lambda c: (0, c, 0))
    dec_blk = pl.BlockSpec((bsz, 8 * scan_chunks, d), lambda c: (0, c, 0))
    row = pl.BlockSpec((1, d), lambda c: (0, 0))
    return pl.pallas_call(
        functools.partial(_wkv_out_kernel, final_norm),
        grid=(s // rows,),
        in_specs=[blk, blk, blk, blk, dec_blk, blk, blk, blk, row, row]
                 + [_resident(a.shape) for a in (wo,) + ffn] + [row],
        out_specs=blk,
        out_shape=act(F32),
        scratch_shapes=[pltpu.VMEM((bsz, HEAD_DIM, d), F32), pltpu.VMEM((bsz * rows, d), F32),
                        pltpu.VMEM((bsz * rows, d), F32)],
        compiler_params=_cparams("arbitrary"),
        name="wkv7_out_ffn",
    )(q, y0, mc, z, dec, x, bonus, g, lnw.reshape(1, d), lnb.reshape(1, d), wo, *ffn, g_last.reshape(1, d))


def kernel(x, positions, norm_mix, norm_ffn, norm_final, attn_w_in, attn_b_in, attn_sinks, attn_w_out, rwkv_mu, rwkv_w_rkv, rwkv_w0, rwkv_w1, rwkv_w2, rwkv_a0, rwkv_a1, rwkv_a2, rwkv_g1, rwkv_g2, rwkv_k_k, rwkv_k_a, rwkv_r_k, rwkv_lnx_w, rwkv_lnx_b, rwkv_w_o, rwkv_v0, rwkv_v1, rwkv_v2, ffn_w_gate, ffn_w_up, ffn_w_down):
    depth = norm_mix.shape[0]
    bf = lambda a: a.astype(BF16)
    tables = _rope_tables(positions)
    v_first = None
    for layer in range(depth):
        i = layer // 2
        ffn = (norm_ffn[layer].reshape(1, -1), bf(ffn_w_gate[layer]), bf(ffn_w_up[layer]), bf(ffn_w_down[layer]))
        if layer % 2 == 0:
            qa, kva, qkv_b = _attn_in(x, norm_mix[layer], _widen_in_proj(bf(attn_w_in[i])),
                                      _widen_in_proj(attn_b_in[i]), tables)
            oa = _attn_a(qa, kva, attn_sinks[i])
            obs, lses = zip(*[_attn_b(*qkv_b[pat], pat, dil) for pat, (_, dil) in enumerate(B_PATTERNS)])
            x = _attn_out(x, oa, obs, lses, bf(attn_w_out[i]), ffn)
        else:
            p = dict(mu=rwkv_mu[i], w_r=bf(rwkv_w_rkv[i, 0]), w_k=bf(rwkv_w_rkv[i, 1]), w_v=bf(rwkv_w_rkv[i, 2]),
                     w0=rwkv_w0[i], w1=bf(rwkv_w1[i]), w2=bf(rwkv_w2[i]),
                     a0=rwkv_a0[i], a1=bf(rwkv_a1[i]), a2=bf(rwkv_a2[i]), g1=bf(rwkv_g1[i]), g2=bf(rwkv_g2[i]),
                     k_k=rwkv_k_k[i], k_a=rwkv_k_a[i])
            v_lora = None if i == 0 else (rwkv_v0[i - 1], bf(rwkv_v1[i - 1]), bf(rwkv_v2[i - 1]))
            r, lw, k, v, kk, ab, g = _rwkv_in(x, norm_mix[layer], p, v_lora, v_first)
            if i == 0:
                v_first = v
            x = _wkv(x, r, lw, k, v, kk, ab, g, rwkv_r_k[i], rwkv_lnx_w[i], rwkv_lnx_b[i], bf(rwkv_w_o[i]),
                     ffn, norm_final, final_norm=(layer == depth - 1))
    return x
```

```python
import functools

import jax
import jax.numpy as jnp
from jax import lax
from jax.experimental import pallas as pl
from jax.experimental.pallas import tpu as pltpu

F32 = jnp.float32
BF16 = jnp.bfloat16

HEAD_DIM = 64
ROT_DIM = HEAD_DIM // 4
ROT_HALF = ROT_DIM // 2
ROPE_THETA = 500000.0
BLOCK = 128
NORM_EPS = 1e-5
LNX_EPS = 64e-5

A_Q_HEADS = 12
A_KV_HEADS = 3
A_GROUP = A_Q_HEADS // A_KV_HEADS
A_WINDOW = 128
B_PATTERNS = ((128, 1), (512, 4), (2048, 16))
B_HEADS = 4

A_Q_W = A_Q_HEADS * HEAD_DIM
A_KV_W = A_KV_HEADS * HEAD_DIM
B_PW = B_HEADS * HEAD_DIM
B_W = len(B_PATTERNS) * B_PW
KV_DUP_W = 2 * A_KV_W
OFF_KA = A_Q_W
OFF_QB = OFF_KA + 2 * A_KV_W
OFF_KB = OFF_QB + B_W
OFF_VB = OFF_KB + B_W

LANES = 128
MXU_W = 256
WKV_CHUNK = 64
DECAY_SCALE = 0.6065306597126334
NEG_BIG = -1e30
VMEM_LIMIT = 56 * 1024 * 1024


def _cparams(*sem):
    return pltpu.CompilerParams(dimension_semantics=sem, vmem_limit_bytes=VMEM_LIMIT)


def _dot(a, b):
    return jnp.dot(a, b, preferred_element_type=F32)


def _dot_nt(a, b):
    return lax.dot_general(a, b, (((1,), (1,)), ((), ())), preferred_element_type=F32)


def _dot_tn(a, b):
    return lax.dot_general(a, b, (((0,), (0,)), ((), ())), preferred_element_type=F32)


def _row_sum(z):
    part = z[:, :LANES]
    for j in range(1, z.shape[1] // LANES):
        part = part + z[:, j * LANES:(j + 1) * LANES]
    ones = jnp.ones((LANES, LANES), BF16)
    hi = part.astype(BF16)
    lo = (part - hi.astype(F32)).astype(BF16)
    return _dot(hi, ones) + _dot(lo, ones)


def _rms(x, g):
    inv = lax.rsqrt(_row_sum(x * x) * (1.0 / x.shape[1]) + NORM_EPS)
    return x * jnp.concatenate([inv] * (x.shape[1] // LANES), axis=1) * g


def _sigmoid(z):
    return 1.0 / (1.0 + jnp.exp(-z))


def _head_mask(n):
    r = lax.broadcasted_iota(jnp.int32, (n, n), 0) // HEAD_DIM
    c = lax.broadcasted_iota(jnp.int32, (n, n), 1) // HEAD_DIM
    return r == c


def _head_sum(z, ones_bd, split=True):
    outs = []
    for j in range(z.shape[1] // MXU_W):
        zc = z[:, j * MXU_W:(j + 1) * MXU_W]
        hi = zc.astype(BF16)
        acc = _dot(hi, ones_bd)
        if split:
            acc = acc + _dot((zc - hi.astype(F32)).astype(BF16), ones_bd)
        outs.append(acc)
    return jnp.concatenate(outs, axis=1)


def _rope_trig_kernel(pos_ref, invf_ref, cos_ref, sin_ref):
    ang = pos_ref[...].astype(F32)[None] * invf_ref[...]
    cos_ref[...] = jnp.cos(ang)
    sin_ref[...] = jnp.sin(ang)


def _rope_tables(positions):
    bsz, s = positions.shape
    rows = bsz * s // LANES
    inv_freq = jnp.power(ROPE_THETA, -2.0 * jnp.arange(ROT_HALF, dtype=F32) / ROT_DIM)
    invf = jnp.broadcast_to(inv_freq[:, None, None], (ROT_HALF, 1, LANES))
    cos, sin = pl.pallas_call(
        _rope_trig_kernel,
        out_shape=(jax.ShapeDtypeStruct((ROT_HALF, rows, LANES), F32),) * 2,
        name="rope_trig",
    )(positions.reshape(rows, LANES), invf)
    cos = jnp.tile(cos.reshape(ROT_HALF, bsz, s).transpose(1, 2, 0), (1, 1, LANES // ROT_HALF))
    sin = jnp.tile(sin.reshape(ROT_HALF, bsz, s).transpose(1, 2, 0), (1, 1, LANES // ROT_HALF))
    dim = jnp.arange(LANES) % HEAD_DIM
    c = jnp.where(dim < ROT_DIM, cos, 1.0)
    sa = jnp.where(dim < ROT_HALF, -sin, 0.0)
    sb = jnp.where((dim >= ROT_HALF) & (dim < ROT_DIM), sin, 0.0)
    return c, sa, sb


def _residue_major(ref, dil):
    n = ref.shape[0] // dil
    return jnp.concatenate([ref[pl.ds(r, n, stride=dil), :] for r in range(dil)], axis=0)


def _attn_in_kernel(x_ref, g_ref, w_ref, b_ref, c_ref, sa_ref, sb_ref, qa_ref, kva_ref, *rest):
    b_refs, hbuf = rest[:-1], rest[-1]
    hf = _rms(x_ref[...], g_ref[...])
    tm, d = hf.shape
    for j in range(d // LANES):
        hbuf[j] = hf[:, j * LANES:(j + 1) * LANES]
    scale = HEAD_DIM ** -0.5

    def proj(h, lo, width):
        return _dot(h, w_ref[:, lo:lo + width]) + b_ref[:, lo:lo + width]

    def rope(z, tabs):
        c, sa, sb = tabs
        return z * c + pltpu.roll(z, LANES - ROT_HALF, 1) * sa + pltpu.roll(z, ROT_HALF, 1) * sb

    def rope_all(z, tabs):
        return jnp.concatenate([rope(z[:, j * LANES:(j + 1) * LANES], tabs)
                                for j in range(z.shape[1] // LANES)], axis=1)

    h = hf.astype(BF16)
    tabs = (c_ref[...], sa_ref[...], sb_ref[...])
    for j in range(A_Q_W // MXU_W):
        qa_ref[:, j * MXU_W:(j + 1) * MXU_W] = (rope_all(proj(h, j * MXU_W, MXU_W), tabs) * scale).astype(BF16)
    z = proj(h, OFF_KA, 2 * A_KV_W)
    low = lax.broadcasted_iota(jnp.int32, (tm, LANES), 1) < HEAD_DIM
    mid = z[:, LANES:2 * LANES]
    src = [rope(z[:, :LANES], tabs), jnp.where(low, rope(mid, tabs), mid), z[:, 2 * LANES:]]
    for j, t in enumerate(src):
        swapped = pltpu.roll(t, HEAD_DIM, 1)
        kva_ref[:, 2 * j * LANES:(2 * j + 1) * LANES] = jnp.where(low, t, swapped).astype(BF16)
        kva_ref[:, (2 * j + 1) * LANES:(2 * j + 2) * LANES] = jnp.where(low, swapped, t).astype(BF16)
    for pat, (_, dil) in enumerate(B_PATTERNS):
        q_ref, k_ref, v_ref = b_refs[3 * pat:3 * pat + 3]
        if dil > 1:
            h = jnp.concatenate([_residue_major(hbuf.at[j], dil) for j in range(d // LANES)],
                                axis=1).astype(BF16)
            tabs = tuple(_residue_major(t, dil) for t in (c_ref, sa_ref, sb_ref))
        q = (rope_all(proj(h, OFF_QB + pat * B_PW, B_PW), tabs) * scale).astype(BF16)
        k = rope_all(proj(h, OFF_KB + pat * B_PW, B_PW), tabs).astype(BF16)
        v = proj(h, OFF_VB + pat * B_PW, B_PW).astype(BF16)
        n = tm // dil
        for r in range(dil):
            q_ref[:, r * B_PW:(r + 1) * B_PW] = q[r * n:(r + 1) * n]
            k_ref[:, r * B_PW:(r + 1) * B_PW] = k[r * n:(r + 1) * n]
            v_ref[:, r * B_PW:(r + 1) * B_PW] = v[r * n:(r + 1) * n]


def _attn_in(x, g, w_in, b_in, tables, tm=1024):
    bsz, s, d = x.shape
    in_w = w_in.shape[1]
    tok = lambda w: pl.BlockSpec((None, tm, w), lambda b, i: (b, i, 0))
    full = _resident
    out_specs = [tok(A_Q_W), tok(2 * KV_DUP_W)]
    out_shape = [jax.ShapeDtypeStruct((bsz, s, A_Q_W), BF16), jax.ShapeDtypeStruct((bsz, s, 2 * KV_DUP_W), BF16)]
    for _, dil in B_PATTERNS:
        out_specs += [pl.BlockSpec((None, tm // dil, dil * B_PW), lambda b, i: (b, i, 0))] * 3
        out_shape += [jax.ShapeDtypeStruct((bsz, s // dil, dil * B_PW), BF16)] * 3
    outs = pl.pallas_call(
        _attn_in_kernel,
        grid=(bsz, s // tm),
        in_specs=[tok(d), full((1, d)), full((d, in_w)), full((1, in_w)),
                  tok(LANES), tok(LANES), tok(LANES)],
        out_specs=out_specs,
        out_shape=out_shape,
        scratch_shapes=[pltpu.VMEM((d // LANES, tm, LANES), F32)],
        compiler_params=_cparams("parallel", "parallel"),
        name="attn_in",
    )(x, g.reshape(1, d), w_in, b_in.reshape(1, in_w), *tables)
    return outs[0], outs[1], [outs[2 + 3 * p:5 + 3 * p] for p in range(len(B_PATTERNS))]


def _band_bias(rows, max_dist, first_tile):
    qi = lax.broadcasted_iota(jnp.int32, (rows, 2 * BLOCK), 0) % BLOCK
    kj = lax.broadcasted_iota(jnp.int32, (rows, 2 * BLOCK), 1)
    dist = BLOCK + qi - kj
    band = (dist >= 0) & (dist <= max_dist)
    inner = jnp.where(band, 0.0, NEG_BIG)
    return jnp.where(band & ((kj >= BLOCK) | jnp.logical_not(first_tile)), 0.0, NEG_BIG), inner


def _window_rows(prev_ref, cur_ref, jb, lanes):
    if jb == 0:
        return jnp.concatenate([prev_ref[:, lanes], cur_ref[0:BLOCK, lanes]], axis=0)
    return cur_ref[(jb - 1) * BLOCK:(jb + 1) * BLOCK, lanes]


def _attn_a_kernel(sink_ref, q_ref, kvp_ref, kvc_ref, o_ref):
    nb = q_ref.shape[0] // BLOCK
    rows = A_GROUP * BLOCK
    first_tile = pl.program_id(1) == 0
    qi = lax.broadcasted_iota(jnp.int32, (rows, BLOCK), 0) % BLOCK
    from_prev = lax.broadcasted_iota(jnp.int32, (rows, BLOCK), 1) > qi
    low = lax.broadcasted_iota(jnp.int32, (BLOCK, LANES), 1) < HEAD_DIM
    rowg = lax.broadcasted_iota(jnp.int32, (rows, 1), 0) // BLOCK
    sinks = []
    for hk in range(A_KV_HEADS):
        sk = jnp.zeros((rows, 1), F32)
        for g in range(A_GROUP):
            sk = jnp.where(rowg == g, sink_ref[hk * A_GROUP + g], sk)
        sinks.append(sk)
    kvh = range(A_KV_HEADS)
    for jb in range(nb):
        rs = slice(jb * BLOCK, (jb + 1) * BLOCK)
        sc = []
        for hk in kvh:
            tiles = [q_ref[rs, (2 * hk + t) * LANES:(2 * hk + t + 1) * LANES] for t in range(2)]
            zero = jnp.zeros_like(tiles[0])
            q4 = jnp.concatenate([jnp.where(low, tiles[0], zero), jnp.where(low, zero, tiles[0]),
                                  jnp.where(low, tiles[1], zero), jnp.where(low, zero, tiles[1])], axis=0)
            k = _window_rows(kvp_ref, kvc_ref, jb, slice(hk * LANES, (hk + 1) * LANES))
            both = _dot_nt(q4, k)
            prev = both[:, :BLOCK]
            if jb == 0:
                prev = jnp.where(first_tile, NEG_BIG, prev)
            sc.append(jnp.where(from_prev, prev, both[:, BLOCK:]))
        m = [jnp.maximum(jnp.max(s, axis=-1, keepdims=True), sk) for s, sk in zip(sc, sinks)]
        p = [jnp.exp(s - mm) for s, mm in zip(sc, m)]
        den = [jnp.sum(pp, axis=-1, keepdims=True) + jnp.exp(sk - mm) for pp, sk, mm in zip(p, sinks, m)]
        o = []
        for hk in kvh:
            v = _window_rows(kvp_ref, kvc_ref, jb, slice((A_KV_HEADS + hk) * LANES, (A_KV_HEADS + hk + 1) * LANES))
            pb = p[hk].astype(BF16)
            zero = jnp.zeros_like(pb)
            unfolded = jnp.concatenate([jnp.where(from_prev, pb, zero), jnp.where(from_prev, zero, pb)], axis=1)
            o.append(_dot(unfolded, v) * (1.0 / den[hk]))
        for hk in kvh:
            for t in range(2):
                pair = jnp.where(low, o[hk][2 * t * BLOCK:(2 * t + 1) * BLOCK], o[hk][(2 * t + 1) * BLOCK:(2 * t + 2) * BLOCK])
                o_ref[rs, (2 * hk + t) * LANES:(2 * hk + t + 1) * LANES] = pair.astype(BF16)


def _attn_a(qa, kva, sinks, tile=1024):
    bsz, s, _ = qa.shape
    tile = min(tile, s)
    kvw = kva.shape[2]
    per = tile // BLOCK
    return pl.pallas_call(
        _attn_a_kernel,
        grid=(bsz, s // tile),
        in_specs=[pl.BlockSpec(memory_space=pltpu.SMEM),
                  pl.BlockSpec((None, tile, A_Q_W), lambda b, i: (b, i, 0)),
                  pl.BlockSpec((None, BLOCK, kvw), lambda b, i: (b, jnp.maximum(i * per - 1, 0), 0)),
                  pl.BlockSpec((None, tile, kvw), lambda b, i: (b, i, 0))],
        out_specs=pl.BlockSpec((None, tile, A_Q_W), lambda b, i: (b, i, 0)),
        out_shape=jax.ShapeDtypeStruct((bsz, s, A_Q_W), BF16),
        compiler_params=_cparams("parallel", "parallel"),
        name="attn_a",
    )(sinks, qa, kva, kva)


def _attn_b_kernel(group, q_ref, kp_ref, kc_ref, vp_ref, vc_ref, o_ref, l_ref):
    nb = q_ref.shape[0] // BLOCK
    bias_first, bias_inner = _band_bias(BLOCK, BLOCK, pl.program_id(2) == 0)
    low = lax.broadcasted_iota(jnp.int32, (BLOCK, LANES), 1) < HEAD_DIM
    work = [(res, jb) for res in range(q_ref.shape[1] // B_PW) for jb in range(nb)]
    for w0 in range(0, len(work), group):
        items = [(res, jb, h) for res, jb in work[w0:w0 + group] for h in range(B_HEADS)]
        lanes = [slice(res * B_PW + (h // 2) * LANES, res * B_PW + (h // 2 + 1) * LANES) for res, _, h in items]
        sc = []
        for (_, jb, h), ls in zip(items, lanes):
            qt = q_ref[jb * BLOCK:(jb + 1) * BLOCK, ls]
            zero = jnp.zeros_like(qt)
            qh = jnp.where(low, qt, zero) if h % 2 == 0 else jnp.where(low, zero, qt)
            sc.append(_dot_nt(qh, _window_rows(kp_ref, kc_ref, jb, ls)) + (bias_first if jb == 0 else bias_inner))
        m = [jnp.max(s, axis=-1, keepdims=True) for s in sc]
        p = [jnp.exp(s - mm) for s, mm in zip(sc, m)]
        den = [jnp.sum(pp, axis=-1, keepdims=True) for pp in p]
        o = [_dot(pp.astype(BF16), _window_rows(vp_ref, vc_ref, jb, ls)) * (1.0 / dd)
             for pp, dd, (_, jb, _), ls in zip(p, den, items, lanes)]
        lse = [mm + jnp.log(dd) for mm, dd in zip(m, den)]
        for idx in range(0, len(items), 2):
            _, jb, _ = items[idx]
            dst = (slice(jb * BLOCK, (jb + 1) * BLOCK), lanes[idx])
            o_ref[dst] = jnp.where(low, o[idx], o[idx + 1])
            l_ref[dst] = jnp.where(low, lse[idx], lse[idx + 1])


def _attn_b(q, k, v, pat, dil, tile=1024, group=2):
    bsz, length, _ = q.shape
    res = max(1, min(tile // length, dil))
    tile = min(tile, length)
    per = tile // BLOCK
    cur = pl.BlockSpec((None, tile, res * B_PW), lambda b, r, i: (b, i, r))
    prev = pl.BlockSpec((None, BLOCK, res * B_PW), lambda b, r, i: (b, jnp.maximum(i * per - 1, 0), r))
    return pl.pallas_call(
        functools.partial(_attn_b_kernel, group),
        grid=(bsz, dil // res, length // tile),
        in_specs=[cur, prev, cur, prev, cur],
        out_specs=[cur, cur],
        out_shape=[jax.ShapeDtypeStruct((bsz, length, dil * B_PW), F32)] * 2,
        compiler_params=_cparams("parallel", "parallel", "parallel"),
        name=f"attn_b{pat}",
    )(q, k, k, v, v)


def _attn_out_kernel(x_ref, oa_ref, o0_ref, o1_ref, o2_ref, l0_ref, l1_ref, l2_ref, w_ref,
                     gf_ref, wg_ref, wu_ref, wd_ref, out_ref, buf, acc_ref):
    tm = x_ref.shape[0]

    def natural(ref, dil, slot):
        if dil == 1:
            return ref[...]
        n = tm // dil
        halves = B_PW // LANES
        for r in range(dil):
            for j in range(halves):
                buf[slot * halves + j, pl.ds(r, n, stride=dil), :] = ref[:, r * B_PW + j * LANES:r * B_PW + (j + 1) * LANES]
        return jnp.concatenate([buf[slot * halves + j] for j in range(halves)], axis=1)

    dils = [dil for _, dil in B_PATTERNS]
    o = [natural(ref, dil, i) for i, (ref, dil) in enumerate(zip((o0_ref, o1_ref, o2_ref), dils))]
    l = [natural(ref, dil, 3 + i) for i, (ref, dil) in enumerate(zip((l0_ref, l1_ref, l2_ref), dils))]
    m = jnp.maximum(jnp.maximum(l[0], l[1]), l[2])
    e = [jnp.exp(v - m) for v in l]
    ob = (e[0] * o[0] + e[1] * o[1] + e[2] * o[2]) / (e[0] + e[1] + e[2])
    mix = _dot(oa_ref[...], w_ref[:A_Q_W, :]) + _dot(ob.astype(BF16), w_ref[A_Q_W:, :])
    out_ref[...] = _ffn_block(x_ref[...] + mix, gf_ref, wg_ref, wu_ref, wd_ref, acc_ref)


def _attn_out(x, oa, obs, lses, w_out, ffn, tm=512):
    bsz, s, d = x.shape
    tok = lambda w: pl.BlockSpec((None, tm, w), lambda b, i: (b, i, 0))
    pat = [pl.BlockSpec((None, tm // dil, dil * B_PW), lambda b, i: (b, i, 0)) for _, dil in B_PATTERNS]
    return pl.pallas_call(
        _attn_out_kernel,
        grid=(bsz, s // tm),
        in_specs=[tok(d), tok(A_Q_W)] + pat + pat + [_resident(a.shape) for a in (w_out,) + ffn],
        out_specs=tok(d),
        out_shape=jax.ShapeDtypeStruct((bsz, s, d), F32),
        scratch_shapes=[pltpu.VMEM((2 * len(B_PATTERNS) * (B_PW // LANES), tm, LANES), F32),
                        pltpu.VMEM((tm, d), F32)],
        compiler_params=_cparams("parallel", "parallel"),
        name="attn_out_ffn",
    )(x, oa, *obs, *lses, w_out, *ffn)


FF_CHUNK = 256


def _ffn_block(x, g_ref, wg_ref, wu_ref, wd_ref, acc_ref):
    h = _rms(x, g_ref[...]).astype(BF16)
    for c in range(wg_ref.shape[1] // FF_CHUNK):
        sl = slice(c * FF_CHUNK, (c + 1) * FF_CHUNK)
        gate = _dot(h, wg_ref[:, sl])
        up = _dot(h, wu_ref[:, sl])
        contrib = _dot((gate * _sigmoid(gate) * up).astype(BF16), wd_ref[sl, :])
        if c == 0:
            acc_ref[...] = x + contrib
        else:
            acc_ref[...] += contrib
    return acc_ref[...]


def _resident(shape):
    return pl.BlockSpec(shape, lambda *_: (0,) * len(shape), pipeline_mode=pl.Buffered(1))


def _rwkv_in_kernel(has_vlora, *refs):
    (x_ref, gn_ref, mu_ref, wr_ref, wk_ref, wv_ref, w0_ref, w1_ref, w2_ref, a0_ref, a1_ref, a2_ref,
     g1_ref, g2_ref, kk_ref, ka_ref) = refs[:16]
    refs = refs[16:]
    if has_vlora:
        v0_ref, v1_ref, v2_ref, vf_ref = refs[:4]
        refs = refs[4:]
    r_o, lw_o, k_o, v_o, kk_o, ab_o, g_o, carry, mix = refs
    tm, d = x_ref.shape

    @pl.when(pl.program_id(1) == 0)
    def _():
        carry[...] = jnp.zeros(carry.shape, F32)

    h = _rms(x_ref[...], gn_ref[...])
    rolled = pltpu.roll(h, 1, 0)
    first = lax.broadcasted_iota(jnp.int32, (8, 1), 0) == 0
    hprev = jnp.concatenate([jnp.where(first, carry[7:8, :], rolled[0:8]), rolled[8:]], axis=0)
    carry[...] = h[tm - 8:tm, :]
    hb = h.astype(BF16)
    xxb = (hprev - h).astype(BF16)
    mub = mu_ref[...].astype(BF16)
    for i in range(mix.shape[0]):
        mix[i] = hb + xxb * mub[i:i + 1, :]

    mid_w = jnp.tanh(_dot(mix[1], w1_ref[...])).astype(BF16)
    mid_a = _dot(mix[4], a1_ref[...]).astype(BF16)
    mid_g = _sigmoid(_dot(mix[5], g1_ref[...])).astype(BF16)
    if has_vlora:
        mid_v = _dot(mix[3], v1_ref[...]).astype(BF16)

    ones_bd = jnp.where(_head_mask(MXU_W), 1.0, 0.0).astype(BF16)
    for c in range(d // MXU_W):
        cs = slice(c * MXU_W, (c + 1) * MXU_W)
        r_o[:, cs] = _dot(mix[0], wr_ref[:, cs])
        lw_o[:, cs] = -DECAY_SCALE * _sigmoid(w0_ref[:, cs] + _dot(mid_w, w2_ref[:, cs]))
        v = _dot(mix[3], wv_ref[:, cs])
        if has_vlora:
            v = v + (vf_ref[:, cs] - v) * _sigmoid(v0_ref[:, cs] + _dot(mid_v, v2_ref[:, cs]))
        v_o[:, cs] = v
        g_o[:, cs] = _dot(mid_g, g2_ref[:, cs]).astype(BF16)
        k = _dot(mix[2], wk_ref[:, cs])
        a = _sigmoid(a0_ref[:, cs] + _dot(mid_a, a2_ref[:, cs]))
        kk = k * kk_ref[:, cs]
        kk = kk / jnp.maximum(jnp.sqrt(_head_sum(kk * kk, ones_bd)), 1e-12)
        kk_o[:, cs] = kk
        ab_o[:, cs] = kk * a
        k_o[:, cs] = k * (1.0 + (a - 1.0) * ka_ref[:, cs])


def _rwkv_in(x, gn, p, v_lora, v_first, tm=512):
    bsz, s, d = x.shape
    tok = pl.BlockSpec((None, tm, d), lambda b, i: (b, i, 0))
    row = lambda a: a.reshape(1, d)
    args = [x, row(gn), p["mu"], p["w_r"], p["w_k"], p["w_v"], row(p["w0"]), p["w1"], p["w2"], row(p["a0"]), p["a1"], p["a2"],
            p["g1"], p["g2"], row(p["k_k"]), row(p["k_a"])]
    specs = [tok] + [_resident(a.shape) for a in args[1:]]
    if v_lora is not None:
        v0, v1, v2 = v_lora
        extra = [row(v0), v1, v2]
        args += extra + [v_first]
        specs += [_resident(a.shape) for a in extra] + [tok]
    return pl.pallas_call(
        functools.partial(_rwkv_in_kernel, v_lora is not None),
        grid=(bsz, s // tm),
        in_specs=specs,
        out_specs=[tok] * 7,
        out_shape=[jax.ShapeDtypeStruct((bsz, s, d), F32)] * 6 + [jax.ShapeDtypeStruct((bsz, s, d), BF16)],
        scratch_shapes=[pltpu.VMEM((8, d), F32), pltpu.VMEM((p["mu"].shape[0], tm, d), BF16)],
        compiler_params=_cparams("parallel", "arbitrary"),
        name="rwkv_in",
    )(*args)


def _bd2(z):
    lane = lax.broadcasted_iota(jnp.int32, z.shape, 1)
    zero = jnp.zeros(z.shape, z.dtype)
    return jnp.concatenate([jnp.where(lane < HEAD_DIM, z, zero), jnp.where(lane >= HEAD_DIM, z, zero)], axis=0)


def _fold2(full):
    lane = lax.broadcasted_iota(jnp.int32, (HEAD_DIM, LANES), 1)
    return jnp.where(lane < HEAD_DIM, full[:HEAD_DIM], full[HEAD_DIM:])


def _wkv_prep_kernel(r_ref, lw_ref, k_ref, v_ref, kk_ref, ab_ref, rk_ref,
                     q_ref, y0_ref, mc_ref, z_ref, dec_ref, bonus_ref):
    cs = WKV_CHUNK
    rows, d = r_ref.shape
    nc = rows // cs
    ri = lax.broadcasted_iota(jnp.int32, (rows, rows), 0)
    ci = lax.broadcasted_iota(jnp.int32, (rows, rows), 1)
    tri = jnp.where((ri >= ci) & (ri // cs == ci // cs), 1.0, 0.0).astype(BF16)
    lw = lw_ref[...]
    hi = lw.astype(BF16)
    lo = (lw - hi.astype(F32)).astype(BF16)
    cum = _dot(tri, hi) + _dot(tri, lo)
    totals = [cum[(j + 1) * cs - 1:(j + 1) * cs, :] for j in range(nc)]
    tot = jnp.concatenate([jnp.broadcast_to(t, (cs, d)) for t in totals], axis=0)
    e_neg = jnp.exp(-cum)
    e_tot = jnp.exp(tot - cum)
    kk, ab, kx = kk_ref[...], ab_ref[...], k_ref[...]
    at = (-kk * jnp.exp(cum - lw)).astype(BF16)
    rt = r_ref[...] * jnp.exp(cum)
    rtb = rt.astype(BF16)
    bt = (ab * e_neg).astype(BF16)
    kt = (kx * e_neg).astype(BF16)
    bh = (ab * e_tot).astype(BF16)
    kh = (kx * e_tot).astype(BF16)
    vb = v_ref[...].astype(BF16)
    ones_bd = jnp.where(_head_mask(MXU_W), 1.0, 0.0).astype(BF16)
    bonus_ref[...] = _head_sum(r_ref[...] * kx * rk_ref[...], ones_bd, split=False) * v_ref[...]

    trow = lax.broadcasted_iota(jnp.int32, (cs, LANES), 0)
    tcol = lax.broadcasted_iota(jnp.int32, (cs, LANES), 1) % HEAD_DIM
    strict = trow > tcol
    incl = trow >= tcol
    eye = jnp.where(trow == tcol, 1.0, 0.0)

    for j in range(nc):
        dec_ref[8 * j:8 * j + 8, :] = jnp.broadcast_to(jnp.exp(totals[j]), (8, d))

    chains = [(slice(j * cs, (j + 1) * cs), slice(p * LANES, (p + 1) * LANES))
              for j in range(nc) for p in range(d // LANES)]

    g12 = [_dot_nt(jnp.concatenate([at[c], rtb[c]], axis=0),
                   jnp.concatenate([_bd2(bt[c]), _bd2(kt[c])], axis=0)) for c in chains]
    a_ab = [jnp.where(strict, g[:cs, :LANES], 0.0) for g in g12]
    a_rb = [jnp.where(incl, g[cs:, :LANES], 0.0).astype(BF16) for g in g12]
    a_k = [jnp.concatenate([jnp.where(strict, g[:cs, LANES:], 0.0),
                            jnp.where(incl, g[cs:, LANES:], 0.0)], axis=0).astype(BF16) for g in g12]
    g4 = [_dot(a, _bd2(vb[c])) for a, c in zip(a_k, chains)]
    x = [eye + jnp.where((trow - tcol == 1) & (trow % 2 == 1), a, 0.0) for a in a_ab]
    size = 2
    while size < cs:
        lower_left = (trow // size - tcol // size == 1) & ((trow // size) % 2 == 1)
        xb = [v.astype(BF16) for v in x]
        xn = [_dot(v, _bd2(jnp.where(lower_left, a, 0.0).astype(BF16))) for v, a in zip(xb, a_ab)]
        x = [v + _dot(n.astype(BF16), _bd2(vb16)) for v, n, vb16 in zip(x, xn, xb)]
        size *= 2
    wu = [_dot(v.astype(BF16), jnp.concatenate([_bd2(at[c]), _bd2(g[:cs].astype(BF16))], axis=1))
          for v, g, c in zip(x, g4, chains)]
    wb = [v[:, :LANES].astype(BF16) for v in wu]
    ub = [v[:, LANES:].astype(BF16) for v in wu]
    qy = [_dot(a, jnp.concatenate([_bd2(w), _bd2(u)], axis=1)) for a, w, u in zip(a_rb, wb, ub)]
    mc = [_dot_tn(w, bh[c]) for w, c in zip(wb, chains)]
    zz = [_dot_tn(jnp.concatenate([u, vb[c]], axis=0), jnp.concatenate([bh[c], kh[c]], axis=0))
          for u, c in zip(ub, chains)]
    for i, c in enumerate(chains):
        q_ref[c] = (rt[c] + qy[i][:, :LANES]).astype(BF16)
        y0_ref[c] = g4[i][cs:] + qy[i][:, LANES:]
        mc_ref[c] = _fold2(mc[i]).astype(BF16)
        z_ref[c] = _fold2(zz[i])


def _wkv_out_kernel(final_norm, q_ref, y0_ref, mc_ref, z_ref, dec_ref, x_ref, bonus_ref, g_ref, lnw_ref, lnb_ref,
                    wo_ref, gf_ref, wg_ref, wu_ref, wd_ref, gl_ref, out_ref, s_ref, y_buf, acc_ref):
    cs = WKV_CHUNK
    bsz, rows, d = q_ref.shape

    @pl.when(pl.program_id(0) == 0)
    def _():
        s_ref[...] = jnp.zeros(s_ref.shape, F32)

    seqs = [(b, slice(p * LANES, (p + 1) * LANES)) for b in range(bsz) for p in range(d // LANES)]
    state = [s_ref[b, :, ls] for b, ls in seqs]
    for j in range(rows // cs):
        rs = slice(j * cs, (j + 1) * cs)
        sb = [s.astype(BF16) for s in state]
        upd = [_dot(v, _bd2(mc_ref[b, rs, ls])) for v, (b, ls) in zip(sb, seqs)]
        for v, (b, ls) in zip(sb, seqs):
            y_buf[b * rows + j * cs:b * rows + (j + 1) * cs, ls] = _dot_nt(q_ref[b, rs, ls], _bd2(v)) + y0_ref[b, rs, ls]
        state = [s * dec_ref[b, 8 * j:8 * j + 1, ls] + u + z_ref[b, rs, ls]
                 for s, u, (b, ls) in zip(state, upd, seqs)]
    for s, (b, ls) in zip(state, seqs):
        s_ref[b, :, ls] = s

    ones_bd = jnp.where(_head_mask(MXU_W), 1.0, 0.0).astype(BF16)
    y = y_buf[...]
    dev = y - _head_sum(y, ones_bd, split=False) * (1.0 / HEAD_DIM)
    var = _head_sum(dev * dev, ones_bd, split=False) * (1.0 / HEAD_DIM)
    yn = dev * lax.rsqrt(var + LNX_EPS) * lnw_ref[...] + lnb_ref[...]
    gated = jnp.concatenate([((yn[b * rows:(b + 1) * rows] + bonus_ref[b]) * g_ref[b]).astype(BF16)
                             for b in range(bsz)], axis=0)
    x1 = jnp.concatenate([x_ref[b] for b in range(bsz)], axis=0) + _dot(gated, wo_ref[...])
    out = _ffn_block(x1, gf_ref, wg_ref, wu_ref, wd_ref, acc_ref)
    if final_norm:
        out = _rms(out, gl_ref[...])
    for b in range(bsz):
        out_ref[b] = out[b * rows:(b + 1) * rows]


def _wkv(x, r, lw, k, v, kk, ab, g, rk, lnw, lnb, wo, ffn, g_last, final_norm, prep_chunks=4, scan_chunks=2):
    bsz, s, d = r.shape
    cs = WKV_CHUNK
    rows = prep_chunks * cs
    blk = pl.BlockSpec((None, rows, d), lambda b, c: (b, c, 0))
    dec_blk = pl.BlockSpec((None, 8 * prep_chunks, d), lambda b, c: (b, c, 0))
    act = lambda dt: jax.ShapeDtypeStruct((bsz, s, d), dt)
    q, y0, mc, z, dec, bonus = pl.pallas_call(
        _wkv_prep_kernel,
        grid=(bsz, s // rows),
        in_specs=[blk] * 6 + [pl.BlockSpec((1, d), lambda b, c: (0, 0))],
        out_specs=[blk, blk, blk, blk, dec_blk, blk],
        out_shape=[act(BF16), act(F32), act(BF16), act(F32),
                   jax.ShapeDtypeStruct((bsz, 8 * s // cs, d), F32), act(F32)],
        compiler_params=_cparams("parallel", "parallel"),
        name="wkv7_prep",
    )(r, lw, k, v, kk, ab, rk.reshape(1, d))
    rows = scan_chunks * cs
    blk = pl.BlockSpec((bsz, rows, d), lambda c: (0, c, 0))
    dec_blk = pl.BlockSpec((bsz, 8 * scan_chunks, d), lambda c: (0, c, 0))
    row = pl.BlockSpec((1, d), lambda c: (0, 0))
    return pl.pallas_call(
        functools.partial(_wkv_out_kernel, final_norm),
        grid=(s // rows,),
        in_specs=[blk, blk, blk, blk, dec_blk, blk, blk, blk, row, row]
                 + [_resident(a.shape) for a in (wo,) + ffn] + [row],
        out_specs=blk,
        out_shape=act(F32),
        scratch_shapes=[pltpu.VMEM((bsz, HEAD_DIM, d), F32), pltpu.VMEM((bsz * rows, d), F32),
                        pltpu.VMEM((bsz * rows, d), F32)],
        compiler_params=_cparams("arbitrary"),
        name="wkv7_out_ffn",
    )(q, y0, mc, z, dec, x, bonus, g, lnw.reshape(1, d), lnb.reshape(1, d), wo, *ffn, g_last.reshape(1, d))


def kernel(x, positions, norm_mix, norm_ffn, norm_final, attn_w_in, attn_b_in, attn_sinks, attn_w_out, rwkv_mu, rwkv_w_rkv, rwkv_w0, rwkv_w1, rwkv_w2, rwkv_a0, rwkv_a1, rwkv_a2, rwkv_g1, rwkv_g2, rwkv_k_k, rwkv_k_a, rwkv_r_k, rwkv_lnx_w, rwkv_lnx_b, rwkv_w_o, rwkv_v0, rwkv_v1, rwkv_v2, ffn_w_gate, ffn_w_up, ffn_w_down):
    depth = norm_mix.shape[0]
    bf = lambda a: a.astype(BF16)
    tables = _rope_tables(positions)
    v_first = None
    for layer in range(depth):
        i = layer // 2
        ffn = (norm_ffn[layer].reshape(1, -1), bf(ffn_w_gate[layer]), bf(ffn_w_up[layer]), bf(ffn_w_down[layer]))
        if layer % 2 == 0:
            qa, kva, qkv_b = _attn_in(x, norm_mix[layer], bf(attn_w_in[i]), attn_b_in[i], tables)
            oa = _attn_a(qa, kva, attn_sinks[i])
            obs, lses = zip(*[_attn_b(*qkv_b[pat], pat, dil) for pat, (_, dil) in enumerate(B_PATTERNS)])
            x = _attn_out(x, oa, obs, lses, bf(attn_w_out[i]), ffn)
        else:
            p = dict(mu=rwkv_mu[i], w_r=bf(rwkv_w_rkv[i, 0]), w_k=bf(rwkv_w_rkv[i, 1]), w_v=bf(rwkv_w_rkv[i, 2]),
                     w0=rwkv_w0[i], w1=bf(rwkv_w1[i]), w2=bf(rwkv_w2[i]),
                     a0=rwkv_a0[i], a1=bf(rwkv_a1[i]), a2=bf(rwkv_a2[i]), g1=bf(rwkv_g1[i]), g2=bf(rwkv_g2[i]),
                     k_k=rwkv_k_k[i], k_a=rwkv_k_a[i])
            v_lora = None if i == 0 else (rwkv_v0[i - 1], bf(rwkv_v1[i - 1]), bf(rwkv_v2[i - 1]))
            r, lw, k, v, kk, ab, g = _rwkv_in(x, norm_mix[layer], p, v_lora, v_first)
            if i == 0:
                v_first = v
            x = _wkv(x, r, lw, k, v, kk, ab, g, rwkv_r_k[i], rwkv_lnx_w[i], rwkv_lnx_b[i], bf(rwkv_w_o[i]),
                     ffn, norm_final, final_norm=(layer == depth - 1))
    return x
```

```python
import functools

import jax
import jax.numpy as jnp
from jax import lax
from jax.experimental import pallas as pl
from jax.experimental.pallas import tpu as pltpu

F32 = jnp.float32
BF16 = jnp.bfloat16

HEAD_DIM = 64
ROT_DIM = HEAD_DIM // 4
ROT_HALF = ROT_DIM // 2
ROPE_THETA = 500000.0
BLOCK = 128
NORM_EPS = 1e-5
LNX_EPS = 64e-5

A_Q_HEADS = 12
A_KV_HEADS = 3
A_GROUP = A_Q_HEADS // A_KV_HEADS
A_WINDOW = 128
B_PATTERNS = ((128, 1), (512, 4), (2048, 16))
B_HEADS = 4

A_Q_W = A_Q_HEADS * HEAD_DIM
A_KV_W = A_KV_HEADS * HEAD_DIM
B_PW = B_HEADS * HEAD_DIM
B_W = len(B_PATTERNS) * B_PW
KV_DUP_W = 2 * A_KV_W
OFF_KA = A_Q_W
OFF_QB = OFF_KA + 2 * A_KV_W
OFF_KB = OFF_QB + B_W
OFF_VB = OFF_KB + B_W

LANES = 128
MXU_W = 256
WKV_CHUNK = 64
DECAY_SCALE = 0.6065306597126334
NEG_BIG = -1e30
VMEM_LIMIT = 56 * 1024 * 1024


def _cparams(*sem):
    return pltpu.CompilerParams(dimension_semantics=sem, vmem_limit_bytes=VMEM_LIMIT)


def _dot(a, b):
    return jnp.dot(a, b, preferred_element_type=F32)


def _dot_nt(a, b):
    return lax.dot_general(a, b, (((1,), (1,)), ((), ())), preferred_element_type=F32)


def _dot_tn(a, b):
    return lax.dot_general(a, b, (((0,), (0,)), ((), ())), preferred_element_type=F32)


def _row_sum(z):
    part = z[:, :LANES]
    for j in range(1, z.shape[1] // LANES):
        part = part + z[:, j * LANES:(j + 1) * LANES]
    ones = jnp.ones((LANES, LANES), BF16)
    hi = part.astype(BF16)
    lo = (part - hi.astype(F32)).astype(BF16)
    return _dot(hi, ones) + _dot(lo, ones)


def _rms(x, g):
    inv = lax.rsqrt(_row_sum(x * x) * (1.0 / x.shape[1]) + NORM_EPS)
    return x * jnp.concatenate([inv] * (x.shape[1] // LANES), axis=1) * g


def _sigmoid(z):
    return 1.0 / (1.0 + jnp.exp(-z))


def _head_mask(n):
    r = lax.broadcasted_iota(jnp.int32, (n, n), 0) // HEAD_DIM
    c = lax.broadcasted_iota(jnp.int32, (n, n), 1) // HEAD_DIM
    return r == c


def _head_sum(z, ones_bd, split=True):
    outs = []
    for j in range(z.shape[1] // MXU_W):
        zc = z[:, j * MXU_W:(j + 1) * MXU_W]
        hi = zc.astype(BF16)
        acc = _dot(hi, ones_bd)
        if split:
            acc = acc + _dot((zc - hi.astype(F32)).astype(BF16), ones_bd)
        outs.append(acc)
    return jnp.concatenate(outs, axis=1)


def _rope_trig_kernel(pos_ref, invf_ref, cos_ref, sin_ref):
    ang = pos_ref[...].astype(F32)[None] * invf_ref[...]
    cos_ref[...] = jnp.cos(ang)
    sin_ref[...] = jnp.sin(ang)


def _rope_tables(positions):
    bsz, s = positions.shape
    rows = bsz * s // LANES
    inv_freq = jnp.power(ROPE_THETA, -2.0 * jnp.arange(ROT_HALF, dtype=F32) / ROT_DIM)
    invf = jnp.broadcast_to(inv_freq[:, None, None], (ROT_HALF, 1, LANES))
    cos, sin = pl.pallas_call(
        _rope_trig_kernel,
        out_shape=(jax.ShapeDtypeStruct((ROT_HALF, rows, LANES), F32),) * 2,
        name="rope_trig",
    )(positions.reshape(rows, LANES), invf)
    cos = jnp.tile(cos.reshape(ROT_HALF, bsz, s).transpose(1, 2, 0), (1, 1, LANES // ROT_HALF))
    sin = jnp.tile(sin.reshape(ROT_HALF, bsz, s).transpose(1, 2, 0), (1, 1, LANES // ROT_HALF))
    dim = jnp.arange(LANES) % HEAD_DIM
    c = jnp.where(dim < ROT_DIM, cos, 1.0)
    sa = jnp.where(dim < ROT_HALF, -sin, 0.0)
    sb = jnp.where((dim >= ROT_HALF) & (dim < ROT_DIM), sin, 0.0)
    return c, sa, sb


def _residue_major(ref, dil):
    n = ref.shape[0] // dil
    return jnp.concatenate([ref[pl.ds(r, n, stride=dil), :] for r in range(dil)], axis=0)


def _attn_in_kernel(x_ref, g_ref, w_ref, b_ref, c_ref, sa_ref, sb_ref, qa_ref, kva_ref, *rest):
    b_refs, hbuf = rest[:-1], rest[-1]
    hf = _rms(x_ref[...], g_ref[...])
    tm, d = hf.shape
    for j in range(d // LANES):
        hbuf[j] = hf[:, j * LANES:(j + 1) * LANES]
    scale = HEAD_DIM ** -0.5

    def proj(h, lo, width):
        return _dot(h, w_ref[:, lo:lo + width]) + b_ref[:, lo:lo + width]

    def rope(z, tabs):
        c, sa, sb = tabs
        return z * c + pltpu.roll(z, LANES - ROT_HALF, 1) * sa + pltpu.roll(z, ROT_HALF, 1) * sb

    def rope_all(z, tabs):
        return jnp.concatenate([rope(z[:, j * LANES:(j + 1) * LANES], tabs)
                                for j in range(z.shape[1] // LANES)], axis=1)

    h = hf.astype(BF16)
    tabs = (c_ref[...], sa_ref[...], sb_ref[...])
    for j in range(A_Q_W // MXU_W):
        qa_ref[:, j * MXU_W:(j + 1) * MXU_W] = (rope_all(proj(h, j * MXU_W, MXU_W), tabs) * scale).astype(BF16)
    z = proj(h, OFF_KA, 2 * A_KV_W)
    low = lax.broadcasted_iota(jnp.int32, (tm, LANES), 1) < HEAD_DIM
    mid = z[:, LANES:2 * LANES]
    src = [rope(z[:, :LANES], tabs), jnp.where(low, rope(mid, tabs), mid), z[:, 2 * LANES:]]
    for j, t in enumerate(src):
        swapped = pltpu.roll(t, HEAD_DIM, 1)
        kva_ref[:, 2 * j * LANES:(2 * j + 1) * LANES] = jnp.where(low, t, swapped).astype(BF16)
        kva_ref[:, (2 * j + 1) * LANES:(2 * j + 2) * LANES] = jnp.where(low, swapped, t).astype(BF16)
    for pat, (_, dil) in enumerate(B_PATTERNS):
        q_ref, k_ref, v_ref = b_refs[3 * pat:3 * pat + 3]
        if dil > 1:
            h = jnp.concatenate([_residue_major(hbuf.at[j], dil) for j in range(d // LANES)],
                                axis=1).astype(BF16)
            tabs = tuple(_residue_major(t, dil) for t in (c_ref, sa_ref, sb_ref))
        q = (rope_all(proj(h, OFF_QB + pat * B_PW, B_PW), tabs) * scale).astype(BF16)
        k = rope_all(proj(h, OFF_KB + pat * B_PW, B_PW), tabs).astype(BF16)
        v = proj(h, OFF_VB + pat * B_PW, B_PW).astype(BF16)
        n = tm // dil
        for r in range(dil):
            q_ref[:, r * B_PW:(r + 1) * B_PW] = q[r * n:(r + 1) * n]
            k_ref[:, r * B_PW:(r + 1) * B_PW] = k[r * n:(r + 1) * n]
            v_ref[:, r * B_PW:(r + 1) * B_PW] = v[r * n:(r + 1) * n]


def _attn_in(x, g, w_in, b_in, tables, tm=1024):
    bsz, s, d = x.shape
    in_w = w_in.shape[1]
    tok = lambda w: pl.BlockSpec((None, tm, w), lambda b, i: (b, i, 0))
    full = _resident
    out_specs = [tok(A_Q_W), tok(2 * KV_DUP_W)]
    out_shape = [jax.ShapeDtypeStruct((bsz, s, A_Q_W), BF16), jax.ShapeDtypeStruct((bsz, s, 2 * KV_DUP_W), BF16)]
    for _, dil in B_PATTERNS:
        out_specs += [pl.BlockSpec((None, tm // dil, dil * B_PW), lambda b, i: (b, i, 0))] * 3
        out_shape += [jax.ShapeDtypeStruct((bsz, s // dil, dil * B_PW), BF16)] * 3
    outs = pl.pallas_call(
        _attn_in_kernel,
        grid=(bsz, s // tm),
        in_specs=[tok(d), full((1, d)), full((d, in_w)), full((1, in_w)),
                  tok(LANES), tok(LANES), tok(LANES)],
        out_specs=out_specs,
        out_shape=out_shape,
        scratch_shapes=[pltpu.VMEM((d // LANES, tm, LANES), F32)],
        compiler_params=_cparams("parallel", "parallel"),
        name="attn_in",
    )(x, g.reshape(1, d), w_in, b_in.reshape(1, in_w), *tables)
    return outs[0], outs[1], [outs[2 + 3 * p:5 + 3 * p] for p in range(len(B_PATTERNS))]


def _band_bias(rows, max_dist, first_tile):
    qi = lax.broadcasted_iota(jnp.int32, (rows, 2 * BLOCK), 0) % BLOCK
    kj = lax.broadcasted_iota(jnp.int32, (rows, 2 * BLOCK), 1)
    dist = BLOCK + qi - kj
    band = (dist >= 0) & (dist <= max_dist)
    inner = jnp.where(band, 0.0, NEG_BIG)
    return jnp.where(band & ((kj >= BLOCK) | jnp.logical_not(first_tile)), 0.0, NEG_BIG), inner


def _window_rows(prev_ref, cur_ref, jb, lanes):
    if jb == 0:
        return jnp.concatenate([prev_ref[:, lanes], cur_ref[0:BLOCK, lanes]], axis=0)
    return cur_ref[(jb - 1) * BLOCK:(jb + 1) * BLOCK, lanes]


def _attn_a_kernel(sink_ref, q_ref, kvp_ref, kvc_ref, o_ref):
    nb = q_ref.shape[0] // BLOCK
    rows = A_GROUP * BLOCK
    first_tile = pl.program_id(1) == 0
    qi = lax.broadcasted_iota(jnp.int32, (rows, BLOCK), 0) % BLOCK
    from_prev = lax.broadcasted_iota(jnp.int32, (rows, BLOCK), 1) > qi
    low = lax.broadcasted_iota(jnp.int32, (BLOCK, LANES), 1) < HEAD_DIM
    rowg = lax.broadcasted_iota(jnp.int32, (rows, 1), 0) // BLOCK
    sinks = []
    for hk in range(A_KV_HEADS):
        sk = jnp.zeros((rows, 1), F32)
        for g in range(A_GROUP):
            sk = jnp.where(rowg == g, sink_ref[hk * A_GROUP + g], sk)
        sinks.append(sk)
    kvh = range(A_KV_HEADS)
    for jb in range(nb):
        rs = slice(jb * BLOCK, (jb + 1) * BLOCK)
        sc = []
        for hk in kvh:
            tiles = [q_ref[rs, (2 * hk + t) * LANES:(2 * hk + t + 1) * LANES] for t in range(2)]
            zero = jnp.zeros_like(tiles[0])
            q4 = jnp.concatenate([jnp.where(low, tiles[0], zero), jnp.where(low, zero, tiles[0]),
                                  jnp.where(low, tiles[1], zero), jnp.where(low, zero, tiles[1])], axis=0)
            k = _window_rows(kvp_ref, kvc_ref, jb, slice(hk * LANES, (hk + 1) * LANES))
            both = _dot_nt(q4, k)
            prev = both[:, :BLOCK]
            if jb == 0:
                prev = jnp.where(first_tile, NEG_BIG, prev)
            sc.append(jnp.where(from_prev, prev, both[:, BLOCK:]))
        m = [jnp.max(s, axis=-1, keepdims=True) for s in sc]
        p = [jnp.exp(s - mm) for s, mm in zip(sc, m)]
        den = [jnp.sum(pp, axis=-1, keepdims=True) + jnp.exp(sk - mm) for pp, sk, mm in zip(p, sinks, m)]
        o = []
        for hk in kvh:
            v = _window_rows(kvp_ref, kvc_ref, jb, slice((A_KV_HEADS + hk) * LANES, (A_KV_HEADS + hk + 1) * LANES))
            pb = p[hk].astype(BF16)
            zero = jnp.zeros_like(pb)
            unfolded = jnp.concatenate([jnp.where(from_prev, pb, zero), jnp.where(from_prev, zero, pb)], axis=1)
            o.append(_dot(unfolded, v) * (1.0 / den[hk]))
        for hk in kvh:
            for t in range(2):
                pair = jnp.where(low, o[hk][2 * t * BLOCK:(2 * t + 1) * BLOCK], o[hk][(2 * t + 1) * BLOCK:(2 * t + 2) * BLOCK])
                o_ref[rs, (2 * hk + t) * LANES:(2 * hk + t + 1) * LANES] = pair.astype(BF16)


def _attn_a(qa, kva, sinks, tile=1024):
    assert A_WINDOW == BLOCK, "the score fold needs a window of exactly one block"
    bsz, s, _ = qa.shape
    tile = min(tile, s)
    kvw = kva.shape[2]
    per = tile // BLOCK
    return pl.pallas_call(
        _attn_a_kernel,
        grid=(bsz, s // tile),
        in_specs=[pl.BlockSpec(memory_space=pltpu.SMEM),
                  pl.BlockSpec((None, tile, A_Q_W), lambda b, i: (b, i, 0)),
                  pl.BlockSpec((None, BLOCK, kvw), lambda b, i: (b, jnp.maximum(i * per - 1, 0), 0)),
                  pl.BlockSpec((None, tile, kvw), lambda b, i: (b, i, 0))],
        out_specs=pl.BlockSpec((None, tile, A_Q_W), lambda b, i: (b, i, 0)),
        out_shape=jax.ShapeDtypeStruct((bsz, s, A_Q_W), BF16),
        compiler_params=_cparams("parallel", "parallel"),
        name="attn_a",
    )(sinks, qa, kva, kva)


def _attn_b_kernel(group, q_ref, kp_ref, kc_ref, vp_ref, vc_ref, o_ref, l_ref):
    nb = q_ref.shape[0] // BLOCK
    bias_first, bias_inner = _band_bias(BLOCK, BLOCK, pl.program_id(2) == 0)
    low = lax.broadcasted_iota(jnp.int32, (BLOCK, LANES), 1) < HEAD_DIM
    work = [(res, jb) for res in range(q_ref.shape[1] // B_PW) for jb in range(nb)]
    for w0 in range(0, len(work), group):
        items = [(res, jb, h) for res, jb in work[w0:w0 + group] for h in range(B_HEADS)]
        lanes = [slice(res * B_PW + (h // 2) * LANES, res * B_PW + (h // 2 + 1) * LANES) for res, _, h in items]
        sc = []
        for (_, jb, h), ls in zip(items, lanes):
            qt = q_ref[jb * BLOCK:(jb + 1) * BLOCK, ls]
            zero = jnp.zeros_like(qt)
            qh = jnp.where(low, qt, zero) if h % 2 == 0 else jnp.where(low, zero, qt)
            sc.append(_dot_nt(qh, _window_rows(kp_ref, kc_ref, jb, ls)) + (bias_first if jb == 0 else bias_inner))
        m = [jnp.max(s, axis=-1, keepdims=True) for s in sc]
        p = [jnp.exp(s - mm) for s, mm in zip(sc, m)]
        den = [jnp.sum(pp, axis=-1, keepdims=True) for pp in p]
        o = [_dot(pp.astype(BF16), _window_rows(vp_ref, vc_ref, jb, ls)) * (1.0 / dd)
             for pp, dd, (_, jb, _), ls in zip(p, den, items, lanes)]
        lse = [mm + jnp.log(dd) for mm, dd in zip(m, den)]
        for idx in range(0, len(items), 2):
            _, jb, _ = items[idx]
            dst = (slice(jb * BLOCK, (jb + 1) * BLOCK), lanes[idx])
            o_ref[dst] = jnp.where(low, o[idx], o[idx + 1])
            l_ref[dst] = jnp.where(low, lse[idx], lse[idx + 1])


def _attn_b(q, k, v, pat, dil, tile=1024, group=2):
    bsz, length, _ = q.shape
    res = max(1, min(tile // length, dil))
    tile = min(tile, length)
    per = tile // BLOCK
    cur = pl.BlockSpec((None, tile, res * B_PW), lambda b, r, i: (b, i, r))
    prev = pl.BlockSpec((None, BLOCK, res * B_PW), lambda b, r, i: (b, jnp.maximum(i * per - 1, 0), r))
    return pl.pallas_call(
        functools.partial(_attn_b_kernel, group),
        grid=(bsz, dil // res, length // tile),
        in_specs=[cur, prev, cur, prev, cur],
        out_specs=[cur, cur],
        out_shape=[jax.ShapeDtypeStruct((bsz, length, dil * B_PW), F32)] * 2,
        compiler_params=_cparams("parallel", "parallel", "parallel"),
        name=f"attn_b{pat}",
    )(q, k, k, v, v)


def _attn_out_kernel(x_ref, oa_ref, o0_ref, o1_ref, o2_ref, l0_ref, l1_ref, l2_ref, w_ref,
                     gf_ref, wg_ref, wu_ref, wd_ref, out_ref, buf, acc_ref):
    tm = x_ref.shape[0]

    def natural(ref, dil, slot):
        if dil == 1:
            return ref[...]
        n = tm // dil
        halves = B_PW // LANES
        for r in range(dil):
            for j in range(halves):
                buf[slot * halves + j, pl.ds(r, n, stride=dil), :] = ref[:, r * B_PW + j * LANES:r * B_PW + (j + 1) * LANES]
        return jnp.concatenate([buf[slot * halves + j] for j in range(halves)], axis=1)

    dils = [dil for _, dil in B_PATTERNS]
    o = [natural(ref, dil, i) for i, (ref, dil) in enumerate(zip((o0_ref, o1_ref, o2_ref), dils))]
    l = [natural(ref, dil, 3 + i) for i, (ref, dil) in enumerate(zip((l0_ref, l1_ref, l2_ref), dils))]
    m = jnp.maximum(jnp.maximum(l[0], l[1]), l[2])
    e = [jnp.exp(v - m) for v in l]
    ob = (e[0] * o[0] + e[1] * o[1] + e[2] * o[2]) / (e[0] + e[1] + e[2])
    mix = _dot(oa_ref[...], w_ref[:A_Q_W, :]) + _dot(ob.astype(BF16), w_ref[A_Q_W:, :])
    out_ref[...] = _ffn_block(x_ref[...] + mix, gf_ref, wg_ref, wu_ref, wd_ref, acc_ref)


def _attn_out(x, oa, obs, lses, w_out, ffn, tm=512):
    bsz, s, d = x.shape
    tok = lambda w: pl.BlockSpec((None, tm, w), lambda b, i: (b, i, 0))
    pat = [pl.BlockSpec((None, tm // dil, dil * B_PW), lambda b, i: (b, i, 0)) for _, dil in B_PATTERNS]
    return pl.pallas_call(
        _attn_out_kernel,
        grid=(bsz, s // tm),
        in_specs=[tok(d), tok(A_Q_W)] + pat + pat + [_resident(a.shape) for a in (w_out,) + ffn],
        out_specs=tok(d),
        out_shape=jax.ShapeDtypeStruct((bsz, s, d), F32),
        scratch_shapes=[pltpu.VMEM((2 * len(B_PATTERNS) * (B_PW // LANES), tm, LANES), F32),
                        pltpu.VMEM((tm, d), F32)],
        compiler_params=_cparams("parallel", "parallel"),
        name="attn_out_ffn",
    )(x, oa, *obs, *lses, w_out, *ffn)


FF_CHUNK = 256


def _ffn_block(x, g_ref, wg_ref, wu_ref, wd_ref, acc_ref):
    h = _rms(x, g_ref[...]).astype(BF16)
    for c in range(wg_ref.shape[1] // FF_CHUNK):
        sl = slice(c * FF_CHUNK, (c + 1) * FF_CHUNK)
        gate = _dot(h, wg_ref[:, sl])
        up = _dot(h, wu_ref[:, sl])
        contrib = _dot((gate * _sigmoid(gate) * up).astype(BF16), wd_ref[sl, :])
        if c == 0:
            acc_ref[...] = x + contrib
        else:
            acc_ref[...] += contrib
    return acc_ref[...]


def _resident(shape):
    return pl.BlockSpec(shape, lambda *_: (0,) * len(shape), pipeline_mode=pl.Buffered(1))


def _rwkv_in_kernel(has_vlora, *refs):
    (x_ref, gn_ref, mu_ref, wr_ref, wk_ref, wv_ref, w0_ref, w1_ref, w2_ref, a0_ref, a1_ref, a2_ref,
     g1_ref, g2_ref, kk_ref, ka_ref) = refs[:16]
    refs = refs[16:]
    if has_vlora:
        v0_ref, v1_ref, v2_ref, vf_ref = refs[:4]
        refs = refs[4:]
    r_o, lw_o, k_o, v_o, kk_o, ab_o, g_o, carry, mix = refs
    tm, d = x_ref.shape

    @pl.when(pl.program_id(1) == 0)
    def _():
        carry[...] = jnp.zeros(carry.shape, F32)

    h = _rms(x_ref[...], gn_ref[...])
    rolled = pltpu.roll(h, 1, 0)
    first = lax.broadcasted_iota(jnp.int32, (8, 1), 0) == 0
    hprev = jnp.concatenate([jnp.where(first, carry[7:8, :], rolled[0:8]), rolled[8:]], axis=0)
    carry[...] = h[tm - 8:tm, :]
    hb = h.astype(BF16)
    xxb = (hprev - h).astype(BF16)
    mub = mu_ref[...].astype(BF16)
    for i in range(mix.shape[0]):
        mix[i] = hb + xxb * mub[i:i + 1, :]

    mid_w = jnp.tanh(_dot(mix[1], w1_ref[...])).astype(BF16)
    mid_a = _dot(mix[4], a1_ref[...]).astype(BF16)
    mid_g = _sigmoid(_dot(mix[5], g1_ref[...])).astype(BF16)
    if has_vlora:
        mid_v = _dot(mix[3], v1_ref[...]).astype(BF16)

    ones_bd = jnp.where(_head_mask(MXU_W), 1.0, 0.0).astype(BF16)
    for c in range(d // MXU_W):
        cs = slice(c * MXU_W, (c + 1) * MXU_W)
        r_o[:, cs] = _dot(mix[0], wr_ref[:, cs])
        lw_o[:, cs] = -DECAY_SCALE * _sigmoid(w0_ref[:, cs] + _dot(mid_w, w2_ref[:, cs]))
        v = _dot(mix[3], wv_ref[:, cs])
        if has_vlora:
            v = v + (vf_ref[:, cs] - v) * _sigmoid(v0_ref[:, cs] + _dot(mid_v, v2_ref[:, cs]))
        v_o[:, cs] = v
        g_o[:, cs] = _dot(mid_g, g2_ref[:, cs]).astype(BF16)
        k = _dot(mix[2], wk_ref[:, cs])
        a = _sigmoid(a0_ref[:, cs] + _dot(mid_a, a2_ref[:, cs]))
        kk = k * kk_ref[:, cs]
        kk = kk / jnp.maximum(jnp.sqrt(_head_sum(kk * kk, ones_bd)), 1e-12)
        kk_o[:, cs] = kk
        ab_o[:, cs] = kk * a
        k_o[:, cs] = k * (1.0 + (a - 1.0) * ka_ref[:, cs])


def _rwkv_in(x, gn, p, v_lora, v_first, tm=512):
    bsz, s, d = x.shape
    tok = pl.BlockSpec((None, tm, d), lambda b, i: (b, i, 0))
    row = lambda a: a.reshape(1, d)
    args = [x, row(gn), p["mu"], p["w_r"], p["w_k"], p["w_v"], row(p["w0"]), p["w1"], p["w2"], row(p["a0"]), p["a1"], p["a2"],
            p["g1"], p["g2"], row(p["k_k"]), row(p["k_a"])]
    specs = [tok] + [_resident(a.shape) for a in args[1:]]
    if v_lora is not None:
        v0, v1, v2 = v_lora
        extra = [row(v0), v1, v2]
        args += extra + [v_first]
        specs += [_resident(a.shape) for a in extra] + [tok]
    return pl.pallas_call(
        functools.partial(_rwkv_in_kernel, v_lora is not None),
        grid=(bsz, s // tm),
        in_specs=specs,
        out_specs=[tok] * 7,
        out_shape=[jax.ShapeDtypeStruct((bsz, s, d), F32)] * 6 + [jax.ShapeDtypeStruct((bsz, s, d), BF16)],
        scratch_shapes=[pltpu.VMEM((8, d), F32), pltpu.VMEM((p["mu"].shape[0], tm, d), BF16)],
        compiler_params=_cparams("parallel", "arbitrary"),
        name="rwkv_in",
    )(*args)


def _bd2(z):
    lane = lax.broadcasted_iota(jnp.int32, z.shape, 1)
    zero = jnp.zeros(z.shape, z.dtype)
    return jnp.concatenate([jnp.where(lane < HEAD_DIM, z, zero), jnp.where(lane >= HEAD_DIM, z, zero)], axis=0)


def _fold2(full):
    lane = lax.broadcasted_iota(jnp.int32, (HEAD_DIM, LANES), 1)
    return jnp.where(lane < HEAD_DIM, full[:HEAD_DIM], full[HEAD_DIM:])


def _wkv_prep_kernel(r_ref, lw_ref, k_ref, v_ref, kk_ref, ab_ref, rk_ref,
                     q_ref, y0_ref, mc_ref, z_ref, dec_ref, bonus_ref):
    cs = WKV_CHUNK
    rows, d = r_ref.shape
    nc = rows // cs
    ri = lax.broadcasted_iota(jnp.int32, (rows, rows), 0)
    ci = lax.broadcasted_iota(jnp.int32, (rows, rows), 1)
    tri = jnp.where((ri >= ci) & (ri // cs == ci // cs), 1.0, 0.0).astype(BF16)
    lw = lw_ref[...]
    hi = lw.astype(BF16)
    lo = (lw - hi.astype(F32)).astype(BF16)
    cum = _dot(tri, hi) + _dot(tri, lo)
    totals = [cum[(j + 1) * cs - 1:(j + 1) * cs, :] for j in range(nc)]
    tot = jnp.concatenate([jnp.broadcast_to(t, (cs, d)) for t in totals], axis=0)
    e_neg = jnp.exp(-cum)
    e_tot = jnp.exp(tot - cum)
    kk, ab, kx = kk_ref[...], ab_ref[...], k_ref[...]
    at = (-kk * jnp.exp(cum - lw)).astype(BF16)
    rt = r_ref[...] * jnp.exp(cum)
    rtb = rt.astype(BF16)
    bt = (ab * e_neg).astype(BF16)
    kt = (kx * e_neg).astype(BF16)
    bh = (ab * e_tot).astype(BF16)
    kh = (kx * e_tot).astype(BF16)
    vb = v_ref[...].astype(BF16)
    ones_bd = jnp.where(_head_mask(MXU_W), 1.0, 0.0).astype(BF16)
    bonus_ref[...] = _head_sum(r_ref[...] * kx * rk_ref[...], ones_bd, split=False) * v_ref[...]

    trow = lax.broadcasted_iota(jnp.int32, (cs, LANES), 0)
    tcol = lax.broadcasted_iota(jnp.int32, (cs, LANES), 1) % HEAD_DIM
    strict = trow > tcol
    incl = trow >= tcol
    eye = jnp.where(trow == tcol, 1.0, 0.0)

    for j in range(nc):
        dec_ref[8 * j:8 * j + 8, :] = jnp.broadcast_to(jnp.exp(totals[j]), (8, d))

    chains = [(slice(j * cs, (j + 1) * cs), slice(p * LANES, (p + 1) * LANES))
              for j in range(nc) for p in range(d // LANES)]

    g12 = [_dot_nt(jnp.concatenate([at[c], rtb[c]], axis=0),
                   jnp.concatenate([_bd2(bt[c]), _bd2(kt[c])], axis=0)) for c in chains]
    a_ab = [jnp.where(strict, g[:cs, :LANES], 0.0) for g in g12]
    a_rb = [jnp.where(incl, g[cs:, :LANES], 0.0).astype(BF16) for g in g12]
    a_k = [jnp.concatenate([jnp.where(strict, g[:cs, LANES:], 0.0),
                            jnp.where(incl, g[cs:, LANES:], 0.0)], axis=0).astype(BF16) for g in g12]
    g4 = [_dot(a, _bd2(vb[c])) for a, c in zip(a_k, chains)]
    x = [eye + jnp.where((trow - tcol == 1) & (trow % 2 == 1), a, 0.0) for a in a_ab]
    size = 2
    while size < cs:
        lower_left = (trow // size - tcol // size == 1) & ((trow // size) % 2 == 1)
        xb = [v.astype(BF16) for v in x]
        xn = [_dot(v, _bd2(jnp.where(lower_left, a, 0.0).astype(BF16))) for v, a in zip(xb, a_ab)]
        x = [v + _dot(n.astype(BF16), _bd2(vb16)) for v, n, vb16 in zip(x, xn, xb)]
        size *= 2
    wu = [_dot(v.astype(BF16), jnp.concatenate([_bd2(at[c]), _bd2(g[:cs].astype(BF16))], axis=1))
          for v, g, c in zip(x, g4, chains)]
    wb = [v[:, :LANES].astype(BF16) for v in wu]
    ub = [v[:, LANES:].astype(BF16) for v in wu]
    qy = [_dot(a, jnp.concatenate([_bd2(w), _bd2(u)], axis=1)) for a, w, u in zip(a_rb, wb, ub)]
    mc = [_dot_tn(w, bh[c]) for w, c in zip(wb, chains)]
    zz = [_dot_tn(jnp.concatenate([u, vb[c]], axis=0), jnp.concatenate([bh[c], kh[c]], axis=0))
          for u, c in zip(ub, chains)]
    for i, c in enumerate(chains):
        q_ref[c] = (rt[c] + qy[i][:, :LANES]).astype(BF16)
        y0_ref[c] = g4[i][cs:] + qy[i][:, LANES:]
        mc_ref[c] = _fold2(mc[i]).astype(BF16)
        z_ref[c] = _fold2(zz[i])


def _wkv_out_kernel(final_norm, q_ref, y0_ref, mc_ref, z_ref, dec_ref, x_ref, bonus_ref, g_ref, lnw_ref, lnb_ref,
                    wo_ref, gf_ref, wg_ref, wu_ref, wd_ref, gl_ref, out_ref, s_ref, y_buf, acc_ref):
    cs = WKV_CHUNK
    bsz, rows, d = q_ref.shape

    @pl.when(pl.program_id(0) == 0)
    def _():
        s_ref[...] = jnp.zeros(s_ref.shape, F32)

    seqs = [(b, slice(p * LANES, (p + 1) * LANES)) for b in range(bsz) for p in range(d // LANES)]
    state = [s_ref[b, :, ls] for b, ls in seqs]
    for j in range(rows // cs):
        rs = slice(j * cs, (j + 1) * cs)
        sb = [s.astype(BF16) for s in state]
        upd = [_dot(v, _bd2(mc_ref[b, rs, ls])) for v, (b, ls) in zip(sb, seqs)]
        for v, (b, ls) in zip(sb, seqs):
            y_buf[b * rows + j * cs:b * rows + (j + 1) * cs, ls] = _dot_nt(q_ref[b, rs, ls], _bd2(v)) + y0_ref[b, rs, ls]
        state = [s * dec_ref[b, 8 * j:8 * j + 1, ls] + u + z_ref[b, rs, ls]
                 for s, u, (b, ls) in zip(state, upd, seqs)]
    for s, (b, ls) in zip(state, seqs):
        s_ref[b, :, ls] = s

    ones_bd = jnp.where(_head_mask(MXU_W), 1.0, 0.0).astype(BF16)
    y = y_buf[...]
    dev = y - _head_sum(y, ones_bd, split=False) * (1.0 / HEAD_DIM)
    var = _head_sum(dev * dev, ones_bd, split=False) * (1.0 / HEAD_DIM)
    yn = dev * lax.rsqrt(var + LNX_EPS) * lnw_ref[...] + lnb_ref[...]
    gated = jnp.concatenate([((yn[b * rows:(b + 1) * rows] + bonus_ref[b]) * g_ref[b]).astype(BF16)
                             for b in range(bsz)], axis=0)
    x1 = jnp.concatenate([x_ref[b] for b in range(bsz)], axis=0) + _dot(gated, wo_ref[...])
    out = _ffn_block(x1, gf_ref, wg_ref, wu_ref, wd_ref, acc_ref)
    if final_norm:
        out = _rms(out, gl_ref[...])
    for b in range(bsz):
        out_ref[b] = out[b * rows:(b + 1) * rows]


def _wkv(x, r, lw, k, v, kk, ab, g, rk, lnw, lnb, wo, ffn, g_last, final_norm, prep_chunks=4, scan_chunks=2):
    bsz, s, d = r.shape
    cs = WKV_CHUNK
    rows = prep_chunks * cs
    blk = pl.BlockSpec((None, rows, d), lambda b, c: (b, c, 0))
    dec_blk = pl.BlockSpec((None, 8 * prep_chunks, d), lambda b, c: (b, c, 0))
    act = lambda dt: jax.ShapeDtypeStruct((bsz, s, d), dt)
    q, y0, mc, z, dec, bonus = pl.pallas_call(
        _wkv_prep_kernel,
        grid=(bsz, s // rows),
        in_specs=[blk] * 6 + [pl.BlockSpec((1, d), lambda b, c: (0, 0))],
        out_specs=[blk, blk, blk, blk, dec_blk, blk],
        out_shape=[act(BF16), act(F32), act(BF16), act(F32),
                   jax.ShapeDtypeStruct((bsz, 8 * s // cs, d), F32), act(F32)],
        compiler_params=_cparams("parallel", "parallel"),
        name="wkv7_prep",
    )(r, lw, k, v, kk, ab, rk.reshape(1, d))
    rows = scan_chunks * cs
    blk = pl.BlockSpec((bsz, rows, d), lambda c: (0, c, 0))
    dec_blk = pl.BlockSpec((bsz, 8 * scan_chunks, d), lambda c: (0, c, 0))
    row = pl.BlockSpec((1, d), lambda c: (0, 0))
    return pl.pallas_call(
        functools.partial(_wkv_out_kernel, final_norm),
        grid=(s // rows,),
        in_specs=[blk, blk, blk, blk, dec_blk, blk, blk, blk, row, row]
                 + [_resident(a.shape) for a in (wo,) + ffn] + [row],
        out_specs=blk,
        out_shape=act(F32),
        scratch_shapes=[pltpu.VMEM((bsz, HEAD_DIM, d), F32), pltpu.VMEM((bsz * rows, d), F32),
                        pltpu.VMEM((bsz * rows, d), F32)],
        compiler_params=_cparams("arbitrary"),
        name="wkv7_out_ffn",
    )(q, y0, mc, z, dec, x, bonus, g, lnw.reshape(1, d), lnb.reshape(1, d), wo, *ffn, g_last.reshape(1, d))


def kernel(x, positions, norm_mix, norm_ffn, norm_final, attn_w_in, attn_b_in, attn_sinks, attn_w_out, rwkv_mu, rwkv_w_rkv, rwkv_w0, rwkv_w1, rwkv_w2, rwkv_a0, rwkv_a1, rwkv_a2, rwkv_g1, rwkv_g2, rwkv_k_k, rwkv_k_a, rwkv_r_k, rwkv_lnx_w, rwkv_lnx_b, rwkv_w_o, rwkv_v0, rwkv_v1, rwkv_v2, ffn_w_gate, ffn_w_up, ffn_w_down):
    depth = norm_mix.shape[0]
    assert depth % 2 == 0, "the final norm is fused into the last RWKV layer"
    assert all(win // dil == BLOCK for win, dil in B_PATTERNS), "mixer B bands must span one block"
    bf = lambda a: a.astype(BF16)
    tables = _rope_tables(positions)
    v_first = None
    for layer in range(depth):
        i = layer // 2
        ffn = (norm_ffn[layer].reshape(1, -1), bf(ffn_w_gate[layer]), bf(ffn_w_up[layer]), bf(ffn_w_down[layer]))
        if layer % 2 == 0:
            qa, kva, qkv_b = _attn_in(x, norm_mix[layer], bf(attn_w_in[i]), attn_b_in[i], tables)
            oa = _attn_a(qa, kva, attn_sinks[i])
            obs, lses = zip(*[_attn_b(*qkv_b[pat], pat, dil) for pat, (_, dil) in enumerate(B_PATTERNS)])
            x = _attn_out(x, oa, obs, lses, bf(attn_w_out[i]), ffn)
        else:
            p = dict(mu=rwkv_mu[i], w_r=bf(rwkv_w_rkv[i, 0]), w_k=bf(rwkv_w_rkv[i, 1]), w_v=bf(rwkv_w_rkv[i, 2]),
                     w0=rwkv_w0[i], w1=bf(rwkv_w1[i]), w2=bf(rwkv_w2[i]),
                     a0=rwkv_a0[i], a1=bf(rwkv_a1[i]), a2=bf(rwkv_a2[i]), g1=bf(rwkv_g1[i]), g2=bf(rwkv_g2[i]),
                     k_k=rwkv_k_k[i], k_a=rwkv_k_a[i])
            v_lora = None if i == 0 else (rwkv_v0[i - 1], bf(rwkv_v1[i - 1]), bf(rwkv_v2[i - 1]))
            r, lw, k, v, kk, ab, g = _rwkv_in(x, norm_mix[layer], p, v_lora, v_first)
            if i == 0:
                v_first = v
            x = _wkv(x, r, lw, k, v, kk, ab, g, rwkv_r_k[i], rwkv_lnx_w[i], rwkv_lnx_b[i], bf(rwkv_w_o[i]),
                     ffn, norm_final, final_norm=(layer == depth - 1))
    return x
```

```python
import functools

import jax
import jax.numpy as jnp
from jax import lax
from jax.experimental import pallas as pl
from jax.experimental.pallas import tpu as pltpu

F32 = jnp.float32
BF16 = jnp.bfloat16

HEAD_DIM = 64
ROT_DIM = HEAD_DIM // 4
ROT_HALF = ROT_DIM // 2
ROPE_THETA = 500000.0
BLOCK = 128
NORM_EPS = 1e-5
LNX_EPS = 64e-5

A_Q_HEADS = 12
A_KV_HEADS = 3
A_GROUP = A_Q_HEADS // A_KV_HEADS
A_WINDOW = 128
B_PATTERNS = ((128, 1), (512, 4), (2048, 16))
B_HEADS = 4

A_Q_W = A_Q_HEADS * HEAD_DIM
A_KV_W = A_KV_HEADS * HEAD_DIM
B_PW = B_HEADS * HEAD_DIM
B_W = len(B_PATTERNS) * B_PW
KV_DUP_W = 2 * A_KV_W
OFF_KA = A_Q_W
OFF_QB = OFF_KA + 2 * A_KV_W
OFF_KB = OFF_QB + B_W
OFF_VB = OFF_KB + B_W

LANES = 128
MXU_W = 256
WKV_CHUNK = 64
DECAY_SCALE = 0.6065306597126334
NEG_BIG = -1e30
VMEM_LIMIT = 56 * 1024 * 1024


def _cparams(*sem):
    return pltpu.CompilerParams(dimension_semantics=sem, vmem_limit_bytes=VMEM_LIMIT)


def _dot(a, b):
    return jnp.dot(a, b, preferred_element_type=F32)


def _dot_nt(a, b):
    return lax.dot_general(a, b, (((1,), (1,)), ((), ())), preferred_element_type=F32)


def _dot_tn(a, b):
    return lax.dot_general(a, b, (((0,), (0,)), ((), ())), preferred_element_type=F32)


def _row_sum(z):
    part = z[:, :LANES]
    for j in range(1, z.shape[1] // LANES):
        part = part + z[:, j * LANES:(j + 1) * LANES]
    ones = jnp.ones((LANES, LANES), BF16)
    hi = part.astype(BF16)
    lo = (part - hi.astype(F32)).astype(BF16)
    return _dot(hi, ones) + _dot(lo, ones)


def _rms(x, g):
    inv = lax.rsqrt(_row_sum(x * x) * (1.0 / x.shape[1]) + NORM_EPS)
    return x * jnp.concatenate([inv] * (x.shape[1] // LANES), axis=1) * g


def _sigmoid(z):
    return 1.0 / (1.0 + jnp.exp(-z))


def _head_mask(n):
    r = lax.broadcasted_iota(jnp.int32, (n, n), 0) // HEAD_DIM
    c = lax.broadcasted_iota(jnp.int32, (n, n), 1) // HEAD_DIM
    return r == c


def _head_sum(z, ones_bd, split=True):
    outs = []
    for j in range(z.shape[1] // MXU_W):
        zc = z[:, j * MXU_W:(j + 1) * MXU_W]
        hi = zc.astype(BF16)
        acc = _dot(hi, ones_bd)
        if split:
            acc = acc + _dot((zc - hi.astype(F32)).astype(BF16), ones_bd)
        outs.append(acc)
    return jnp.concatenate(outs, axis=1)


def _rope_trig_kernel(pos_ref, invf_ref, cos_ref, sin_ref):
    ang = pos_ref[...].astype(F32)[None] * invf_ref[...]
    cos_ref[...] = jnp.cos(ang)
    sin_ref[...] = jnp.sin(ang)


def _rope_tables(positions):
    bsz, s = positions.shape
    rows = bsz * s // LANES
    inv_freq = jnp.power(ROPE_THETA, -2.0 * jnp.arange(ROT_HALF, dtype=F32) / ROT_DIM)
    invf = jnp.broadcast_to(inv_freq[:, None, None], (ROT_HALF, 1, LANES))
    cos, sin = pl.pallas_call(
        _rope_trig_kernel,
        out_shape=(jax.ShapeDtypeStruct((ROT_HALF, rows, LANES), F32),) * 2,
        name="rope_trig",
    )(positions.reshape(rows, LANES), invf)
    cos = jnp.tile(cos.reshape(ROT_HALF, bsz, s).transpose(1, 2, 0), (1, 1, LANES // ROT_HALF))
    sin = jnp.tile(sin.reshape(ROT_HALF, bsz, s).transpose(1, 2, 0), (1, 1, LANES // ROT_HALF))
    dim = jnp.arange(LANES) % HEAD_DIM
    c = jnp.where(dim < ROT_DIM, cos, 1.0)
    sa = jnp.where(dim < ROT_HALF, -sin, 0.0)
    sb = jnp.where((dim >= ROT_HALF) & (dim < ROT_DIM), sin, 0.0)
    return c, sa, sb


def _residue_major(ref, dil, groups=1):
    span = ref.shape[0] // groups
    n = span // dil
    return jnp.concatenate([ref[pl.ds(g * span + r, n, stride=dil), :]
                            for r in range(dil) for g in range(groups)], axis=0)


def _attn_in_kernel(x_ref, g_ref, w_ref, b_ref, c_ref, sa_ref, sb_ref, qa_ref, kva_ref, *rest):
    b_refs, hbuf = rest[:-1], rest[-1]
    hf = _rms(x_ref[...], g_ref[...])
    tm, d = hf.shape
    tiles = d // LANES
    for j in range(tiles):
        hbuf[j] = hf[:, j * LANES:(j + 1) * LANES]
    scale = HEAD_DIM ** -0.5

    def proj(h, lo, width):
        return _dot(h, w_ref[:, lo:lo + width]) + b_ref[:, lo:lo + width]

    def rope(z, tabs):
        c, sa, sb = tabs
        return z * c + pltpu.roll(z, LANES - ROT_HALF, 1) * sa + pltpu.roll(z, ROT_HALF, 1) * sb

    def rope_all(z, tabs):
        return jnp.concatenate([rope(z[:, j * LANES:(j + 1) * LANES], tabs)
                                for j in range(z.shape[1] // LANES)], axis=1)

    h = hf.astype(BF16)
    tabs = (c_ref[...], sa_ref[...], sb_ref[...])
    for j in range(A_Q_W // MXU_W):
        qa_ref[:, j * MXU_W:(j + 1) * MXU_W] = (rope_all(proj(h, j * MXU_W, MXU_W), tabs) * scale).astype(BF16)
    z = proj(h, OFF_KA, 2 * A_KV_W)
    low = lax.broadcasted_iota(jnp.int32, (tm, LANES), 1) < HEAD_DIM
    mid = z[:, LANES:2 * LANES]
    src = [rope(z[:, :LANES], tabs), jnp.where(low, rope(mid, tabs), mid), z[:, 2 * LANES:]]
    for j, t in enumerate(src):
        swapped = pltpu.roll(t, HEAD_DIM, 1)
        kva_ref[:, 2 * j * LANES:(2 * j + 1) * LANES] = jnp.where(low, t, swapped).astype(BF16)
        kva_ref[:, (2 * j + 1) * LANES:(2 * j + 2) * LANES] = jnp.where(low, swapped, t).astype(BF16)
    prev_dil = 1
    for pat, (_, dil) in enumerate(B_PATTERNS):
        q_ref, k_ref, v_ref = b_refs[3 * pat:3 * pat + 3]
        if dil > 1:
            step = dil // prev_dil
            srcs = [hbuf.at[j] for j in range(tiles)] + [hbuf.at[tiles + j] for j in range(3)]
            if prev_dil == 1:
                srcs[tiles:] = [c_ref, sa_ref, sb_ref]
            regrouped = [_residue_major(src, step, prev_dil) for src in srcs]
            for j, val in enumerate(regrouped):
                hbuf[j] = val
            h = jnp.concatenate(regrouped[:tiles], axis=1).astype(BF16)
            tabs = tuple(regrouped[tiles:])
            prev_dil = dil
        q = (rope_all(proj(h, OFF_QB + pat * B_PW, B_PW), tabs) * scale).astype(BF16)
        k = rope_all(proj(h, OFF_KB + pat * B_PW, B_PW), tabs).astype(BF16)
        v = proj(h, OFF_VB + pat * B_PW, B_PW).astype(BF16)
        n = tm // dil
        for r in range(dil):
            q_ref[:, r * B_PW:(r + 1) * B_PW] = q[r * n:(r + 1) * n]
            k_ref[:, r * B_PW:(r + 1) * B_PW] = k[r * n:(r + 1) * n]
            v_ref[:, r * B_PW:(r + 1) * B_PW] = v[r * n:(r + 1) * n]


def _attn_in(x, g, w_in, b_in, tables, tm=1024):
    bsz, s, d = x.shape
    in_w = w_in.shape[1]
    tok = lambda w: pl.BlockSpec((None, tm, w), lambda b, i: (b, i, 0))
    full = _resident
    out_specs = [tok(A_Q_W), tok(2 * KV_DUP_W)]
    out_shape = [jax.ShapeDtypeStruct((bsz, s, A_Q_W), BF16), jax.ShapeDtypeStruct((bsz, s, 2 * KV_DUP_W), BF16)]
    for _, dil in B_PATTERNS:
        out_specs += [pl.BlockSpec((None, tm // dil, dil * B_PW), lambda b, i: (b, i, 0))] * 3
        out_shape += [jax.ShapeDtypeStruct((bsz, s // dil, dil * B_PW), BF16)] * 3
    outs = pl.pallas_call(
        _attn_in_kernel,
        grid=(bsz, s // tm),
        in_specs=[tok(d), full((1, d)), full((d, in_w)), full((1, in_w)),
                  tok(LANES), tok(LANES), tok(LANES)],
        out_specs=out_specs,
        out_shape=out_shape,
        scratch_shapes=[pltpu.VMEM((d // LANES + 3, tm, LANES), F32)],
        compiler_params=_cparams("parallel", "parallel"),
        name="attn_in",
    )(x, g.reshape(1, d), w_in, b_in.reshape(1, in_w), *tables)
    return outs[0], outs[1], [outs[2 + 3 * p:5 + 3 * p] for p in range(len(B_PATTERNS))]


def _band_bias(rows, max_dist, first_tile):
    qi = lax.broadcasted_iota(jnp.int32, (rows, 2 * BLOCK), 0) % BLOCK
    kj = lax.broadcasted_iota(jnp.int32, (rows, 2 * BLOCK), 1)
    dist = BLOCK + qi - kj
    band = (dist >= 0) & (dist <= max_dist)
    inner = jnp.where(band, 0.0, NEG_BIG)
    return jnp.where(band & ((kj >= BLOCK) | jnp.logical_not(first_tile)), 0.0, NEG_BIG), inner


def _window_rows(prev_ref, cur_ref, jb, lanes):
    if jb == 0:
        return jnp.concatenate([prev_ref[:, lanes], cur_ref[0:BLOCK, lanes]], axis=0)
    return cur_ref[(jb - 1) * BLOCK:(jb + 1) * BLOCK, lanes]


def _attn_a_kernel(sink_ref, q_ref, kvp_ref, kvc_ref, o_ref):
    nb = q_ref.shape[0] // BLOCK
    rows = A_GROUP * BLOCK
    first_tile = pl.program_id(1) == 0
    qi = lax.broadcasted_iota(jnp.int32, (rows, BLOCK), 0) % BLOCK
    from_prev = lax.broadcasted_iota(jnp.int32, (rows, BLOCK), 1) > qi
    low = lax.broadcasted_iota(jnp.int32, (BLOCK, LANES), 1) < HEAD_DIM
    rowg = lax.broadcasted_iota(jnp.int32, (rows, 1), 0) // BLOCK
    sinks = []
    for hk in range(A_KV_HEADS):
        sk = jnp.zeros((rows, 1), F32)
        for g in range(A_GROUP):
            sk = jnp.where(rowg == g, sink_ref[hk * A_GROUP + g], sk)
        sinks.append(sk)
    kvh = range(A_KV_HEADS)
    for jb in range(nb):
        rs = slice(jb * BLOCK, (jb + 1) * BLOCK)
        sc = []
        for hk in kvh:
            tiles = [q_ref[rs, (2 * hk + t) * LANES:(2 * hk + t + 1) * LANES] for t in range(2)]
            zero = jnp.zeros_like(tiles[0])
            q4 = jnp.concatenate([jnp.where(low, tiles[0], zero), jnp.where(low, zero, tiles[0]),
                                  jnp.where(low, tiles[1], zero), jnp.where(low, zero, tiles[1])], axis=0)
            k = _window_rows(kvp_ref, kvc_ref, jb, slice(hk * LANES, (hk + 1) * LANES))
            both = _dot_nt(q4, k)
            prev = both[:, :BLOCK]
            if jb == 0:
                prev = jnp.where(first_tile, NEG_BIG, prev)
            sc.append(jnp.where(from_prev, prev, both[:, BLOCK:]))
        m = [jnp.max(s, axis=-1, keepdims=True) for s in sc]
        p = [jnp.exp(s - mm) for s, mm in zip(sc, m)]
        den = [jnp.sum(pp, axis=-1, keepdims=True) + jnp.exp(sk - mm) for pp, sk, mm in zip(p, sinks, m)]
        o = []
        for hk in kvh:
            v = _window_rows(kvp_ref, kvc_ref, jb, slice((A_KV_HEADS + hk) * LANES, (A_KV_HEADS + hk + 1) * LANES))
            pb = p[hk].astype(BF16)
            zero = jnp.zeros_like(pb)
            unfolded = jnp.concatenate([jnp.where(from_prev, pb, zero), jnp.where(from_prev, zero, pb)], axis=1)
            o.append(_dot(unfolded, v) * (1.0 / den[hk]))
        for hk in kvh:
            for t in range(2):
                pair = jnp.where(low, o[hk][2 * t * BLOCK:(2 * t + 1) * BLOCK], o[hk][(2 * t + 1) * BLOCK:(2 * t + 2) * BLOCK])
                o_ref[rs, (2 * hk + t) * LANES:(2 * hk + t + 1) * LANES] = pair.astype(BF16)


def _attn_a(qa, kva, sinks, tile=1024):
    assert A_WINDOW == BLOCK, "the score fold needs a window of exactly one block"
    bsz, s, _ = qa.shape
    tile = min(tile, s)
    kvw = kva.shape[2]
    per = tile // BLOCK
    return pl.pallas_call(
        _attn_a_kernel,
        grid=(bsz, s // tile),
        in_specs=[pl.BlockSpec(memory_space=pltpu.SMEM),
                  pl.BlockSpec((None, tile, A_Q_W), lambda b, i: (b, i, 0)),
                  pl.BlockSpec((None, BLOCK, kvw), lambda b, i: (b, jnp.maximum(i * per - 1, 0), 0)),
                  pl.BlockSpec((None, tile, kvw), lambda b, i: (b, i, 0))],
        out_specs=pl.BlockSpec((None, tile, A_Q_W), lambda b, i: (b, i, 0)),
        out_shape=jax.ShapeDtypeStruct((bsz, s, A_Q_W), BF16),
        compiler_params=_cparams("parallel", "parallel"),
        name="attn_a",
    )(sinks, qa, kva, kva)


def _attn_b_kernel(group, q_ref, kp_ref, kc_ref, vp_ref, vc_ref, o_ref, l_ref):
    nb = q_ref.shape[0] // BLOCK
    bias_first, bias_inner = _band_bias(BLOCK, BLOCK, pl.program_id(2) == 0)
    low = lax.broadcasted_iota(jnp.int32, (BLOCK, LANES), 1) < HEAD_DIM
    work = [(res, jb) for res in range(q_ref.shape[1] // B_PW) for jb in range(nb)]
    for w0 in range(0, len(work), group):
        items = [(res, jb, h) for res, jb in work[w0:w0 + group] for h in range(B_HEADS)]
        lanes = [slice(res * B_PW + (h // 2) * LANES, res * B_PW + (h // 2 + 1) * LANES) for res, _, h in items]
        sc = []
        for (_, jb, h), ls in zip(items, lanes):
            qt = q_ref[jb * BLOCK:(jb + 1) * BLOCK, ls]
            zero = jnp.zeros_like(qt)
            qh = jnp.where(low, qt, zero) if h % 2 == 0 else jnp.where(low, zero, qt)
            sc.append(_dot_nt(qh, _window_rows(kp_ref, kc_ref, jb, ls)) + (bias_first if jb == 0 else bias_inner))
        m = [jnp.max(s, axis=-1, keepdims=True) for s in sc]
        p = [jnp.exp(s - mm) for s, mm in zip(sc, m)]
        den = [jnp.sum(pp, axis=-1, keepdims=True) for pp in p]
        o = [_dot(pp.astype(BF16), _window_rows(vp_ref, vc_ref, jb, ls)) * (1.0 / dd)
             for pp, dd, (_, jb, _), ls in zip(p, den, items, lanes)]
        lse = [mm + jnp.log(dd) for mm, dd in zip(m, den)]
        for idx in range(0, len(items), 2):
            _, jb, _ = items[idx]
            dst = (slice(jb * BLOCK, (jb + 1) * BLOCK), lanes[idx])
            o_ref[dst] = jnp.where(low, o[idx], o[idx + 1])
            l_ref[dst] = jnp.where(low, lse[idx], lse[idx + 1])


def _attn_b(q, k, v, pat, dil, tile=1024, group=2):
    bsz, length, _ = q.shape
    res = max(1, min(tile // length, dil))
    tile = min(tile, length)
    per = tile // BLOCK
    cur = pl.BlockSpec((None, tile, res * B_PW), lambda b, r, i: (b, i, r))
    prev = pl.BlockSpec((None, BLOCK, res * B_PW), lambda b, r, i: (b, jnp.maximum(i * per - 1, 0), r))
    return pl.pallas_call(
        functools.partial(_attn_b_kernel, group),
        grid=(bsz, dil // res, length // tile),
        in_specs=[cur, prev, cur, prev, cur],
        out_specs=[cur, cur],
        out_shape=[jax.ShapeDtypeStruct((bsz, length, dil * B_PW), F32)] * 2,
        compiler_params=_cparams("parallel", "parallel", "parallel"),
        name=f"attn_b{pat}",
    )(q, k, k, v, v)


def _attn_out_kernel(x_ref, oa_ref, o0_ref, o1_ref, o2_ref, l0_ref, l1_ref, l2_ref, w_ref,
                     gf_ref, wg_ref, wu_ref, wd_ref, out_ref, buf, acc_ref):
    tm = x_ref.shape[0]

    def natural(ref, dil, slot):
        if dil == 1:
            return ref[...]
        n = tm // dil
        halves = B_PW // LANES
        for r in range(dil):
            for j in range(halves):
                buf[slot * halves + j, pl.ds(r, n, stride=dil), :] = ref[:, r * B_PW + j * LANES:r * B_PW + (j + 1) * LANES]
        return jnp.concatenate([buf[slot * halves + j] for j in range(halves)], axis=1)

    dils = [dil for _, dil in B_PATTERNS]
    o = [natural(ref, dil, i) for i, (ref, dil) in enumerate(zip((o0_ref, o1_ref, o2_ref), dils))]
    l = [natural(ref, dil, 3 + i) for i, (ref, dil) in enumerate(zip((l0_ref, l1_ref, l2_ref), dils))]
    m = jnp.maximum(jnp.maximum(l[0], l[1]), l[2])
    e = [jnp.exp(v - m) for v in l]
    ob = (e[0] * o[0] + e[1] * o[1] + e[2] * o[2]) / (e[0] + e[1] + e[2])
    mix = _dot(oa_ref[...], w_ref[:A_Q_W, :]) + _dot(ob.astype(BF16), w_ref[A_Q_W:, :])
    out_ref[...] = _ffn_block(x_ref[...] + mix, gf_ref, wg_ref, wu_ref, wd_ref, acc_ref)


def _attn_out(x, oa, obs, lses, w_out, ffn, tm=512):
    bsz, s, d = x.shape
    tok = lambda w: pl.BlockSpec((None, tm, w), lambda b, i: (b, i, 0))
    pat = [pl.BlockSpec((None, tm // dil, dil * B_PW), lambda b, i: (b, i, 0)) for _, dil in B_PATTERNS]
    return pl.pallas_call(
        _attn_out_kernel,
        grid=(bsz, s // tm),
        in_specs=[tok(d), tok(A_Q_W)] + pat + pat + [_resident(a.shape) for a in (w_out,) + ffn],
        out_specs=tok(d),
        out_shape=jax.ShapeDtypeStruct((bsz, s, d), F32),
        scratch_shapes=[pltpu.VMEM((2 * len(B_PATTERNS) * (B_PW // LANES), tm, LANES), F32),
                        pltpu.VMEM((tm, d), F32)],
        compiler_params=_cparams("parallel", "parallel"),
        name="attn_out_ffn",
    )(x, oa, *obs, *lses, w_out, *ffn)


FF_CHUNK = 256


def _ffn_block(x, g_ref, wg_ref, wu_ref, wd_ref, acc_ref):
    h = _rms(x, g_ref[...]).astype(BF16)
    for c in range(wg_ref.shape[1] // FF_CHUNK):
        sl = slice(c * FF_CHUNK, (c + 1) * FF_CHUNK)
        gate = _dot(h, wg_ref[:, sl])
        up = _dot(h, wu_ref[:, sl])
        contrib = _dot((gate * _sigmoid(gate) * up).astype(BF16), wd_ref[sl, :])
        if c == 0:
            acc_ref[...] = x + contrib
        else:
            acc_ref[...] += contrib
    return acc_ref[...]


def _resident(shape):
    return pl.BlockSpec(shape, lambda *_: (0,) * len(shape), pipeline_mode=pl.Buffered(1))


def _rwkv_in_kernel(has_vlora, *refs):
    (x_ref, gn_ref, mu_ref, wr_ref, wk_ref, wv_ref, w0_ref, w1_ref, w2_ref, a0_ref, a1_ref, a2_ref,
     g1_ref, g2_ref, kk_ref, ka_ref) = refs[:16]
    refs = refs[16:]
    if has_vlora:
        v0_ref, v1_ref, v2_ref, vf_ref = refs[:4]
        refs = refs[4:]
    r_o, lw_o, k_o, v_o, kk_o, ab_o, g_o, carry, mix = refs
    tm, d = x_ref.shape

    @pl.when(pl.program_id(1) == 0)
    def _():
        carry[...] = jnp.zeros(carry.shape, F32)

    h = _rms(x_ref[...], gn_ref[...])
    rolled = pltpu.roll(h, 1, 0)
    first = lax.broadcasted_iota(jnp.int32, (8, 1), 0) == 0
    hprev = jnp.concatenate([jnp.where(first, carry[7:8, :], rolled[0:8]), rolled[8:]], axis=0)
    carry[...] = h[tm - 8:tm, :]
    hb = h.astype(BF16)
    xxb = (hprev - h).astype(BF16)
    mub = mu_ref[...].astype(BF16)
    for i in range(mix.shape[0]):
        mix[i] = hb + xxb * mub[i:i + 1, :]

    mid_w = jnp.tanh(_dot(mix[1], w1_ref[...])).astype(BF16)
    mid_a = _dot(mix[4], a1_ref[...]).astype(BF16)
    mid_g = _sigmoid(_dot(mix[5], g1_ref[...])).astype(BF16)
    if has_vlora:
        mid_v = _dot(mix[3], v1_ref[...]).astype(BF16)

    ones_bd = jnp.where(_head_mask(MXU_W), 1.0, 0.0).astype(BF16)
    for c in range(d // MXU_W):
        cs = slice(c * MXU_W, (c + 1) * MXU_W)
        r_o[:, cs] = _dot(mix[0], wr_ref[:, cs])
        lw_o[:, cs] = -DECAY_SCALE * _sigmoid(w0_ref[:, cs] + _dot(mid_w, w2_ref[:, cs]))
        v = _dot(mix[3], wv_ref[:, cs])
        if has_vlora:
            v = v + (vf_ref[:, cs] - v) * _sigmoid(v0_ref[:, cs] + _dot(mid_v, v2_ref[:, cs]))
        v_o[:, cs] = v
        g_o[:, cs] = _dot(mid_g, g2_ref[:, cs]).astype(BF16)
        k = _dot(mix[2], wk_ref[:, cs])
        a = _sigmoid(a0_ref[:, cs] + _dot(mid_a, a2_ref[:, cs]))
        kk = k * kk_ref[:, cs]
        kk = kk / jnp.maximum(jnp.sqrt(_head_sum(kk * kk, ones_bd)), 1e-12)
        kk_o[:, cs] = kk
        ab_o[:, cs] = kk * a
        k_o[:, cs] = k * (1.0 + (a - 1.0) * ka_ref[:, cs])


def _rwkv_in(x, gn, p, v_lora, v_first, tm=512):
    bsz, s, d = x.shape
    tok = pl.BlockSpec((None, tm, d), lambda b, i: (b, i, 0))
    row = lambda a: a.reshape(1, d)
    args = [x, row(gn), p["mu"], p["w_r"], p["w_k"], p["w_v"], row(p["w0"]), p["w1"], p["w2"], row(p["a0"]), p["a1"], p["a2"],
            p["g1"], p["g2"], row(p["k_k"]), row(p["k_a"])]
    specs = [tok] + [_resident(a.shape) for a in args[1:]]
    if v_lora is not None:
        v0, v1, v2 = v_lora
        extra = [row(v0), v1, v2]
        args += extra + [v_first]
        specs += [_resident(a.shape) for a in extra] + [tok]
    return pl.pallas_call(
        functools.partial(_rwkv_in_kernel, v_lora is not None),
        grid=(bsz, s // tm),
        in_specs=specs,
        out_specs=[tok] * 7,
        out_shape=[jax.ShapeDtypeStruct((bsz, s, d), F32)] * 6 + [jax.ShapeDtypeStruct((bsz, s, d), BF16)],
        scratch_shapes=[pltpu.VMEM((8, d), F32), pltpu.VMEM((p["mu"].shape[0], tm, d), BF16)],
        compiler_params=_cparams("parallel", "arbitrary"),
        name="rwkv_in",
    )(*args)


def _bd2(z):
    lane = lax.broadcasted_iota(jnp.int32, z.shape, 1)
    zero = jnp.zeros(z.shape, z.dtype)
    return jnp.concatenate([jnp.where(lane < HEAD_DIM, z, zero), jnp.where(lane >= HEAD_DIM, z, zero)], axis=0)


def _fold2(full):
    lane = lax.broadcasted_iota(jnp.int32, (HEAD_DIM, LANES), 1)
    return jnp.where(lane < HEAD_DIM, full[:HEAD_DIM], full[HEAD_DIM:])


def _wkv_prep_kernel(r_ref, lw_ref, k_ref, v_ref, kk_ref, ab_ref, rk_ref,
                     q_ref, y0_ref, mc_ref, z_ref, dec_ref, bonus_ref):
    cs = WKV_CHUNK
    rows, d = r_ref.shape
    nc = rows // cs
    ri = lax.broadcasted_iota(jnp.int32, (rows, rows), 0)
    ci = lax.broadcasted_iota(jnp.int32, (rows, rows), 1)
    tri = jnp.where((ri >= ci) & (ri // cs == ci // cs), 1.0, 0.0).astype(BF16)
    lw = lw_ref[...]
    hi = lw.astype(BF16)
    lo = (lw - hi.astype(F32)).astype(BF16)
    cum = _dot(tri, hi) + _dot(tri, lo)
    totals = [cum[(j + 1) * cs - 1:(j + 1) * cs, :] for j in range(nc)]
    tot = jnp.concatenate([jnp.broadcast_to(t, (cs, d)) for t in totals], axis=0)
    e_neg = jnp.exp(-cum)
    e_tot = jnp.exp(tot - cum)
    kk, ab, kx = kk_ref[...], ab_ref[...], k_ref[...]
    at = (-kk * jnp.exp(cum - lw)).astype(BF16)
    rt = r_ref[...] * jnp.exp(cum)
    rtb = rt.astype(BF16)
    bt = (ab * e_neg).astype(BF16)
    kt = (kx * e_neg).astype(BF16)
    bh = (ab * e_tot).astype(BF16)
    kh = (kx * e_tot).astype(BF16)
    vb = v_ref[...].astype(BF16)
    ones_bd = jnp.where(_head_mask(MXU_W), 1.0, 0.0).astype(BF16)
    bonus_ref[...] = _head_sum(r_ref[...] * kx * rk_ref[...], ones_bd, split=False) * v_ref[...]

    trow = lax.broadcasted_iota(jnp.int32, (cs, LANES), 0)
    tcol = lax.broadcasted_iota(jnp.int32, (cs, LANES), 1) % HEAD_DIM
    strict = trow > tcol
    incl = trow >= tcol
    eye = jnp.where(trow == tcol, 1.0, 0.0)

    for j in range(nc):
        dec_ref[8 * j:8 * j + 8, :] = jnp.broadcast_to(jnp.exp(totals[j]), (8, d))

    chains = [(slice(j * cs, (j + 1) * cs), slice(p * LANES, (p + 1) * LANES))
              for j in range(nc) for p in range(d // LANES)]

    g12 = [_dot_nt(jnp.concatenate([at[c], rtb[c]], axis=0),
                   jnp.concatenate([_bd2(bt[c]), _bd2(kt[c])], axis=0)) for c in chains]
    a_ab = [jnp.where(strict, g[:cs, :LANES], 0.0) for g in g12]
    a_rb = [jnp.where(incl, g[cs:, :LANES], 0.0).astype(BF16) for g in g12]
    a_k = [jnp.concatenate([jnp.where(strict, g[:cs, LANES:], 0.0),
                            jnp.where(incl, g[cs:, LANES:], 0.0)], axis=0).astype(BF16) for g in g12]
    g4 = [_dot(a, _bd2(vb[c])) for a, c in zip(a_k, chains)]
    x = [eye + jnp.where((trow - tcol == 1) & (trow % 2 == 1), a, 0.0) for a in a_ab]
    size = 2
    while size < cs:
        lower_left = (trow // size - tcol // size == 1) & ((trow // size) % 2 == 1)
        xb = [v.astype(BF16) for v in x]
        xn = [_dot(v, _bd2(jnp.where(lower_left, a, 0.0).astype(BF16))) for v, a in zip(xb, a_ab)]
        x = [v + _dot(n.astype(BF16), _bd2(vb16)) for v, n, vb16 in zip(x, xn, xb)]
        size *= 2
    wu = [_dot(v.astype(BF16), jnp.concatenate([_bd2(at[c]), _bd2(g[:cs].astype(BF16))], axis=1))
          for v, g, c in zip(x, g4, chains)]
    wb = [v[:, :LANES].astype(BF16) for v in wu]
    ub = [v[:, LANES:].astype(BF16) for v in wu]
    qy = [_dot(a, jnp.concatenate([_bd2(w), _bd2(u)], axis=1)) for a, w, u in zip(a_rb, wb, ub)]
    mc = [_dot_tn(w, bh[c]) for w, c in zip(wb, chains)]
    zz = [_dot_tn(jnp.concatenate([u, vb[c]], axis=0), jnp.concatenate([bh[c], kh[c]], axis=0))
          for u, c in zip(ub, chains)]
    for i, c in enumerate(chains):
        q_ref[c] = (rt[c] + qy[i][:, :LANES]).astype(BF16)
        y0_ref[c] = g4[i][cs:] + qy[i][:, LANES:]
        mc_ref[c] = _fold2(mc[i]).astype(BF16)
        z_ref[c] = _fold2(zz[i])


def _wkv_out_kernel(final_norm, q_ref, y0_ref, mc_ref, z_ref, dec_ref, x_ref, bonus_ref, g_ref, lnw_ref, lnb_ref,
                    wo_ref, gf_ref, wg_ref, wu_ref, wd_ref, gl_ref, out_ref, s_ref, y_buf, acc_ref):
    cs = WKV_CHUNK
    bsz, rows, d = q_ref.shape

    @pl.when(pl.program_id(0) == 0)
    def _():
        s_ref[...] = jnp.zeros(s_ref.shape, F32)

    seqs = [(b, slice(p * LANES, (p + 1) * LANES)) for b in range(bsz) for p in range(d // LANES)]
    state = [s_ref[b, :, ls] for b, ls in seqs]
    for j in range(rows // cs):
        rs = slice(j * cs, (j + 1) * cs)
        sb = [s.astype(BF16) for s in state]
        upd = [_dot(v, _bd2(mc_ref[b, rs, ls])) for v, (b, ls) in zip(sb, seqs)]
        for v, (b, ls) in zip(sb, seqs):
            y_buf[b * rows + j * cs:b * rows + (j + 1) * cs, ls] = _dot_nt(q_ref[b, rs, ls], _bd2(v)) + y0_ref[b, rs, ls]
        state = [s * dec_ref[b, 8 * j:8 * j + 1, ls] + u + z_ref[b, rs, ls]
                 for s, u, (b, ls) in zip(state, upd, seqs)]
    for s, (b, ls) in zip(state, seqs):
        s_ref[b, :, ls] = s

    ones_bd = jnp.where(_head_mask(MXU_W), 1.0, 0.0).astype(BF16)
    y = y_buf[...]
    dev = y - _head_sum(y, ones_bd, split=False) * (1.0 / HEAD_DIM)
    var = _head_sum(dev * dev, ones_bd, split=False) * (1.0 / HEAD_DIM)
    yn = dev * lax.rsqrt(var + LNX_EPS) * lnw_ref[...] + lnb_ref[...]
    gated = jnp.concatenate([((yn[b * rows:(b + 1) * rows] + bonus_ref[b]) * g_ref[b]).astype(BF16)
                             for b in range(bsz)], axis=0)
    x1 = jnp.concatenate([x_ref[b] for b in range(bsz)], axis=0) + _dot(gated, wo_ref[...])
    out = _ffn_block(x1, gf_ref, wg_ref, wu_ref, wd_ref, acc_ref)
    if final_norm:
        out = _rms(out, gl_ref[...])
    for b in range(bsz):
        out_ref[b] = out[b * rows:(b + 1) * rows]


def _wkv(x, r, lw, k, v, kk, ab, g, rk, lnw, lnb, wo, ffn, g_last, final_norm, prep_chunks=4, scan_chunks=2):
    bsz, s, d = r.shape
    cs = WKV_CHUNK
    rows = prep_chunks * cs
    blk = pl.BlockSpec((None, rows, d), lambda b, c: (b, c, 0))
    dec_blk = pl.BlockSpec((None, 8 * prep_chunks, d), lambda b, c: (b, c, 0))
    act = lambda dt: jax.ShapeDtypeStruct((bsz, s, d), dt)
    q, y0, mc, z, dec, bonus = pl.pallas_call(
        _wkv_prep_kernel,
        grid=(bsz, s // rows),
        in_specs=[blk] * 6 + [pl.BlockSpec((1, d), lambda b, c: (0, 0))],
        out_specs=[blk, blk, blk, blk, dec_blk, blk],
        out_shape=[act(BF16), act(F32), act(BF16), act(F32),
                   jax.ShapeDtypeStruct((bsz, 8 * s // cs, d), F32), act(F32)],
        compiler_params=_cparams("parallel", "parallel"),
        name="wkv7_prep",
    )(r, lw, k, v, kk, ab, rk.reshape(1, d))
    rows = scan_chunks * cs
    blk = pl.BlockSpec((bsz, rows, d), lambda c: (0, c, 0))
    dec_blk = pl.BlockSpec((bsz, 8 * scan_chunks, d), lambda c: (0, c, 0))
    row = pl.BlockSpec((1, d), lambda c: (0, 0))
    return pl.pallas_call(
        functools.partial(_wkv_out_kernel, final_norm),
        grid=(s // rows,),
        in_specs=[blk, blk, blk, blk, dec_blk, blk, blk, blk, row, row]
                 + [_resident(a.shape) for a in (wo,) + ffn] + [row],
        out_specs=blk,
        out_shape=act(F32),
        scratch_shapes=[pltpu.VMEM((bsz, HEAD_DIM, d), F32), pltpu.VMEM((bsz * rows, d), F32),
                        pltpu.VMEM((bsz * rows, d), F32)],
        compiler_params=_cparams("arbitrary"),
        name="wkv7_out_ffn",
    )(q, y0, mc, z, dec, x, bonus, g, lnw.reshape(1, d), lnb.reshape(1, d), wo, *ffn, g_last.reshape(1, d))


def kernel(x, positions, norm_mix, norm_ffn, norm_final, attn_w_in, attn_b_in, attn_sinks, attn_w_out, rwkv_mu, rwkv_w_rkv, rwkv_w0, rwkv_w1, rwkv_w2, rwkv_a0, rwkv_a1, rwkv_a2, rwkv_g1, rwkv_g2, rwkv_k_k, rwkv_k_a, rwkv_r_k, rwkv_lnx_w, rwkv_lnx_b, rwkv_w_o, rwkv_v0, rwkv_v1, rwkv_v2, ffn_w_gate, ffn_w_up, ffn_w_down):
    depth = norm_mix.shape[0]
    assert depth % 2 == 0, "the final norm is fused into the last RWKV layer"
    assert all(win // dil == BLOCK for win, dil in B_PATTERNS), "mixer B bands must span one block"
    bf = lambda a: a.astype(BF16)
    tables = _rope_tables(positions)
    v_first = None
    for layer in range(depth):
        i = layer // 2
        ffn = (norm_ffn[layer].reshape(1, -1), bf(ffn_w_gate[layer]), bf(ffn_w_up[layer]), bf(ffn_w_down[layer]))
        if layer % 2 == 0:
            qa, kva, qkv_b = _attn_in(x, norm_mix[layer], bf(attn_w_in[i]), attn_b_in[i], tables)
            oa = _attn_a(qa, kva, attn_sinks[i])
            obs, lses = zip(*[_attn_b(*qkv_b[pat], pat, dil) for pat, (_, dil) in enumerate(B_PATTERNS)])
            x = _attn_out(x, oa, obs, lses, bf(attn_w_out[i]), ffn)
        else:
            p = dict(mu=rwkv_mu[i], w_r=bf(rwkv_w_rkv[i, 0]), w_k=bf(rwkv_w_rkv[i, 1]), w_v=bf(rwkv_w_rkv[i, 2]),
                     w0=rwkv_w0[i], w1=bf(rwkv_w1[i]), w2=bf(rwkv_w2[i]),
                     a0=rwkv_a0[i], a1=bf(rwkv_a1[i]), a2=bf(rwkv_a2[i]), g1=bf(rwkv_g1[i]), g2=bf(rwkv_g2[i]),
                     k_k=rwkv_k_k[i], k_a=rwkv_k_a[i])
            v_lora = None if i == 0 else (rwkv_v0[i - 1], bf(rwkv_v1[i - 1]), bf(rwkv_v2[i - 1]))
            r, lw, k, v, kk, ab, g = _rwkv_in(x, norm_mix[layer], p, v_lora, v_first)
            if i == 0:
                v_first = v
            x = _wkv(x, r, lw, k, v, kk, ab, g, rwkv_r_k[i], rwkv_lnx_w[i], rwkv_lnx_b[i], bf(rwkv_w_o[i]),
                     ffn, norm_final, final_norm=(layer == depth - 1))
    return x
```

```python
import functools

import jax
import jax.numpy as jnp
from jax import lax
from jax.experimental import pallas as pl
from jax.experimental.pallas import tpu as pltpu

F32 = jnp.float32
BF16 = jnp.bfloat16

HEAD_DIM = 64
ROT_DIM = HEAD_DIM // 4
ROT_HALF = ROT_DIM // 2
ROPE_THETA = 500000.0
BLOCK = 128
NORM_EPS = 1e-5
LNX_EPS = 64e-5

A_Q_HEADS = 12
A_KV_HEADS = 3
A_GROUP = A_Q_HEADS // A_KV_HEADS
A_WINDOW = 128
B_PATTERNS = ((128, 1), (512, 4), (2048, 16))
B_HEADS = 4

A_Q_W = A_Q_HEADS * HEAD_DIM
A_KV_W = A_KV_HEADS * HEAD_DIM
B_PW = B_HEADS * HEAD_DIM
B_W = len(B_PATTERNS) * B_PW
KV_DUP_W = 2 * A_KV_W
OFF_KA = A_Q_W
OFF_QB = OFF_KA + 2 * A_KV_W
OFF_KB = OFF_QB + B_W
OFF_VB = OFF_KB + B_W

LANES = 128
MXU_W = 256
WKV_CHUNK = 64
DECAY_SCALE = 0.6065306597126334
NEG_BIG = -1e30
VMEM_LIMIT = 56 * 1024 * 1024


def _cparams(*sem):
    return pltpu.CompilerParams(dimension_semantics=sem, vmem_limit_bytes=VMEM_LIMIT)


def _dot(a, b):
    return jnp.dot(a, b, preferred_element_type=F32)


def _dot_nt(a, b):
    return lax.dot_general(a, b, (((1,), (1,)), ((), ())), preferred_element_type=F32)


def _dot_tn(a, b):
    return lax.dot_general(a, b, (((0,), (0,)), ((), ())), preferred_element_type=F32)


def _row_sum(z):
    part = z[:, :LANES]
    for j in range(1, z.shape[1] // LANES):
        part = part + z[:, j * LANES:(j + 1) * LANES]
    ones = jnp.ones((LANES, LANES), BF16)
    hi = part.astype(BF16)
    lo = (part - hi.astype(F32)).astype(BF16)
    return _dot(hi, ones) + _dot(lo, ones)


def _rms(x, g):
    inv = lax.rsqrt(_row_sum(x * x) * (1.0 / x.shape[1]) + NORM_EPS)
    return x * jnp.concatenate([inv] * (x.shape[1] // LANES), axis=1) * g


def _sigmoid(z):
    return 1.0 / (1.0 + jnp.exp(-z))


def _head_mask(n):
    r = lax.broadcasted_iota(jnp.int32, (n, n), 0) // HEAD_DIM
    c = lax.broadcasted_iota(jnp.int32, (n, n), 1) // HEAD_DIM
    return r == c


def _head_sum(z, ones_bd, split=True):
    rows, n = z.shape[0], z.shape[1] // MXU_W
    zc = jnp.concatenate([z[:, j * MXU_W:(j + 1) * MXU_W] for j in range(n)], axis=0)
    hi = zc.astype(BF16)
    acc = _dot(hi, ones_bd)
    if split:
        acc = acc + _dot((zc - hi.astype(F32)).astype(BF16), ones_bd)
    return jnp.concatenate([acc[j * rows:(j + 1) * rows] for j in range(n)], axis=1)


def _rope_trig_kernel(pos_ref, invf_ref, cos_ref, sin_ref):
    ang = pos_ref[...].astype(F32)[None] * invf_ref[...]
    cos_ref[...] = jnp.cos(ang)
    sin_ref[...] = jnp.sin(ang)


def _rope_tables(positions):
    bsz, s = positions.shape
    rows = bsz * s // LANES
    inv_freq = jnp.power(ROPE_THETA, -2.0 * jnp.arange(ROT_HALF, dtype=F32) / ROT_DIM)
    invf = jnp.broadcast_to(inv_freq[:, None, None], (ROT_HALF, 1, LANES))
    cos, sin = pl.pallas_call(
        _rope_trig_kernel,
        out_shape=(jax.ShapeDtypeStruct((ROT_HALF, rows, LANES), F32),) * 2,
        name="rope_trig",
    )(positions.reshape(rows, LANES), invf)
    cos = jnp.tile(cos.reshape(ROT_HALF, bsz, s).transpose(1, 2, 0), (1, 1, LANES // ROT_HALF))
    sin = jnp.tile(sin.reshape(ROT_HALF, bsz, s).transpose(1, 2, 0), (1, 1, LANES // ROT_HALF))
    dim = jnp.arange(LANES) % HEAD_DIM
    c = jnp.where(dim < ROT_DIM, cos, 1.0)
    sa = jnp.where(dim < ROT_HALF, -sin, 0.0)
    sb = jnp.where((dim >= ROT_HALF) & (dim < ROT_DIM), sin, 0.0)
    return c, sa, sb


def _residue_major(ref, dil, groups=1):
    span = ref.shape[0] // groups
    n = span // dil
    return jnp.concatenate([ref[pl.ds(g * span + r, n, stride=dil), :]
                            for r in range(dil) for g in range(groups)], axis=0)


def _attn_in_kernel(x_ref, g_ref, w_ref, b_ref, c_ref, sa_ref, sb_ref, qa_ref, kva_ref, *rest):
    b_refs, hbuf = rest[:-1], rest[-1]
    hf = _rms(x_ref[...], g_ref[...])
    tm, d = hf.shape
    tiles = d // LANES
    for j in range(tiles):
        hbuf[j] = hf[:, j * LANES:(j + 1) * LANES]
    scale = HEAD_DIM ** -0.5

    def proj(h, lo, width):
        return _dot(h, w_ref[:, lo:lo + width]) + b_ref[:, lo:lo + width]

    def rope(z, tabs):
        c, sa, sb = tabs
        return z * c + pltpu.roll(z, LANES - ROT_HALF, 1) * sa + pltpu.roll(z, ROT_HALF, 1) * sb

    def rope_all(z, tabs):
        return jnp.concatenate([rope(z[:, j * LANES:(j + 1) * LANES], tabs)
                                for j in range(z.shape[1] // LANES)], axis=1)

    h = hf.astype(BF16)
    tabs = (c_ref[...], sa_ref[...], sb_ref[...])
    for j in range(A_Q_W // MXU_W):
        qa_ref[:, j * MXU_W:(j + 1) * MXU_W] = (rope_all(proj(h, j * MXU_W, MXU_W), tabs) * scale).astype(BF16)
    z = proj(h, OFF_KA, 2 * A_KV_W)
    low = lax.broadcasted_iota(jnp.int32, (tm, LANES), 1) < HEAD_DIM
    mid = z[:, LANES:2 * LANES]
    src = [rope(z[:, :LANES], tabs), jnp.where(low, rope(mid, tabs), mid), z[:, 2 * LANES:]]
    for j, t in enumerate(src):
        swapped = pltpu.roll(t, HEAD_DIM, 1)
        kva_ref[:, 2 * j * LANES:(2 * j + 1) * LANES] = jnp.where(low, t, swapped).astype(BF16)
        kva_ref[:, (2 * j + 1) * LANES:(2 * j + 2) * LANES] = jnp.where(low, swapped, t).astype(BF16)
    prev_dil = 1
    for pat, (_, dil) in enumerate(B_PATTERNS):
        q_ref, k_ref, v_ref = b_refs[3 * pat:3 * pat + 3]
        if dil > 1:
            step = dil // prev_dil
            srcs = [hbuf.at[j] for j in range(tiles)] + [hbuf.at[tiles + j] for j in range(3)]
            if prev_dil == 1:
                srcs[tiles:] = [c_ref, sa_ref, sb_ref]
            regrouped = [_residue_major(src, step, prev_dil) for src in srcs]
            for j, val in enumerate(regrouped):
                hbuf[j] = val
            h = jnp.concatenate(regrouped[:tiles], axis=1).astype(BF16)
            tabs = tuple(regrouped[tiles:])
            prev_dil = dil
        q = (rope_all(proj(h, OFF_QB + pat * B_PW, B_PW), tabs) * scale).astype(BF16)
        k = rope_all(proj(h, OFF_KB + pat * B_PW, B_PW), tabs).astype(BF16)
        v = proj(h, OFF_VB + pat * B_PW, B_PW).astype(BF16)
        n = tm // dil
        for r in range(dil):
            q_ref[:, r * B_PW:(r + 1) * B_PW] = q[r * n:(r + 1) * n]
            k_ref[:, r * B_PW:(r + 1) * B_PW] = k[r * n:(r + 1) * n]
            v_ref[:, r * B_PW:(r + 1) * B_PW] = v[r * n:(r + 1) * n]


def _attn_in(x, g, w_in, b_in, tables, tm=1024):
    bsz, s, d = x.shape
    in_w = w_in.shape[1]
    tok = lambda w: pl.BlockSpec((None, tm, w), lambda b, i: (b, i, 0))
    full = _resident
    out_specs = [tok(A_Q_W), tok(2 * KV_DUP_W)]
    out_shape = [jax.ShapeDtypeStruct((bsz, s, A_Q_W), BF16), jax.ShapeDtypeStruct((bsz, s, 2 * KV_DUP_W), BF16)]
    for _, dil in B_PATTERNS:
        out_specs += [pl.BlockSpec((None, tm // dil, dil * B_PW), lambda b, i: (b, i, 0))] * 3
        out_shape += [jax.ShapeDtypeStruct((bsz, s // dil, dil * B_PW), BF16)] * 3
    outs = pl.pallas_call(
        _attn_in_kernel,
        grid=(bsz, s // tm),
        in_specs=[tok(d), full((1, d)), full((d, in_w)), full((1, in_w)),
                  tok(LANES), tok(LANES), tok(LANES)],
        out_specs=out_specs,
        out_shape=out_shape,
        scratch_shapes=[pltpu.VMEM((d // LANES + 3, tm, LANES), F32)],
        compiler_params=_cparams("parallel", "parallel"),
        name="attn_in",
    )(x, g.reshape(1, d), w_in, b_in.reshape(1, in_w), *tables)
    return outs[0], outs[1], [outs[2 + 3 * p:5 + 3 * p] for p in range(len(B_PATTERNS))]


def _band_bias(rows, max_dist, first_tile):
    qi = lax.broadcasted_iota(jnp.int32, (rows, 2 * BLOCK), 0) % BLOCK
    kj = lax.broadcasted_iota(jnp.int32, (rows, 2 * BLOCK), 1)
    dist = BLOCK + qi - kj
    band = (dist >= 0) & (dist <= max_dist)
    inner = jnp.where(band, 0.0, NEG_BIG)
    return jnp.where(band & ((kj >= BLOCK) | jnp.logical_not(first_tile)), 0.0, NEG_BIG), inner


def _window_rows(prev_ref, cur_ref, jb, lanes):
    if jb == 0:
        return jnp.concatenate([prev_ref[:, lanes], cur_ref[0:BLOCK, lanes]], axis=0)
    return cur_ref[(jb - 1) * BLOCK:(jb + 1) * BLOCK, lanes]


def _attn_a_kernel(sink_ref, q_ref, kvp_ref, kvc_ref, o_ref):
    nb = q_ref.shape[0] // BLOCK
    rows = A_GROUP * BLOCK
    first_tile = pl.program_id(1) == 0
    qi = lax.broadcasted_iota(jnp.int32, (rows, BLOCK), 0) % BLOCK
    from_prev = lax.broadcasted_iota(jnp.int32, (rows, BLOCK), 1) > qi
    low = lax.broadcasted_iota(jnp.int32, (BLOCK, LANES), 1) < HEAD_DIM
    rowg = lax.broadcasted_iota(jnp.int32, (rows, 1), 0) // BLOCK
    sinks = []
    for hk in range(A_KV_HEADS):
        sk = jnp.zeros((rows, 1), F32)
        for g in range(A_GROUP):
            sk = jnp.where(rowg == g, sink_ref[hk * A_GROUP + g], sk)
        sinks.append(sk)
    kvh = range(A_KV_HEADS)
    for jb in range(nb):
        rs = slice(jb * BLOCK, (jb + 1) * BLOCK)
        sc = []
        for hk in kvh:
            tiles = [q_ref[rs, (2 * hk + t) * LANES:(2 * hk + t + 1) * LANES] for t in range(2)]
            zero = jnp.zeros_like(tiles[0])
            q4 = jnp.concatenate([jnp.where(low, tiles[0], zero), jnp.where(low, zero, tiles[0]),
                                  jnp.where(low, tiles[1], zero), jnp.where(low, zero, tiles[1])], axis=0)
            k = _window_rows(kvp_ref, kvc_ref, jb, slice(hk * LANES, (hk + 1) * LANES))
            both = _dot_nt(q4, k)
            prev = both[:, :BLOCK]
            if jb == 0:
                prev = jnp.where(first_tile, NEG_BIG, prev)
            sc.append(jnp.where(from_prev, prev, both[:, BLOCK:]))
        m = [jnp.max(s, axis=-1, keepdims=True) for s in sc]
        p = [jnp.exp(s - mm) for s, mm in zip(sc, m)]
        den = [jnp.sum(pp, axis=-1, keepdims=True) + jnp.exp(sk - mm) for pp, sk, mm in zip(p, sinks, m)]
        o = []
        for hk in kvh:
            v = _window_rows(kvp_ref, kvc_ref, jb, slice((A_KV_HEADS + hk) * LANES, (A_KV_HEADS + hk + 1) * LANES))
            pb = p[hk].astype(BF16)
            zero = jnp.zeros_like(pb)
            unfolded = jnp.concatenate([jnp.where(from_prev, pb, zero), jnp.where(from_prev, zero, pb)], axis=1)
            o.append(_dot(unfolded, v) * (1.0 / den[hk]))
        for hk in kvh:
            for t in range(2):
                pair = jnp.where(low, o[hk][2 * t * BLOCK:(2 * t + 1) * BLOCK], o[hk][(2 * t + 1) * BLOCK:(2 * t + 2) * BLOCK])
                o_ref[rs, (2 * hk + t) * LANES:(2 * hk + t + 1) * LANES] = pair.astype(BF16)


def _attn_a(qa, kva, sinks, tile=1024):
    assert A_WINDOW == BLOCK, "the score fold needs a window of exactly one block"
    bsz, s, _ = qa.shape
    tile = min(tile, s)
    kvw = kva.shape[2]
    per = tile // BLOCK
    return pl.pallas_call(
        _attn_a_kernel,
        grid=(bsz, s // tile),
        in_specs=[pl.BlockSpec(memory_space=pltpu.SMEM),
                  pl.BlockSpec((None, tile, A_Q_W), lambda b, i: (b, i, 0)),
                  pl.BlockSpec((None, BLOCK, kvw), lambda b, i: (b, jnp.maximum(i * per - 1, 0), 0)),
                  pl.BlockSpec((None, tile, kvw), lambda b, i: (b, i, 0))],
        out_specs=pl.BlockSpec((None, tile, A_Q_W), lambda b, i: (b, i, 0)),
        out_shape=jax.ShapeDtypeStruct((bsz, s, A_Q_W), BF16),
        compiler_params=_cparams("parallel", "parallel"),
        name="attn_a",
    )(sinks, qa, kva, kva)


def _attn_b_kernel(group, q_ref, kp_ref, kc_ref, vp_ref, vc_ref, o_ref, l_ref):
    nb = q_ref.shape[0] // BLOCK
    bias_first, bias_inner = _band_bias(BLOCK, BLOCK, pl.program_id(2) == 0)
    low = lax.broadcasted_iota(jnp.int32, (BLOCK, LANES), 1) < HEAD_DIM
    work = [(res, jb) for res in range(q_ref.shape[1] // B_PW) for jb in range(nb)]
    for w0 in range(0, len(work), group):
        items = [(res, jb, h) for res, jb in work[w0:w0 + group] for h in range(B_HEADS)]
        lanes = [slice(res * B_PW + (h // 2) * LANES, res * B_PW + (h // 2 + 1) * LANES) for res, _, h in items]
        sc = []
        for (_, jb, h), ls in zip(items, lanes):
            qt = q_ref[jb * BLOCK:(jb + 1) * BLOCK, ls]
            zero = jnp.zeros_like(qt)
            qh = jnp.where(low, qt, zero) if h % 2 == 0 else jnp.where(low, zero, qt)
            sc.append(_dot_nt(qh, _window_rows(kp_ref, kc_ref, jb, ls)) + (bias_first if jb == 0 else bias_inner))
        m = [jnp.max(s, axis=-1, keepdims=True) for s in sc]
        p = [jnp.exp(s - mm) for s, mm in zip(sc, m)]
        den = [jnp.sum(pp, axis=-1, keepdims=True) for pp in p]
        o = [_dot(pp.astype(BF16), _window_rows(vp_ref, vc_ref, jb, ls)) * (1.0 / dd)
             for pp, dd, (_, jb, _), ls in zip(p, den, items, lanes)]
        lse = [mm + jnp.log(dd) for mm, dd in zip(m, den)]
        for idx in range(0, len(items), 2):
            _, jb, _ = items[idx]
            dst = (slice(jb * BLOCK, (jb + 1) * BLOCK), lanes[idx])
            o_ref[dst] = jnp.where(low, o[idx], o[idx + 1])
            l_ref[dst] = jnp.where(low, lse[idx], lse[idx + 1])


def _attn_b(q, k, v, pat, dil, tile=1024, group=2):
    bsz, length, _ = q.shape
    res = max(1, min(tile // length, dil))
    tile = min(tile, length)
    per = tile // BLOCK
    cur = pl.BlockSpec((None, tile, res * B_PW), lambda b, r, i: (b, i, r))
    prev = pl.BlockSpec((None, BLOCK, res * B_PW), lambda b, r, i: (b, jnp.maximum(i * per - 1, 0), r))
    return pl.pallas_call(
        functools.partial(_attn_b_kernel, group),
        grid=(bsz, dil // res, length // tile),
        in_specs=[cur, prev, cur, prev, cur],
        out_specs=[cur, cur],
        out_shape=[jax.ShapeDtypeStruct((bsz, length, dil * B_PW), F32)] * 2,
        compiler_params=_cparams("parallel", "parallel", "parallel"),
        name=f"attn_b{pat}",
    )(q, k, k, v, v)


def _attn_out_kernel(x_ref, oa_ref, o0_ref, o1_ref, o2_ref, l0_ref, l1_ref, l2_ref, w_ref,
                     gf_ref, wg_ref, wu_ref, wd_ref, out_ref, buf, acc_ref):
    tm = x_ref.shape[0]

    def natural(ref, dil, slot):
        if dil == 1:
            return ref[...]
        n = tm // dil
        halves = B_PW // LANES
        for r in range(dil):
            for j in range(halves):
                buf[slot * halves + j, pl.ds(r, n, stride=dil), :] = ref[:, r * B_PW + j * LANES:r * B_PW + (j + 1) * LANES]
        return jnp.concatenate([buf[slot * halves + j] for j in range(halves)], axis=1)

    dils = [dil for _, dil in B_PATTERNS]
    o = [natural(ref, dil, i) for i, (ref, dil) in enumerate(zip((o0_ref, o1_ref, o2_ref), dils))]
    l = [natural(ref, dil, 3 + i) for i, (ref, dil) in enumerate(zip((l0_ref, l1_ref, l2_ref), dils))]
    m = jnp.maximum(jnp.maximum(l[0], l[1]), l[2])
    e = [jnp.exp(v - m) for v in l]
    ob = (e[0] * o[0] + e[1] * o[1] + e[2] * o[2]) / (e[0] + e[1] + e[2])
    mix = _dot(oa_ref[...], w_ref[:A_Q_W, :]) + _dot(ob.astype(BF16), w_ref[A_Q_W:, :])
    out_ref[...] = _ffn_block(x_ref[...] + mix, gf_ref, wg_ref, wu_ref, wd_ref, acc_ref)


def _attn_out(x, oa, obs, lses, w_out, ffn, tm=512):
    bsz, s, d = x.shape
    tok = lambda w: pl.BlockSpec((None, tm, w), lambda b, i: (b, i, 0))
    pat = [pl.BlockSpec((None, tm // dil, dil * B_PW), lambda b, i: (b, i, 0)) for _, dil in B_PATTERNS]
    return pl.pallas_call(
        _attn_out_kernel,
        grid=(bsz, s // tm),
        in_specs=[tok(d), tok(A_Q_W)] + pat + pat + [_resident(a.shape) for a in (w_out,) + ffn],
        out_specs=tok(d),
        out_shape=jax.ShapeDtypeStruct((bsz, s, d), F32),
        scratch_shapes=[pltpu.VMEM((2 * len(B_PATTERNS) * (B_PW // LANES), tm, LANES), F32),
                        pltpu.VMEM((tm, d), F32)],
        compiler_params=_cparams("parallel", "parallel"),
        name="attn_out_ffn",
    )(x, oa, *obs, *lses, w_out, *ffn)


FF_CHUNK = 256


def _ffn_block(x, g_ref, wg_ref, wu_ref, wd_ref, acc_ref):
    h = _rms(x, g_ref[...]).astype(BF16)
    for c in range(wg_ref.shape[1] // FF_CHUNK):
        sl = slice(c * FF_CHUNK, (c + 1) * FF_CHUNK)
        gate = _dot(h, wg_ref[:, sl])
        up = _dot(h, wu_ref[:, sl])
        contrib = _dot((gate * _sigmoid(gate) * up).astype(BF16), wd_ref[sl, :])
        if c == 0:
            acc_ref[...] = x + contrib
        else:
            acc_ref[...] += contrib
    return acc_ref[...]


def _resident(shape):
    return pl.BlockSpec(shape, lambda *_: (0,) * len(shape), pipeline_mode=pl.Buffered(1))


def _rwkv_in_kernel(has_vlora, *refs):
    (x_ref, gn_ref, mu_ref, wr_ref, wk_ref, wv_ref, w0_ref, w1_ref, w2_ref, a0_ref, a1_ref, a2_ref,
     g1_ref, g2_ref, kk_ref, ka_ref) = refs[:16]
    refs = refs[16:]
    if has_vlora:
        v0_ref, v1_ref, v2_ref, vf_ref = refs[:4]
        refs = refs[4:]
    r_o, lw_o, k_o, v_o, kk_o, ab_o, g_o, carry, mix = refs
    tm, d = x_ref.shape

    @pl.when(pl.program_id(1) == 0)
    def _():
        carry[...] = jnp.zeros(carry.shape, F32)

    h = _rms(x_ref[...], gn_ref[...])
    rolled = pltpu.roll(h, 1, 0)
    first = lax.broadcasted_iota(jnp.int32, (8, 1), 0) == 0
    hprev = jnp.concatenate([jnp.where(first, carry[7:8, :], rolled[0:8]), rolled[8:]], axis=0)
    carry[...] = h[tm - 8:tm, :]
    hb = h.astype(BF16)
    xxb = (hprev - h).astype(BF16)
    mub = mu_ref[...].astype(BF16)
    for i in range(mix.shape[0]):
        mix[i] = hb + xxb * mub[i:i + 1, :]

    mid_w = jnp.tanh(_dot(mix[1], w1_ref[...])).astype(BF16)
    mid_a = _dot(mix[4], a1_ref[...]).astype(BF16)
    mid_g = _sigmoid(_dot(mix[5], g1_ref[...])).astype(BF16)
    if has_vlora:
        mid_v = _dot(mix[3], v1_ref[...]).astype(BF16)

    ones_bd = jnp.where(_head_mask(MXU_W), 1.0, 0.0).astype(BF16)
    for c in range(d // MXU_W):
        cs = slice(c * MXU_W, (c + 1) * MXU_W)
        r_o[:, cs] = _dot(mix[0], wr_ref[:, cs])
        lw_o[:, cs] = -DECAY_SCALE * _sigmoid(w0_ref[:, cs] + _dot(mid_w, w2_ref[:, cs]))
        v = _dot(mix[3], wv_ref[:, cs])
        if has_vlora:
            v = v + (vf_ref[:, cs] - v) * _sigmoid(v0_ref[:, cs] + _dot(mid_v, v2_ref[:, cs]))
        v_o[:, cs] = v
        g_o[:, cs] = _dot(mid_g, g2_ref[:, cs]).astype(BF16)
        k = _dot(mix[2], wk_ref[:, cs])
        a = _sigmoid(a0_ref[:, cs] + _dot(mid_a, a2_ref[:, cs]))
        kk = k * kk_ref[:, cs]
        kk = kk / jnp.maximum(jnp.sqrt(_head_sum(kk * kk, ones_bd)), 1e-12)
        kk_o[:, cs] = kk
        ab_o[:, cs] = kk * a
        k_o[:, cs] = k * (1.0 + (a - 1.0) * ka_ref[:, cs])


def _rwkv_in(x, gn, p, v_lora, v_first, tm=512):
    bsz, s, d = x.shape
    tok = pl.BlockSpec((None, tm, d), lambda b, i: (b, i, 0))
    row = lambda a: a.reshape(1, d)
    args = [x, row(gn), p["mu"], p["w_r"], p["w_k"], p["w_v"], row(p["w0"]), p["w1"], p["w2"], row(p["a0"]), p["a1"], p["a2"],
            p["g1"], p["g2"], row(p["k_k"]), row(p["k_a"])]
    specs = [tok] + [_resident(a.shape) for a in args[1:]]
    if v_lora is not None:
        v0, v1, v2 = v_lora
        extra = [row(v0), v1, v2]
        args += extra + [v_first]
        specs += [_resident(a.shape) for a in extra] + [tok]
    return pl.pallas_call(
        functools.partial(_rwkv_in_kernel, v_lora is not None),
        grid=(bsz, s // tm),
        in_specs=specs,
        out_specs=[tok] * 7,
        out_shape=[jax.ShapeDtypeStruct((bsz, s, d), F32)] * 6 + [jax.ShapeDtypeStruct((bsz, s, d), BF16)],
        scratch_shapes=[pltpu.VMEM((8, d), F32), pltpu.VMEM((p["mu"].shape[0], tm, d), BF16)],
        compiler_params=_cparams("parallel", "arbitrary"),
        name="rwkv_in",
    )(*args)


def _bd2(z):
    lane = lax.broadcasted_iota(jnp.int32, z.shape, 1)
    zero = jnp.zeros(z.shape, z.dtype)
    return jnp.concatenate([jnp.where(lane < HEAD_DIM, z, zero), jnp.where(lane >= HEAD_DIM, z, zero)], axis=0)


def _fold2(full):
    lane = lax.broadcasted_iota(jnp.int32, (HEAD_DIM, LANES), 1)
    return jnp.where(lane < HEAD_DIM, full[:HEAD_DIM], full[HEAD_DIM:])


def _wkv_prep_kernel(r_ref, lw_ref, k_ref, v_ref, kk_ref, ab_ref, rk_ref,
                     q_ref, y0_ref, mc_ref, z_ref, dec_ref, bonus_ref):
    cs = WKV_CHUNK
    rows, d = r_ref.shape
    nc = rows // cs
    ri = lax.broadcasted_iota(jnp.int32, (rows, rows), 0)
    ci = lax.broadcasted_iota(jnp.int32, (rows, rows), 1)
    tri = jnp.where((ri >= ci) & (ri // cs == ci // cs), 1.0, 0.0).astype(BF16)
    lw = lw_ref[...]
    hi = lw.astype(BF16)
    lo = (lw - hi.astype(F32)).astype(BF16)
    cum = _dot(tri, hi) + _dot(tri, lo)
    totals = [cum[(j + 1) * cs - 1:(j + 1) * cs, :] for j in range(nc)]
    tot = jnp.concatenate([jnp.broadcast_to(t, (cs, d)) for t in totals], axis=0)
    e_neg = jnp.exp(-cum)
    e_tot = jnp.exp(tot - cum)
    kk, ab, kx = kk_ref[...], ab_ref[...], k_ref[...]
    at = (-kk * jnp.exp(cum - lw)).astype(BF16)
    rt = r_ref[...] * jnp.exp(cum)
    rtb = rt.astype(BF16)
    bt = (ab * e_neg).astype(BF16)
    kt = (kx * e_neg).astype(BF16)
    bh = (ab * e_tot).astype(BF16)
    kh = (kx * e_tot).astype(BF16)
    vb = v_ref[...].astype(BF16)
    ones_bd = jnp.where(_head_mask(MXU_W), 1.0, 0.0).astype(BF16)
    bonus_ref[...] = _head_sum(r_ref[...] * kx * rk_ref[...], ones_bd, split=False) * v_ref[...]

    trow = lax.broadcasted_iota(jnp.int32, (cs, LANES), 0)
    tcol = lax.broadcasted_iota(jnp.int32, (cs, LANES), 1) % HEAD_DIM
    strict = trow > tcol
    incl = trow >= tcol
    eye = jnp.where(trow == tcol, 1.0, 0.0)

    for j in range(nc):
        dec_ref[8 * j:8 * j + 8, :] = jnp.broadcast_to(jnp.exp(totals[j]), (8, d))

    chains = [(slice(j * cs, (j + 1) * cs), slice(p * LANES, (p + 1) * LANES))
              for j in range(nc) for p in range(d // LANES)]

    g12 = [_dot_nt(jnp.concatenate([at[c], rtb[c]], axis=0),
                   jnp.concatenate([_bd2(bt[c]), _bd2(kt[c])], axis=0)) for c in chains]
    a_ab = [jnp.where(strict, g[:cs, :LANES], 0.0) for g in g12]
    a_rb = [jnp.where(incl, g[cs:, :LANES], 0.0).astype(BF16) for g in g12]
    a_k = [jnp.concatenate([jnp.where(strict, g[:cs, LANES:], 0.0),
                            jnp.where(incl, g[cs:, LANES:], 0.0)], axis=0).astype(BF16) for g in g12]
    g4 = [_dot(a, _bd2(vb[c])) for a, c in zip(a_k, chains)]
    x = [eye + jnp.where((trow - tcol == 1) & (trow % 2 == 1), a, 0.0) for a in a_ab]
    size = 2
    while size < cs:
        lower_left = (trow // size - tcol // size == 1) & ((trow // size) % 2 == 1)
        xb = [v.astype(BF16) for v in x]
        xn = [_dot(v, _bd2(jnp.where(lower_left, a, 0.0).astype(BF16))) for v, a in zip(xb, a_ab)]
        x = [v + _dot(n.astype(BF16), _bd2(vb16)) for v, n, vb16 in zip(x, xn, xb)]
        size *= 2
    wu = [_dot(v.astype(BF16), jnp.concatenate([_bd2(at[c]), _bd2(g[:cs].astype(BF16))], axis=1))
          for v, g, c in zip(x, g4, chains)]
    wb = [v[:, :LANES].astype(BF16) for v in wu]
    ub = [v[:, LANES:].astype(BF16) for v in wu]
    qy = [_dot(a, jnp.concatenate([_bd2(w), _bd2(u)], axis=1)) for a, w, u in zip(a_rb, wb, ub)]
    mc = [_dot_tn(w, bh[c]) for w, c in zip(wb, chains)]
    zz = [_dot_tn(jnp.concatenate([u, vb[c]], axis=0), jnp.concatenate([bh[c], kh[c]], axis=0))
          for u, c in zip(ub, chains)]
    for i, c in enumerate(chains):
        q_ref[c] = (rt[c] + qy[i][:, :LANES]).astype(BF16)
        y0_ref[c] = g4[i][cs:] + qy[i][:, LANES:]
        mc_ref[c] = _fold2(mc[i]).astype(BF16)
        z_ref[c] = _fold2(zz[i])


def _wkv_out_kernel(final_norm, q_ref, y0_ref, mc_ref, z_ref, dec_ref, x_ref, bonus_ref, g_ref, lnw_ref, lnb_ref,
                    wo_ref, gf_ref, wg_ref, wu_ref, wd_ref, gl_ref, out_ref, s_ref, y_buf, acc_ref):
    cs = WKV_CHUNK
    bsz, rows, d = q_ref.shape

    @pl.when(pl.program_id(0) == 0)
    def _():
        s_ref[...] = jnp.zeros(s_ref.shape, F32)

    seqs = [(b, slice(p * LANES, (p + 1) * LANES)) for b in range(bsz) for p in range(d // LANES)]
    state = [s_ref[b, :, ls] for b, ls in seqs]
    for j in range(rows // cs):
        rs = slice(j * cs, (j + 1) * cs)
        sb = [s.astype(BF16) for s in state]
        upd = [_dot(v, _bd2(mc_ref[b, rs, ls])) for v, (b, ls) in zip(sb, seqs)]
        for v, (b, ls) in zip(sb, seqs):
            y_buf[b * rows + j * cs:b * rows + (j + 1) * cs, ls] = _dot_nt(q_ref[b, rs, ls], _bd2(v)) + y0_ref[b, rs, ls]
        state = [s * dec_ref[b, 8 * j:8 * j + 1, ls] + u + z_ref[b, rs, ls]
                 for s, u, (b, ls) in zip(state, upd, seqs)]
    for s, (b, ls) in zip(state, seqs):
        s_ref[b, :, ls] = s

    ones_bd = jnp.where(_head_mask(MXU_W), 1.0, 0.0).astype(BF16)
    y = y_buf[...]
    dev = y - _head_sum(y, ones_bd, split=False) * (1.0 / HEAD_DIM)
    var = _head_sum(dev * dev, ones_bd, split=False) * (1.0 / HEAD_DIM)
    yn = dev * lax.rsqrt(var + LNX_EPS) * lnw_ref[...] + lnb_ref[...]
    gated = jnp.concatenate([((yn[b * rows:(b + 1) * rows] + bonus_ref[b]) * g_ref[b]).astype(BF16)
                             for b in range(bsz)], axis=0)
    x1 = jnp.concatenate([x_ref[b] for b in range(bsz)], axis=0) + _dot(gated, wo_ref[...])
    out = _ffn_block(x1, gf_ref, wg_ref, wu_ref, wd_ref, acc_ref)
    if final_norm:
        out = _rms(out, gl_ref[...])
    for b in range(bsz):
        out_ref[b] = out[b * rows:(b + 1) * rows]


def _wkv(x, r, lw, k, v, kk, ab, g, rk, lnw, lnb, wo, ffn, g_last, final_norm, prep_chunks=4, scan_chunks=2):
    bsz, s, d = r.shape
    cs = WKV_CHUNK
    rows = prep_chunks * cs
    blk = pl.BlockSpec((None, rows, d), lambda b, c: (b, c, 0))
    dec_blk = pl.BlockSpec((None, 8 * prep_chunks, d), lambda b, c: (b, c, 0))
    act = lambda dt: jax.ShapeDtypeStruct((bsz, s, d), dt)
    q, y0, mc, z, dec, bonus = pl.pallas_call(
        _wkv_prep_kernel,
        grid=(bsz, s // rows),
        in_specs=[blk] * 6 + [pl.BlockSpec((1, d), lambda b, c: (0, 0))],
        out_specs=[blk, blk, blk, blk, dec_blk, blk],
        out_shape=[act(BF16), act(F32), act(BF16), act(F32),
                   jax.ShapeDtypeStruct((bsz, 8 * s // cs, d), F32), act(F32)],
        compiler_params=_cparams("parallel", "parallel"),
        name="wkv7_prep",
    )(r, lw, k, v, kk, ab, rk.reshape(1, d))
    rows = scan_chunks * cs
    blk = pl.BlockSpec((bsz, rows, d), lambda c: (0, c, 0))
    dec_blk = pl.BlockSpec((bsz, 8 * scan_chunks, d), lambda c: (0, c, 0))
    row = pl.BlockSpec((1, d), lambda c: (0, 0))
    return pl.pallas_call(
        functools.partial(_wkv_out_kernel, final_norm),
        grid=(s // rows,),
        in_specs=[blk, blk, blk, blk, dec_blk, blk, blk, blk, row, row]
                 + [_resident(a.shape) for a in (wo,) + ffn] + [row],
        out_specs=blk,
        out_shape=act(F32),
        scratch_shapes=[pltpu.VMEM((bsz, HEAD_DIM, d), F32), pltpu.VMEM((bsz * rows, d), F32),
                        pltpu.VMEM((bsz * rows, d), F32)],
        compiler_params=_cparams("arbitrary"),
        name="wkv7_out_ffn",
    )(q, y0, mc, z, dec, x, bonus, g, lnw.reshape(1, d), lnb.reshape(1, d), wo, *ffn, g_last.reshape(1, d))


def kernel(x, positions, norm_mix, norm_ffn, norm_final, attn_w_in, attn_b_in, attn_sinks, attn_w_out, rwkv_mu, rwkv_w_rkv, rwkv_w0, rwkv_w1, rwkv_w2, rwkv_a0, rwkv_a1, rwkv_a2, rwkv_g1, rwkv_g2, rwkv_k_k, rwkv_k_a, rwkv_r_k, rwkv_lnx_w, rwkv_lnx_b, rwkv_w_o, rwkv_v0, rwkv_v1, rwkv_v2, ffn_w_gate, ffn_w_up, ffn_w_down):
    depth = norm_mix.shape[0]
    assert depth % 2 == 0, "the final norm is fused into the last RWKV layer"
    assert all(win // dil == BLOCK for win, dil in B_PATTERNS), "mixer B bands must span one block"
    bf = lambda a: a.astype(BF16)
    tables = _rope_tables(positions)
    v_first = None
    for layer in range(depth):
        i = layer // 2
        ffn = (norm_ffn[layer].reshape(1, -1), bf(ffn_w_gate[layer]), bf(ffn_w_up[layer]), bf(ffn_w_down[layer]))
        if layer % 2 == 0:
            qa, kva, qkv_b = _attn_in(x, norm_mix[layer], bf(attn_w_in[i]), attn_b_in[i], tables)
            oa = _attn_a(qa, kva, attn_sinks[i])
            obs, lses = zip(*[_attn_b(*qkv_b[pat], pat, dil) for pat, (_, dil) in enumerate(B_PATTERNS)])
            x = _attn_out(x, oa, obs, lses, bf(attn_w_out[i]), ffn)
        else:
            p = dict(mu=rwkv_mu[i], w_r=bf(rwkv_w_rkv[i, 0]), w_k=bf(rwkv_w_rkv[i, 1]), w_v=bf(rwkv_w_rkv[i, 2]),
                     w0=rwkv_w0[i], w1=bf(rwkv_w1[i]), w2=bf(rwkv_w2[i]),
                     a0=rwkv_a0[i], a1=bf(rwkv_a1[i]), a2=bf(rwkv_a2[i]), g1=bf(rwkv_g1[i]), g2=bf(rwkv_g2[i]),
                     k_k=rwkv_k_k[i], k_a=rwkv_k_a[i])
            v_lora = None if i == 0 else (rwkv_v0[i - 1], bf(rwkv_v1[i - 1]), bf(rwkv_v2[i - 1]))
            r, lw, k, v, kk, ab, g = _rwkv_in(x, norm_mix[layer], p, v_lora, v_first)
            if i == 0:
                v_first = v
            x = _wkv(x, r, lw, k, v, kk, ab, g, rwkv_r_k[i], rwkv_lnx_w[i], rwkv_lnx_b[i], bf(rwkv_w_o[i]),
                     ffn, norm_final, final_norm=(layer == depth - 1))
    return x
```

```python
import functools

import jax
import jax.numpy as jnp
from jax import lax
from jax.experimental import pallas as pl
from jax.experimental.pallas import tpu as pltpu

F32 = jnp.float32
BF16 = jnp.bfloat16

HEAD_DIM = 64
ROT_DIM = HEAD_DIM // 4
ROT_HALF = ROT_DIM // 2
ROPE_THETA = 500000.0
BLOCK = 128
NORM_EPS = 1e-5
LNX_EPS = 64e-5

A_Q_HEADS = 12
A_KV_HEADS = 3
A_GROUP = A_Q_HEADS // A_KV_HEADS
A_WINDOW = 128
B_PATTERNS = ((128, 1), (512, 4), (2048, 16))
B_HEADS = 4

A_Q_W = A_Q_HEADS * HEAD_DIM
A_KV_W = A_KV_HEADS * HEAD_DIM
B_PW = B_HEADS * HEAD_DIM
B_W = len(B_PATTERNS) * B_PW
KV_DUP_W = 2 * A_KV_W
OFF_KA = A_Q_W
OFF_QB = OFF_KA + 2 * A_KV_W
OFF_KB = OFF_QB + B_W
OFF_VB = OFF_KB + B_W

LANES = 128
MXU_W = 256
WKV_CHUNK = 64
DECAY_SCALE = 0.6065306597126334
NEG_BIG = -1e30
VMEM_LIMIT = 56 * 1024 * 1024


def _cparams(*sem):
    return pltpu.CompilerParams(dimension_semantics=sem, vmem_limit_bytes=VMEM_LIMIT)


def _dot(a, b):
    return jnp.dot(a, b, preferred_element_type=F32)


def _dot_nt(a, b):
    return lax.dot_general(a, b, (((1,), (1,)), ((), ())), preferred_element_type=F32)


def _dot_tn(a, b):
    return lax.dot_general(a, b, (((0,), (0,)), ((), ())), preferred_element_type=F32)


def _row_sum(z):
    part = z[:, :LANES]
    for j in range(1, z.shape[1] // LANES):
        part = part + z[:, j * LANES:(j + 1) * LANES]
    ones = jnp.ones((LANES, LANES), BF16)
    hi = part.astype(BF16)
    lo = (part - hi.astype(F32)).astype(BF16)
    return _dot(hi, ones) + _dot(lo, ones)


def _rms(x, g):
    inv = lax.rsqrt(_row_sum(x * x) * (1.0 / x.shape[1]) + NORM_EPS)
    return x * jnp.concatenate([inv] * (x.shape[1] // LANES), axis=1) * g


def _sigmoid(z):
    return 1.0 / (1.0 + jnp.exp(-z))


def _head_mask(n):
    r = lax.broadcasted_iota(jnp.int32, (n, n), 0) // HEAD_DIM
    c = lax.broadcasted_iota(jnp.int32, (n, n), 1) // HEAD_DIM
    return r == c


def _head_sum(z, ones_bd, split=True):
    rows, n = z.shape[0], z.shape[1] // MXU_W
    zc = jnp.concatenate([z[:, j * MXU_W:(j + 1) * MXU_W] for j in range(n)], axis=0)
    hi = zc.astype(BF16)
    acc = _dot(hi, ones_bd)
    if split:
        acc = acc + _dot((zc - hi.astype(F32)).astype(BF16), ones_bd)
    return jnp.concatenate([acc[j * rows:(j + 1) * rows] for j in range(n)], axis=1)


def _rope_trig_kernel(pos_ref, invf_ref, cos_ref, sin_ref):
    ang = pos_ref[...].astype(F32)[None] * invf_ref[...]
    cos_ref[...] = jnp.cos(ang)
    sin_ref[...] = jnp.sin(ang)


def _rope_tables(positions):
    bsz, s = positions.shape
    rows = bsz * s // LANES
    inv_freq = jnp.power(ROPE_THETA, -2.0 * jnp.arange(ROT_HALF, dtype=F32) / ROT_DIM)
    invf = jnp.broadcast_to(inv_freq[:, None, None], (ROT_HALF, 1, LANES))
    cos, sin = pl.pallas_call(
        _rope_trig_kernel,
        out_shape=(jax.ShapeDtypeStruct((ROT_HALF, rows, LANES), F32),) * 2,
        name="rope_trig",
    )(positions.reshape(rows, LANES), invf)
    cos = jnp.tile(cos.reshape(ROT_HALF, bsz, s).transpose(1, 2, 0), (1, 1, LANES // ROT_HALF))
    sin = jnp.tile(sin.reshape(ROT_HALF, bsz, s).transpose(1, 2, 0), (1, 1, LANES // ROT_HALF))
    dim = jnp.arange(LANES) % HEAD_DIM
    c = jnp.where(dim < ROT_DIM, cos, 1.0)
    sa = jnp.where(dim < ROT_HALF, -sin, 0.0)
    sb = jnp.where((dim >= ROT_HALF) & (dim < ROT_DIM), sin, 0.0)
    return c, sa, sb


def _residue_major(ref, dil, groups=1):
    span = ref.shape[0] // groups
    n = span // dil
    return jnp.concatenate([ref[pl.ds(g * span + r, n, stride=dil), :]
                            for r in range(dil) for g in range(groups)], axis=0)


def _attn_in_kernel(x_ref, g_ref, w_ref, b_ref, c_ref, sa_ref, sb_ref, qa_ref, kva_ref, *rest):
    b_refs, hbuf = rest[:-1], rest[-1]
    hf = _rms(x_ref[...], g_ref[...])
    tm, d = hf.shape
    tiles = d // LANES
    for j in range(tiles):
        hbuf[j] = hf[:, j * LANES:(j + 1) * LANES]
    scale = HEAD_DIM ** -0.5

    def proj(h, lo, width):
        return _dot(h, w_ref[:, lo:lo + width]) + b_ref[:, lo:lo + width]

    def rope(z, tabs):
        c, sa, sb = tabs
        return z * c + pltpu.roll(z, LANES - ROT_HALF, 1) * sa + pltpu.roll(z, ROT_HALF, 1) * sb

    def rope_all(z, tabs):
        return jnp.concatenate([rope(z[:, j * LANES:(j + 1) * LANES], tabs)
                                for j in range(z.shape[1] // LANES)], axis=1)

    h = hf.astype(BF16)
    tabs = (c_ref[...], sa_ref[...], sb_ref[...])
    for j in range(A_Q_W // MXU_W):
        qa_ref[:, j * MXU_W:(j + 1) * MXU_W] = (rope_all(proj(h, j * MXU_W, MXU_W), tabs) * scale).astype(BF16)
    z = proj(h, OFF_KA, 2 * A_KV_W)
    low = lax.broadcasted_iota(jnp.int32, (tm, LANES), 1) < HEAD_DIM
    mid = z[:, LANES:2 * LANES]
    src = [rope(z[:, :LANES], tabs), jnp.where(low, rope(mid, tabs), mid), z[:, 2 * LANES:]]
    for j, t in enumerate(src):
        swapped = pltpu.roll(t, HEAD_DIM, 1)
        kva_ref[:, 2 * j * LANES:(2 * j + 1) * LANES] = jnp.where(low, t, swapped).astype(BF16)
        kva_ref[:, (2 * j + 1) * LANES:(2 * j + 2) * LANES] = jnp.where(low, swapped, t).astype(BF16)
    prev_dil = 1
    for pat, (_, dil) in enumerate(B_PATTERNS):
        q_ref, k_ref, v_ref = b_refs[3 * pat:3 * pat + 3]
        if dil > 1:
            step = dil // prev_dil
            srcs = [hbuf.at[j] for j in range(tiles)] + [hbuf.at[tiles + j] for j in range(3)]
            if prev_dil == 1:
                srcs[tiles:] = [c_ref, sa_ref, sb_ref]
            regrouped = [_residue_major(src, step, prev_dil) for src in srcs]
            for j, val in enumerate(regrouped):
                hbuf[j] = val
            h = jnp.concatenate(regrouped[:tiles], axis=1).astype(BF16)
            tabs = tuple(regrouped[tiles:])
            prev_dil = dil
        q = (rope_all(proj(h, OFF_QB + pat * B_PW, B_PW), tabs) * scale).astype(BF16)
        k = rope_all(proj(h, OFF_KB + pat * B_PW, B_PW), tabs).astype(BF16)
        v = proj(h, OFF_VB + pat * B_PW, B_PW).astype(BF16)
        n = tm // dil
        for r in range(dil):
            q_ref[:, r * B_PW:(r + 1) * B_PW] = q[r * n:(r + 1) * n]
            k_ref[:, r * B_PW:(r + 1) * B_PW] = k[r * n:(r + 1) * n]
            v_ref[:, r * B_PW:(r + 1) * B_PW] = v[r * n:(r + 1) * n]


def _attn_in(x, g, w_in, b_in, tables, tm=1024):
    bsz, s, d = x.shape
    in_w = b_in.shape[-1]
    tok = lambda w: pl.BlockSpec((None, tm, w), lambda b, i: (b, i, 0))
    full = _resident
    out_specs = [tok(A_Q_W), tok(2 * KV_DUP_W)]
    out_shape = [jax.ShapeDtypeStruct((bsz, s, A_Q_W), BF16), jax.ShapeDtypeStruct((bsz, s, 2 * KV_DUP_W), BF16)]
    for _, dil in B_PATTERNS:
        out_specs += [pl.BlockSpec((None, tm // dil, dil * B_PW), lambda b, i: (b, i, 0))] * 3
        out_shape += [jax.ShapeDtypeStruct((bsz, s // dil, dil * B_PW), BF16)] * 3
    outs = pl.pallas_call(
        _attn_in_kernel,
        grid=(bsz, s // tm),
        in_specs=[tok(d), full((1, d)), _weight_spec(w_in), full((1, in_w)),
                  tok(LANES), tok(LANES), tok(LANES)],
        out_specs=out_specs,
        out_shape=out_shape,
        scratch_shapes=[pltpu.VMEM((d // LANES + 3, tm, LANES), F32)],
        compiler_params=_cparams("parallel", "parallel"),
        name="attn_in",
    )(x, g.reshape(1, d), _weight_arg(w_in), b_in.reshape(1, in_w), *tables)
    return outs[0], outs[1], [outs[2 + 3 * p:5 + 3 * p] for p in range(len(B_PATTERNS))]


def _band_bias(rows, max_dist, first_tile):
    qi = lax.broadcasted_iota(jnp.int32, (rows, 2 * BLOCK), 0) % BLOCK
    kj = lax.broadcasted_iota(jnp.int32, (rows, 2 * BLOCK), 1)
    dist = BLOCK + qi - kj
    band = (dist >= 0) & (dist <= max_dist)
    inner = jnp.where(band, 0.0, NEG_BIG)
    return jnp.where(band & ((kj >= BLOCK) | jnp.logical_not(first_tile)), 0.0, NEG_BIG), inner


def _window_rows(prev_ref, cur_ref, jb, lanes):
    if jb == 0:
        return jnp.concatenate([prev_ref[:, lanes], cur_ref[0:BLOCK, lanes]], axis=0)
    return cur_ref[(jb - 1) * BLOCK:(jb + 1) * BLOCK, lanes]


def _attn_a_kernel(sink_ref, q_ref, kvp_ref, kvc_ref, o_ref):
    nb = q_ref.shape[0] // BLOCK
    rows = A_GROUP * BLOCK
    first_tile = pl.program_id(1) == 0
    qi = lax.broadcasted_iota(jnp.int32, (rows, BLOCK), 0) % BLOCK
    from_prev = lax.broadcasted_iota(jnp.int32, (rows, BLOCK), 1) > qi
    low = lax.broadcasted_iota(jnp.int32, (BLOCK, LANES), 1) < HEAD_DIM
    rowg = lax.broadcasted_iota(jnp.int32, (rows, 1), 0) // BLOCK
    sinks = []
    for hk in range(A_KV_HEADS):
        sk = jnp.zeros((rows, 1), F32)
        for g in range(A_GROUP):
            sk = jnp.where(rowg == g, sink_ref[hk * A_GROUP + g], sk)
        sinks.append(sk)
    kvh = range(A_KV_HEADS)
    for jb in range(nb):
        rs = slice(jb * BLOCK, (jb + 1) * BLOCK)
        sc = []
        for hk in kvh:
            tiles = [q_ref[rs, (2 * hk + t) * LANES:(2 * hk + t + 1) * LANES] for t in range(2)]
            zero = jnp.zeros_like(tiles[0])
            q4 = jnp.concatenate([jnp.where(low, tiles[0], zero), jnp.where(low, zero, tiles[0]),
                                  jnp.where(low, tiles[1], zero), jnp.where(low, zero, tiles[1])], axis=0)
            k = _window_rows(kvp_ref, kvc_ref, jb, slice(hk * LANES, (hk + 1) * LANES))
            both = _dot_nt(q4, k)
            prev = both[:, :BLOCK]
            if jb == 0:
                prev = jnp.where(first_tile, NEG_BIG, prev)
            sc.append(jnp.where(from_prev, prev, both[:, BLOCK:]))
        m = [jnp.max(s, axis=-1, keepdims=True) for s in sc]
        p = [jnp.exp(s - mm) for s, mm in zip(sc, m)]
        den = [jnp.sum(pp, axis=-1, keepdims=True) + jnp.exp(sk - mm) for pp, sk, mm in zip(p, sinks, m)]
        o = []
        for hk in kvh:
            v = _window_rows(kvp_ref, kvc_ref, jb, slice((A_KV_HEADS + hk) * LANES, (A_KV_HEADS + hk + 1) * LANES))
            pb = p[hk].astype(BF16)
            zero = jnp.zeros_like(pb)
            unfolded = jnp.concatenate([jnp.where(from_prev, pb, zero), jnp.where(from_prev, zero, pb)], axis=1)
            o.append(_dot(unfolded, v) * (1.0 / den[hk]))
        for hk in kvh:
            for t in range(2):
                pair = jnp.where(low, o[hk][2 * t * BLOCK:(2 * t + 1) * BLOCK], o[hk][(2 * t + 1) * BLOCK:(2 * t + 2) * BLOCK])
                o_ref[rs, (2 * hk + t) * LANES:(2 * hk + t + 1) * LANES] = pair.astype(BF16)


def _attn_a(qa, kva, sinks, tile=1024):
    assert A_WINDOW == BLOCK, "the score fold needs a window of exactly one block"
    bsz, s, _ = qa.shape
    tile = min(tile, s)
    kvw = kva.shape[2]
    per = tile // BLOCK
    return pl.pallas_call(
        _attn_a_kernel,
        grid=(bsz, s // tile),
        in_specs=[pl.BlockSpec(memory_space=pltpu.SMEM),
                  pl.BlockSpec((None, tile, A_Q_W), lambda b, i: (b, i, 0)),
                  pl.BlockSpec((None, BLOCK, kvw), lambda b, i: (b, jnp.maximum(i * per - 1, 0), 0)),
                  pl.BlockSpec((None, tile, kvw), lambda b, i: (b, i, 0))],
        out_specs=pl.BlockSpec((None, tile, A_Q_W), lambda b, i: (b, i, 0)),
        out_shape=jax.ShapeDtypeStruct((bsz, s, A_Q_W), BF16),
        compiler_params=_cparams("parallel", "parallel"),
        name="attn_a",
    )(sinks, qa, kva, kva)


def _attn_b_kernel(group, q_ref, kp_ref, kc_ref, vp_ref, vc_ref, o_ref, l_ref):
    nb = q_ref.shape[0] // BLOCK
    bias_first, bias_inner = _band_bias(BLOCK, BLOCK, pl.program_id(2) == 0)
    low = lax.broadcasted_iota(jnp.int32, (BLOCK, LANES), 1) < HEAD_DIM
    work = [(res, jb) for res in range(q_ref.shape[1] // B_PW) for jb in range(nb)]
    for w0 in range(0, len(work), group):
        items = [(res, jb, h) for res, jb in work[w0:w0 + group] for h in range(B_HEADS)]
        lanes = [slice(res * B_PW + (h // 2) * LANES, res * B_PW + (h // 2 + 1) * LANES) for res, _, h in items]
        sc = []
        for (_, jb, h), ls in zip(items, lanes):
            qt = q_ref[jb * BLOCK:(jb + 1) * BLOCK, ls]
            zero = jnp.zeros_like(qt)
            qh = jnp.where(low, qt, zero) if h % 2 == 0 else jnp.where(low, zero, qt)
            sc.append(_dot_nt(qh, _window_rows(kp_ref, kc_ref, jb, ls)) + (bias_first if jb == 0 else bias_inner))
        m = [jnp.max(s, axis=-1, keepdims=True) for s in sc]
        p = [jnp.exp(s - mm) for s, mm in zip(sc, m)]
        den = [jnp.sum(pp, axis=-1, keepdims=True) for pp in p]
        o = [_dot(pp.astype(BF16), _window_rows(vp_ref, vc_ref, jb, ls)) * (1.0 / dd)
             for pp, dd, (_, jb, _), ls in zip(p, den, items, lanes)]
        lse = [mm + jnp.log(dd) for mm, dd in zip(m, den)]
        for idx in range(0, len(items), 2):
            _, jb, _ = items[idx]
            dst = (slice(jb * BLOCK, (jb + 1) * BLOCK), lanes[idx])
            o_ref[dst] = jnp.where(low, o[idx], o[idx + 1])
            l_ref[dst] = jnp.where(low, lse[idx], lse[idx + 1])


def _attn_b(q, k, v, pat, dil, tile=1024, group=2):
    bsz, length, _ = q.shape
    res = max(1, min(tile // length, dil))
    tile = min(tile, length)
    per = tile // BLOCK
    cur = pl.BlockSpec((None, tile, res * B_PW), lambda b, r, i: (b, i, r))
    prev = pl.BlockSpec((None, BLOCK, res * B_PW), lambda b, r, i: (b, jnp.maximum(i * per - 1, 0), r))
    return pl.pallas_call(
        functools.partial(_attn_b_kernel, group),
        grid=(bsz, dil // res, length // tile),
        in_specs=[cur, prev, cur, prev, cur],
        out_specs=[cur, cur],
        out_shape=[jax.ShapeDtypeStruct((bsz, length, dil * B_PW), F32)] * 2,
        compiler_params=_cparams("parallel", "parallel", "parallel"),
        name=f"attn_b{pat}",
    )(q, k, k, v, v)


def _attn_out_kernel(x_ref, oa_ref, o0_ref, o1_ref, o2_ref, l0_ref, l1_ref, l2_ref, w_ref,
                     gf_ref, wg_ref, wu_ref, wd_ref, out_ref, buf, acc_ref):
    tm = x_ref.shape[0]

    def natural(ref, dil, slot):
        if dil == 1:
            return ref[...]
        n = tm // dil
        halves = B_PW // LANES
        for r in range(dil):
            for j in range(halves):
                buf[slot * halves + j, pl.ds(r, n, stride=dil), :] = ref[:, r * B_PW + j * LANES:r * B_PW + (j + 1) * LANES]
        return jnp.concatenate([buf[slot * halves + j] for j in range(halves)], axis=1)

    dils = [dil for _, dil in B_PATTERNS]
    o = [natural(ref, dil, i) for i, (ref, dil) in enumerate(zip((o0_ref, o1_ref, o2_ref), dils))]
    l = [natural(ref, dil, 3 + i) for i, (ref, dil) in enumerate(zip((l0_ref, l1_ref, l2_ref), dils))]
    m = jnp.maximum(jnp.maximum(l[0], l[1]), l[2])
    e = [jnp.exp(v - m) for v in l]
    ob = (e[0] * o[0] + e[1] * o[1] + e[2] * o[2]) / (e[0] + e[1] + e[2])
    mix = _dot(oa_ref[...], w_ref[:A_Q_W, :]) + _dot(ob.astype(BF16), w_ref[A_Q_W:, :])
    out_ref[...] = _ffn_block(x_ref[...] + mix, gf_ref, wg_ref, wu_ref, wd_ref, acc_ref)


def _attn_out(x, oa, obs, lses, w_out, ffn, tm=512):
    bsz, s, d = x.shape
    tok = lambda w: pl.BlockSpec((None, tm, w), lambda b, i: (b, i, 0))
    pat = [pl.BlockSpec((None, tm // dil, dil * B_PW), lambda b, i: (b, i, 0)) for _, dil in B_PATTERNS]
    return pl.pallas_call(
        _attn_out_kernel,
        grid=(bsz, s // tm),
        in_specs=[tok(d), tok(A_Q_W)] + pat + pat + [_weight_spec(a) for a in (w_out,) + ffn],
        out_specs=tok(d),
        out_shape=jax.ShapeDtypeStruct((bsz, s, d), F32),
        scratch_shapes=[pltpu.VMEM((2 * len(B_PATTERNS) * (B_PW // LANES), tm, LANES), F32),
                        pltpu.VMEM((tm, d), F32)],
        compiler_params=_cparams("parallel", "parallel"),
        name="attn_out_ffn",
    )(x, oa, *obs, *lses, *map(_weight_arg, (w_out,) + ffn))


FF_CHUNK = 256


def _ffn_block(x, g_ref, wg_ref, wu_ref, wd_ref, acc_ref):
    h = _rms(x, g_ref[...]).astype(BF16)
    for c in range(wg_ref.shape[1] // FF_CHUNK):
        sl = slice(c * FF_CHUNK, (c + 1) * FF_CHUNK)
        gate = _dot(h, wg_ref[:, sl])
        up = _dot(h, wu_ref[:, sl])
        contrib = _dot((gate * _sigmoid(gate) * up).astype(BF16), wd_ref[sl, :])
        if c == 0:
            acc_ref[...] = x + contrib
        else:
            acc_ref[...] += contrib
    return acc_ref[...]


def _resident(shape):
    return pl.BlockSpec(shape, lambda *_: (0,) * len(shape), pipeline_mode=pl.Buffered(1))


def _weight_spec(w):
    if not isinstance(w, tuple):
        return _resident(w.shape)
    stack, idx = w
    rest = stack.shape[len(idx):]
    return pl.BlockSpec((None,) * len(idx) + rest, lambda *_: idx + (0,) * len(rest), pipeline_mode=pl.Buffered(1))


def _weight_arg(w):
    return w[0] if isinstance(w, tuple) else w


def _rwkv_in_kernel(has_vlora, *refs):
    (x_ref, gn_ref, mu_ref, wr_ref, wk_ref, wv_ref, w0_ref, w1_ref, w2_ref, a0_ref, a1_ref, a2_ref,
     g1_ref, g2_ref, kk_ref, ka_ref) = refs[:16]
    refs = refs[16:]
    if has_vlora:
        v0_ref, v1_ref, v2_ref, vf_ref = refs[:4]
        refs = refs[4:]
    r_o, lw_o, k_o, v_o, kk_o, ab_o, g_o, carry, mix = refs
    tm, d = x_ref.shape

    @pl.when(pl.program_id(1) == 0)
    def _():
        carry[...] = jnp.zeros(carry.shape, F32)

    h = _rms(x_ref[...], gn_ref[...])
    rolled = pltpu.roll(h, 1, 0)
    first = lax.broadcasted_iota(jnp.int32, (8, 1), 0) == 0
    hprev = jnp.concatenate([jnp.where(first, carry[7:8, :], rolled[0:8]), rolled[8:]], axis=0)
    carry[...] = h[tm - 8:tm, :]
    hb = h.astype(BF16)
    xxb = (hprev - h).astype(BF16)
    mub = mu_ref[...].astype(BF16)
    for i in range(mix.shape[0]):
        mix[i] = hb + xxb * mub[i:i + 1, :]

    mid_w = jnp.tanh(_dot(mix[1], w1_ref[...])).astype(BF16)
    mid_a = _dot(mix[4], a1_ref[...]).astype(BF16)
    mid_g = _sigmoid(_dot(mix[5], g1_ref[...])).astype(BF16)
    if has_vlora:
        mid_v = _dot(mix[3], v1_ref[...]).astype(BF16)

    ones_bd = jnp.where(_head_mask(MXU_W), 1.0, 0.0).astype(BF16)
    for c in range(d // MXU_W):
        cs = slice(c * MXU_W, (c + 1) * MXU_W)
        r_o[:, cs] = _dot(mix[0], wr_ref[:, cs])
        lw_o[:, cs] = -DECAY_SCALE * _sigmoid(w0_ref[:, cs] + _dot(mid_w, w2_ref[:, cs]))
        v = _dot(mix[3], wv_ref[:, cs])
        if has_vlora:
            v = v + (vf_ref[:, cs] - v) * _sigmoid(v0_ref[:, cs] + _dot(mid_v, v2_ref[:, cs]))
        v_o[:, cs] = v
        g_o[:, cs] = _dot(mid_g, g2_ref[:, cs]).astype(BF16)
        k = _dot(mix[2], wk_ref[:, cs])
        a = _sigmoid(a0_ref[:, cs] + _dot(mid_a, a2_ref[:, cs]))
        kk = k * kk_ref[:, cs]
        kk = kk / jnp.maximum(jnp.sqrt(_head_sum(kk * kk, ones_bd)), 1e-12)
        kk_o[:, cs] = kk
        ab_o[:, cs] = kk * a
        k_o[:, cs] = k * (1.0 + (a - 1.0) * ka_ref[:, cs])


def _rwkv_in(x, gn, p, v_lora, v_first, tm=512):
    bsz, s, d = x.shape
    tok = pl.BlockSpec((None, tm, d), lambda b, i: (b, i, 0))
    row = lambda a: a.reshape(1, d)
    args = [x, row(gn), p["mu"], p["w_r"], p["w_k"], p["w_v"], row(p["w0"]), p["w1"], p["w2"], row(p["a0"]), p["a1"], p["a2"],
            p["g1"], p["g2"], row(p["k_k"]), row(p["k_a"])]
    specs = [tok] + [_weight_spec(a) for a in args[1:]]
    if v_lora is not None:
        v0, v1, v2 = v_lora
        extra = [row(v0), v1, v2]
        args += extra + [v_first]
        specs += [_weight_spec(a) for a in extra] + [tok]
    return pl.pallas_call(
        functools.partial(_rwkv_in_kernel, v_lora is not None),
        grid=(bsz, s // tm),
        in_specs=specs,
        out_specs=[tok] * 7,
        out_shape=[jax.ShapeDtypeStruct((bsz, s, d), F32)] * 6 + [jax.ShapeDtypeStruct((bsz, s, d), BF16)],
        scratch_shapes=[pltpu.VMEM((8, d), F32), pltpu.VMEM((p["mu"].shape[0], tm, d), BF16)],
        compiler_params=_cparams("parallel", "arbitrary"),
        name="rwkv_in",
    )(*map(_weight_arg, args))


def _bd2(z):
    lane = lax.broadcasted_iota(jnp.int32, z.shape, 1)
    zero = jnp.zeros(z.shape, z.dtype)
    return jnp.concatenate([jnp.where(lane < HEAD_DIM, z, zero), jnp.where(lane >= HEAD_DIM, z, zero)], axis=0)


def _fold2(full):
    lane = lax.broadcasted_iota(jnp.int32, (HEAD_DIM, LANES), 1)
    return jnp.where(lane < HEAD_DIM, full[:HEAD_DIM], full[HEAD_DIM:])


def _wkv_prep_kernel(r_ref, lw_ref, k_ref, v_ref, kk_ref, ab_ref, rk_ref,
                     q_ref, y0_ref, mc_ref, z_ref, dec_ref, bonus_ref):
    cs = WKV_CHUNK
    rows, d = r_ref.shape
    nc = rows // cs
    ri = lax.broadcasted_iota(jnp.int32, (rows, rows), 0)
    ci = lax.broadcasted_iota(jnp.int32, (rows, rows), 1)
    tri = jnp.where((ri >= ci) & (ri // cs == ci // cs), 1.0, 0.0).astype(BF16)
    lw = lw_ref[...]
    hi = lw.astype(BF16)
    lo = (lw - hi.astype(F32)).astype(BF16)
    cum = _dot(tri, hi) + _dot(tri, lo)
    totals = [cum[(j + 1) * cs - 1:(j + 1) * cs, :] for j in range(nc)]
    tot = jnp.concatenate([jnp.broadcast_to(t, (cs, d)) for t in totals], axis=0)
    e_neg = jnp.exp(-cum)
    e_tot = jnp.exp(tot - cum)
    kk, ab, kx = kk_ref[...], ab_ref[...], k_ref[...]
    at = (-kk * jnp.exp(cum - lw)).astype(BF16)
    rt = r_ref[...] * jnp.exp(cum)
    rtb = rt.astype(BF16)
    bt = (ab * e_neg).astype(BF16)
    kt = (kx * e_neg).astype(BF16)
    bh = (ab * e_tot).astype(BF16)
    kh = (kx * e_tot).astype(BF16)
    vb = v_ref[...].astype(BF16)
    ones_bd = jnp.where(_head_mask(MXU_W), 1.0, 0.0).astype(BF16)
    bonus_ref[...] = _head_sum(r_ref[...] * kx * rk_ref[...], ones_bd, split=False) * v_ref[...]

    trow = lax.broadcasted_iota(jnp.int32, (cs, LANES), 0)
    tcol = lax.broadcasted_iota(jnp.int32, (cs, LANES), 1) % HEAD_DIM
    strict = trow > tcol
    incl = trow >= tcol
    eye = jnp.where(trow == tcol, 1.0, 0.0)

    for j in range(nc):
        dec_ref[8 * j:8 * j + 8, :] = jnp.broadcast_to(jnp.exp(totals[j]), (8, d))

    chains = [(slice(j * cs, (j + 1) * cs), slice(p * LANES, (p + 1) * LANES))
              for j in range(nc) for p in range(d // LANES)]

    g12 = [_dot_nt(jnp.concatenate([at[c], rtb[c]], axis=0),
                   jnp.concatenate([_bd2(bt[c]), _bd2(kt[c])], axis=0)) for c in chains]
    a_ab = [jnp.where(strict, g[:cs, :LANES], 0.0) for g in g12]
    a_rb = [jnp.where(incl, g[cs:, :LANES], 0.0).astype(BF16) for g in g12]
    a_k = [jnp.concatenate([jnp.where(strict, g[:cs, LANES:], 0.0),
                            jnp.where(incl, g[cs:, LANES:], 0.0)], axis=0).astype(BF16) for g in g12]
    g4 = [_dot(a, _bd2(vb[c])) for a, c in zip(a_k, chains)]
    x = [eye + jnp.where((trow - tcol == 1) & (trow % 2 == 1), a, 0.0) for a in a_ab]
    size = 2
    while size < cs:
        lower_left = (trow // size - tcol // size == 1) & ((trow // size) % 2 == 1)
        xb = [v.astype(BF16) for v in x]
        xn = [_dot(v, _bd2(jnp.where(lower_left, a, 0.0).astype(BF16))) for v, a in zip(xb, a_ab)]
        x = [v + _dot(n.astype(BF16), _bd2(vb16)) for v, n, vb16 in zip(x, xn, xb)]
        size *= 2
    wu = [_dot(v.astype(BF16), jnp.concatenate([_bd2(at[c]), _bd2(g[:cs].astype(BF16))], axis=1))
          for v, g, c in zip(x, g4, chains)]
    wb = [v[:, :LANES].astype(BF16) for v in wu]
    ub = [v[:, LANES:].astype(BF16) for v in wu]
    qy = [_dot(a, jnp.concatenate([_bd2(w), _bd2(u)], axis=1)) for a, w, u in zip(a_rb, wb, ub)]
    mc = [_dot_tn(w, bh[c]) for w, c in zip(wb, chains)]
    zz = [_dot_tn(jnp.concatenate([u, vb[c]], axis=0), jnp.concatenate([bh[c], kh[c]], axis=0))
          for u, c in zip(ub, chains)]
    for i, c in enumerate(chains):
        q_ref[c] = (rt[c] + qy[i][:, :LANES]).astype(BF16)
        y0_ref[c] = g4[i][cs:] + qy[i][:, LANES:]
        mc_ref[c] = _fold2(mc[i]).astype(BF16)
        z_ref[c] = _fold2(zz[i])


def _wkv_out_kernel(final_norm, q_ref, y0_ref, mc_ref, z_ref, dec_ref, x_ref, bonus_ref, g_ref, lnw_ref, lnb_ref,
                    wo_ref, gf_ref, wg_ref, wu_ref, wd_ref, gl_ref, out_ref, s_ref, y_buf, acc_ref):
    cs = WKV_CHUNK
    bsz, rows, d = q_ref.shape

    @pl.when(pl.program_id(0) == 0)
    def _():
        s_ref[...] = jnp.zeros(s_ref.shape, F32)

    seqs = [(b, slice(p * LANES, (p + 1) * LANES)) for b in range(bsz) for p in range(d // LANES)]
    state = [s_ref[b, :, ls] for b, ls in seqs]
    for j in range(rows // cs):
        rs = slice(j * cs, (j + 1) * cs)
        sb = [s.astype(BF16) for s in state]
        upd = [_dot(v, _bd2(mc_ref[b, rs, ls])) for v, (b, ls) in zip(sb, seqs)]
        for v, (b, ls) in zip(sb, seqs):
            y_buf[b * rows + j * cs:b * rows + (j + 1) * cs, ls] = _dot_nt(q_ref[b, rs, ls], _bd2(v)) + y0_ref[b, rs, ls]
        state = [s * dec_ref[b, 8 * j:8 * j + 1, ls] + u + z_ref[b, rs, ls]
                 for s, u, (b, ls) in zip(state, upd, seqs)]
    for s, (b, ls) in zip(state, seqs):
        s_ref[b, :, ls] = s

    ones_bd = jnp.where(_head_mask(MXU_W), 1.0, 0.0).astype(BF16)
    y = y_buf[...]
    dev = y - _head_sum(y, ones_bd, split=False) * (1.0 / HEAD_DIM)
    var = _head_sum(dev * dev, ones_bd, split=False) * (1.0 / HEAD_DIM)
    yn = dev * lax.rsqrt(var + LNX_EPS) * lnw_ref[...] + lnb_ref[...]
    gated = jnp.concatenate([((yn[b * rows:(b + 1) * rows] + bonus_ref[b]) * g_ref[b]).astype(BF16)
                             for b in range(bsz)], axis=0)
    x1 = jnp.concatenate([x_ref[b] for b in range(bsz)], axis=0) + _dot(gated, wo_ref[...])
    out = _ffn_block(x1, gf_ref, wg_ref, wu_ref, wd_ref, acc_ref)
    if final_norm:
        out = _rms(out, gl_ref[...])
    for b in range(bsz):
        out_ref[b] = out[b * rows:(b + 1) * rows]


def _wkv(x, r, lw, k, v, kk, ab, g, rk, lnw, lnb, wo, ffn, g_last, final_norm, prep_chunks=4, scan_chunks=2):
    bsz, s, d = r.shape
    cs = WKV_CHUNK
    rows = prep_chunks * cs
    blk = pl.BlockSpec((None, rows, d), lambda b, c: (b, c, 0))
    dec_blk = pl.BlockSpec((None, 8 * prep_chunks, d), lambda b, c: (b, c, 0))
    act = lambda dt: jax.ShapeDtypeStruct((bsz, s, d), dt)
    q, y0, mc, z, dec, bonus = pl.pallas_call(
        _wkv_prep_kernel,
        grid=(bsz, s // rows),
        in_specs=[blk] * 6 + [pl.BlockSpec((1, d), lambda b, c: (0, 0))],
        out_specs=[blk, blk, blk, blk, dec_blk, blk],
        out_shape=[act(BF16), act(F32), act(BF16), act(F32),
                   jax.ShapeDtypeStruct((bsz, 8 * s // cs, d), F32), act(F32)],
        compiler_params=_cparams("parallel", "parallel"),
        name="wkv7_prep",
    )(r, lw, k, v, kk, ab, rk.reshape(1, d))
    rows = scan_chunks * cs
    blk = pl.BlockSpec((bsz, rows, d), lambda c: (0, c, 0))
    dec_blk = pl.BlockSpec((bsz, 8 * scan_chunks, d), lambda c: (0, c, 0))
    row = pl.BlockSpec((1, d), lambda c: (0, 0))
    return pl.pallas_call(
        functools.partial(_wkv_out_kernel, final_norm),
        grid=(s // rows,),
        in_specs=[blk, blk, blk, blk, dec_blk, blk, blk, blk, row, row]
                 + [_weight_spec(a) for a in (wo,) + ffn] + [row],
        out_specs=blk,
        out_shape=act(F32),
        scratch_shapes=[pltpu.VMEM((bsz, HEAD_DIM, d), F32), pltpu.VMEM((bsz * rows, d), F32),
                        pltpu.VMEM((bsz * rows, d), F32)],
        compiler_params=_cparams("arbitrary"),
        name="wkv7_out_ffn",
    )(q, y0, mc, z, dec, x, bonus, g, lnw.reshape(1, d), lnb.reshape(1, d), *map(_weight_arg, (wo,) + ffn),
      g_last.reshape(1, d))


def kernel(x, positions, norm_mix, norm_ffn, norm_final, attn_w_in, attn_b_in, attn_sinks, attn_w_out, rwkv_mu, rwkv_w_rkv, rwkv_w0, rwkv_w1, rwkv_w2, rwkv_a0, rwkv_a1, rwkv_a2, rwkv_g1, rwkv_g2, rwkv_k_k, rwkv_k_a, rwkv_r_k, rwkv_lnx_w, rwkv_lnx_b, rwkv_w_o, rwkv_v0, rwkv_v1, rwkv_v2, ffn_w_gate, ffn_w_up, ffn_w_down):
    depth = norm_mix.shape[0]
    assert depth % 2 == 0, "the final norm is fused into the last RWKV layer"
    assert all(win // dil == BLOCK for win, dil in B_PATTERNS), "mixer B bands must span one block"
    bf = lambda a: a.astype(BF16)
    wg_all, wu_all, wd_all = bf(ffn_w_gate), bf(ffn_w_up), bf(ffn_w_down)
    w_in_all, w_out_all, w_rkv_all, w_o_all = bf(attn_w_in), bf(attn_w_out), bf(rwkv_w_rkv), bf(rwkv_w_o)
    tables = _rope_tables(positions)
    v_first = None
    for layer in range(depth):
        i = layer // 2
        ffn = (norm_ffn[layer].reshape(1, -1), (wg_all, (layer,)), (wu_all, (layer,)), (wd_all, (layer,)))
        if layer % 2 == 0:
            qa, kva, qkv_b = _attn_in(x, norm_mix[layer], (w_in_all, (i,)), attn_b_in[i], tables)
            oa = _attn_a(qa, kva, attn_sinks[i])
            obs, lses = zip(*[_attn_b(*qkv_b[pat], pat, dil) for pat, (_, dil) in enumerate(B_PATTERNS)])
            x = _attn_out(x, oa, obs, lses, (w_out_all, (i,)), ffn)
        else:
            p = dict(mu=rwkv_mu[i], w_r=(w_rkv_all, (i, 0)), w_k=(w_rkv_all, (i, 1)), w_v=(w_rkv_all, (i, 2)),
                     w0=rwkv_w0[i], w1=bf(rwkv_w1[i]), w2=bf(rwkv_w2[i]),
                     a0=rwkv_a0[i], a1=bf(rwkv_a1[i]), a2=bf(rwkv_a2[i]), g1=bf(rwkv_g1[i]), g2=bf(rwkv_g2[i]),
                     k_k=rwkv_k_k[i], k_a=rwkv_k_a[i])
            v_lora = None if i == 0 else (rwkv_v0[i - 1], bf(rwkv_v1[i - 1]), bf(rwkv_v2[i - 1]))
            r, lw, k, v, kk, ab, g = _rwkv_in(x, norm_mix[layer], p, v_lora, v_first)
            if i == 0:
                v_first = v
            x = _wkv(x, r, lw, k, v, kk, ab, g, rwkv_r_k[i], rwkv_lnx_w[i], rwkv_lnx_b[i], (w_o_all, (i,)),
                     ffn, norm_final, final_norm=(layer == depth - 1))
    return x
```

```python
import functools

import jax
import jax.numpy as jnp
from jax import lax
from jax.experimental import pallas as pl
from jax.experimental.pallas import tpu as pltpu

F32 = jnp.float32
BF16 = jnp.bfloat16

HEAD_DIM = 64
ROT_DIM = HEAD_DIM // 4
ROT_HALF = ROT_DIM // 2
ROPE_THETA = 500000.0
BLOCK = 128
NORM_EPS = 1e-5
LNX_EPS = 64e-5

A_Q_HEADS = 12
A_KV_HEADS = 3
A_GROUP = A_Q_HEADS // A_KV_HEADS
A_WINDOW = 128
B_PATTERNS = ((128, 1), (512, 4), (2048, 16))
B_HEADS = 4

A_Q_W = A_Q_HEADS * HEAD_DIM
A_KV_W = A_KV_HEADS * HEAD_DIM
B_PW = B_HEADS * HEAD_DIM
B_W = len(B_PATTERNS) * B_PW
KV_DUP_W = 2 * A_KV_W
OFF_KA = A_Q_W
OFF_QB = OFF_KA + 2 * A_KV_W
OFF_KB = OFF_QB + B_W
OFF_VB = OFF_KB + B_W

LANES = 128
MXU_W = 256
WKV_CHUNK = 64
DECAY_SCALE = 0.6065306597126334
NEG_BIG = -1e30
VMEM_LIMIT = 56 * 1024 * 1024


def _cparams(*sem, fusible_inputs=0):
    fuse = [True] * fusible_inputs if fusible_inputs else None
    return pltpu.CompilerParams(dimension_semantics=sem, vmem_limit_bytes=VMEM_LIMIT, allow_input_fusion=fuse)


def _dot(a, b):
    return jnp.dot(a, b, preferred_element_type=F32)


def _dot_nt(a, b):
    return lax.dot_general(a, b, (((1,), (1,)), ((), ())), preferred_element_type=F32)


def _dot_tn(a, b):
    return lax.dot_general(a, b, (((0,), (0,)), ((), ())), preferred_element_type=F32)


def _row_sum(z):
    part = z[:, :LANES]
    for j in range(1, z.shape[1] // LANES):
        part = part + z[:, j * LANES:(j + 1) * LANES]
    ones = jnp.ones((LANES, LANES), BF16)
    hi = part.astype(BF16)
    lo = (part - hi.astype(F32)).astype(BF16)
    return _dot(hi, ones) + _dot(lo, ones)


def _rms(x, g):
    inv = lax.rsqrt(_row_sum(x * x) * (1.0 / x.shape[1]) + NORM_EPS)
    return x * jnp.concatenate([inv] * (x.shape[1] // LANES), axis=1) * g


def _sigmoid(z):
    return 1.0 / (1.0 + jnp.exp(-z))


def _head_mask(n):
    r = lax.broadcasted_iota(jnp.int32, (n, n), 0) // HEAD_DIM
    c = lax.broadcasted_iota(jnp.int32, (n, n), 1) // HEAD_DIM
    return r == c


def _head_sum(z, ones_bd, split=True):
    rows, n = z.shape[0], z.shape[1] // MXU_W
    zc = jnp.concatenate([z[:, j * MXU_W:(j + 1) * MXU_W] for j in range(n)], axis=0)
    hi = zc.astype(BF16)
    acc = _dot(hi, ones_bd)
    if split:
        acc = acc + _dot((zc - hi.astype(F32)).astype(BF16), ones_bd)
    return jnp.concatenate([acc[j * rows:(j + 1) * rows] for j in range(n)], axis=1)


def _rope_trig_kernel(pos_ref, invf_ref, cos_ref, sin_ref):
    ang = pos_ref[...].astype(F32)[None] * invf_ref[...]
    cos_ref[...] = jnp.cos(ang)
    sin_ref[...] = jnp.sin(ang)


def _rope_tables(positions):
    bsz, s = positions.shape
    rows = bsz * s // LANES
    inv_freq = jnp.power(ROPE_THETA, -2.0 * jnp.arange(ROT_HALF, dtype=F32) / ROT_DIM)
    invf = jnp.broadcast_to(inv_freq[:, None, None], (ROT_HALF, 1, LANES))
    cos, sin = pl.pallas_call(
        _rope_trig_kernel,
        out_shape=(jax.ShapeDtypeStruct((ROT_HALF, rows, LANES), F32),) * 2,
        name="rope_trig",
    )(positions.reshape(rows, LANES), invf)
    cos = jnp.tile(cos.reshape(ROT_HALF, bsz, s).transpose(1, 2, 0), (1, 1, LANES // ROT_HALF))
    sin = jnp.tile(sin.reshape(ROT_HALF, bsz, s).transpose(1, 2, 0), (1, 1, LANES // ROT_HALF))
    return cos, sin


def _residue_major(ref, dil, groups=1):
    span = ref.shape[0] // groups
    n = span // dil
    return jnp.concatenate([ref[pl.ds(g * span + r, n, stride=dil), :]
                            for r in range(dil) for g in range(groups)], axis=0)


def _attn_in_kernel(x_ref, g_ref, w_ref, b_ref, cos_ref, sin_ref, qa_ref, kva_ref, *rest):
    b_refs, hbuf = rest[:-1], rest[-1]
    hf = _rms(x_ref[...], g_ref[...])
    tm, d = hf.shape
    tiles = d // LANES
    for j in range(tiles):
        hbuf[j] = hf[:, j * LANES:(j + 1) * LANES]
    scale = HEAD_DIM ** -0.5

    def proj(h, lo, width):
        return _dot(h, w_ref[:, lo:lo + width]) + b_ref[:, lo:lo + width]

    dim = lax.broadcasted_iota(jnp.int32, (tm, LANES), 1) % HEAD_DIM

    def rope_tabs(cos, sin):
        return (jnp.where(dim < ROT_DIM, cos, 1.0), jnp.where(dim < ROT_HALF, -sin, 0.0),
                jnp.where((dim >= ROT_HALF) & (dim < ROT_DIM), sin, 0.0))

    def rope(z, tabs):
        c, sa, sb = tabs
        return z * c + pltpu.roll(z, LANES - ROT_HALF, 1) * sa + pltpu.roll(z, ROT_HALF, 1) * sb

    def rope_all(z, tabs):
        return jnp.concatenate([rope(z[:, j * LANES:(j + 1) * LANES], tabs)
                                for j in range(z.shape[1] // LANES)], axis=1)

    h = hf.astype(BF16)
    tabs = rope_tabs(cos_ref[...], sin_ref[...])
    for j in range(A_Q_W // MXU_W):
        qa_ref[:, j * MXU_W:(j + 1) * MXU_W] = (rope_all(proj(h, j * MXU_W, MXU_W), tabs) * scale).astype(BF16)
    z = proj(h, OFF_KA, 2 * A_KV_W)
    low = lax.broadcasted_iota(jnp.int32, (tm, LANES), 1) < HEAD_DIM
    mid = z[:, LANES:2 * LANES]
    src = [rope(z[:, :LANES], tabs), jnp.where(low, rope(mid, tabs), mid), z[:, 2 * LANES:]]
    for j, t in enumerate(src):
        swapped = pltpu.roll(t, HEAD_DIM, 1)
        kva_ref[:, 2 * j * LANES:(2 * j + 1) * LANES] = jnp.where(low, t, swapped).astype(BF16)
        kva_ref[:, (2 * j + 1) * LANES:(2 * j + 2) * LANES] = jnp.where(low, swapped, t).astype(BF16)
    prev_dil = 1
    for pat, (_, dil) in enumerate(B_PATTERNS):
        q_ref, k_ref, v_ref = b_refs[3 * pat:3 * pat + 3]
        if dil > 1:
            step = dil // prev_dil
            srcs = [hbuf.at[j] for j in range(tiles)] + [hbuf.at[tiles + j] for j in range(2)]
            if prev_dil == 1:
                srcs[tiles:] = [cos_ref, sin_ref]
            regrouped = [_residue_major(src, step, prev_dil) for src in srcs]
            for j, val in enumerate(regrouped):
                hbuf[j] = val
            h = jnp.concatenate(regrouped[:tiles], axis=1).astype(BF16)
            tabs = rope_tabs(*regrouped[tiles:])
            prev_dil = dil
        q = (rope_all(proj(h, OFF_QB + pat * B_PW, B_PW), tabs) * scale).astype(BF16)
        k = rope_all(proj(h, OFF_KB + pat * B_PW, B_PW), tabs).astype(BF16)
        v = proj(h, OFF_VB + pat * B_PW, B_PW).astype(BF16)
        n = tm // dil
        for r in range(dil):
            q_ref[:, r * B_PW:(r + 1) * B_PW] = q[r * n:(r + 1) * n]
            k_ref[:, r * B_PW:(r + 1) * B_PW] = k[r * n:(r + 1) * n]
            v_ref[:, r * B_PW:(r + 1) * B_PW] = v[r * n:(r + 1) * n]


def _attn_in(x, g, w_in, b_in, tables, tm=1024):
    bsz, s, d = x.shape
    in_w = b_in.shape[-1]
    tok = lambda w: pl.BlockSpec((None, tm, w), lambda b, i: (b, i, 0))
    full = _resident
    out_specs = [tok(A_Q_W), tok(2 * KV_DUP_W)]
    out_shape = [jax.ShapeDtypeStruct((bsz, s, A_Q_W), BF16), jax.ShapeDtypeStruct((bsz, s, 2 * KV_DUP_W), BF16)]
    for _, dil in B_PATTERNS:
        out_specs += [pl.BlockSpec((None, tm // dil, dil * B_PW), lambda b, i: (b, i, 0))] * 3
        out_shape += [jax.ShapeDtypeStruct((bsz, s // dil, dil * B_PW), BF16)] * 3
    outs = pl.pallas_call(
        _attn_in_kernel,
        grid=(bsz, s // tm),
        in_specs=[tok(d), full((1, d)), _weight_spec(w_in), full((1, in_w)),
                  tok(LANES), tok(LANES)],
        out_specs=out_specs,
        out_shape=out_shape,
        scratch_shapes=[pltpu.VMEM((d // LANES + 2, tm, LANES), F32)],
        compiler_params=_cparams("parallel", "parallel", fusible_inputs=6),
        name="attn_in",
    )(x, g.reshape(1, d), _weight_arg(w_in), b_in.reshape(1, in_w), *tables)
    return outs[0], outs[1], [outs[2 + 3 * p:5 + 3 * p] for p in range(len(B_PATTERNS))]


def _band_bias(rows, max_dist, first_tile):
    qi = lax.broadcasted_iota(jnp.int32, (rows, 2 * BLOCK), 0) % BLOCK
    kj = lax.broadcasted_iota(jnp.int32, (rows, 2 * BLOCK), 1)
    dist = BLOCK + qi - kj
    band = (dist >= 0) & (dist <= max_dist)
    inner = jnp.where(band, 0.0, NEG_BIG)
    return jnp.where(band & ((kj >= BLOCK) | jnp.logical_not(first_tile)), 0.0, NEG_BIG), inner


def _window_rows(prev_ref, cur_ref, jb, lanes):
    if jb == 0:
        return jnp.concatenate([prev_ref[:, lanes], cur_ref[0:BLOCK, lanes]], axis=0)
    return cur_ref[(jb - 1) * BLOCK:(jb + 1) * BLOCK, lanes]


def _attn_a_kernel(sink_ref, q_ref, kvp_ref, kvc_ref, o_ref):
    nb = q_ref.shape[0] // BLOCK
    rows = A_GROUP * BLOCK
    first_tile = pl.program_id(1) == 0
    qi = lax.broadcasted_iota(jnp.int32, (rows, BLOCK), 0) % BLOCK
    from_prev = lax.broadcasted_iota(jnp.int32, (rows, BLOCK), 1) > qi
    low = lax.broadcasted_iota(jnp.int32, (BLOCK, LANES), 1) < HEAD_DIM
    rowg = lax.broadcasted_iota(jnp.int32, (rows, 1), 0) // BLOCK
    sinks = []
    for hk in range(A_KV_HEADS):
        sk = jnp.zeros((rows, 1), F32)
        for g in range(A_GROUP):
            sk = jnp.where(rowg == g, sink_ref[hk * A_GROUP + g], sk)
        sinks.append(sk)
    kvh = range(A_KV_HEADS)
    for jb in range(nb):
        rs = slice(jb * BLOCK, (jb + 1) * BLOCK)
        sc = []
        for hk in kvh:
            tiles = [q_ref[rs, (2 * hk + t) * LANES:(2 * hk + t + 1) * LANES] for t in range(2)]
            zero = jnp.zeros_like(tiles[0])
            q4 = jnp.concatenate([jnp.where(low, tiles[0], zero), jnp.where(low, zero, tiles[0]),
                                  jnp.where(low, tiles[1], zero), jnp.where(low, zero, tiles[1])], axis=0)
            k = _window_rows(kvp_ref, kvc_ref, jb, slice(hk * LANES, (hk + 1) * LANES))
            both = _dot_nt(q4, k)
            prev = both[:, :BLOCK]
            if jb == 0:
                prev = jnp.where(first_tile, NEG_BIG, prev)
            sc.append(jnp.where(from_prev, prev, both[:, BLOCK:]))
        m = [jnp.max(s, axis=-1, keepdims=True) for s in sc]
        p = [jnp.exp(s - mm) for s, mm in zip(sc, m)]
        den = [jnp.sum(pp, axis=-1, keepdims=True) + jnp.exp(sk - mm) for pp, sk, mm in zip(p, sinks, m)]
        o = []
        for hk in kvh:
            v = _window_rows(kvp_ref, kvc_ref, jb, slice((A_KV_HEADS + hk) * LANES, (A_KV_HEADS + hk + 1) * LANES))
            pb = p[hk].astype(BF16)
            zero = jnp.zeros_like(pb)
            unfolded = jnp.concatenate([jnp.where(from_prev, pb, zero), jnp.where(from_prev, zero, pb)], axis=1)
            o.append(_dot(unfolded, v) * (1.0 / den[hk]))
        for hk in kvh:
            for t in range(2):
                pair = jnp.where(low, o[hk][2 * t * BLOCK:(2 * t + 1) * BLOCK], o[hk][(2 * t + 1) * BLOCK:(2 * t + 2) * BLOCK])
                o_ref[rs, (2 * hk + t) * LANES:(2 * hk + t + 1) * LANES] = pair.astype(BF16)


def _attn_a(qa, kva, sinks, tile=1024):
    assert A_WINDOW == BLOCK, "the score fold needs a window of exactly one block"
    bsz, s, _ = qa.shape
    tile = min(tile, s)
    kvw = kva.shape[2]
    per = tile // BLOCK
    return pl.pallas_call(
        _attn_a_kernel,
        grid=(bsz, s // tile),
        in_specs=[pl.BlockSpec(memory_space=pltpu.SMEM),
                  pl.BlockSpec((None, tile, A_Q_W), lambda b, i: (b, i, 0)),
                  pl.BlockSpec((None, BLOCK, kvw), lambda b, i: (b, jnp.maximum(i * per - 1, 0), 0)),
                  pl.BlockSpec((None, tile, kvw), lambda b, i: (b, i, 0))],
        out_specs=pl.BlockSpec((None, tile, A_Q_W), lambda b, i: (b, i, 0)),
        out_shape=jax.ShapeDtypeStruct((bsz, s, A_Q_W), BF16),
        compiler_params=_cparams("parallel", "parallel"),
        name="attn_a",
    )(sinks, qa, kva, kva)


def _attn_b_kernel(group, q_ref, kp_ref, kc_ref, vp_ref, vc_ref, o_ref, l_ref):
    nb = q_ref.shape[0] // BLOCK
    bias_first, bias_inner = _band_bias(BLOCK, BLOCK, pl.program_id(2) == 0)
    low = lax.broadcasted_iota(jnp.int32, (BLOCK, LANES), 1) < HEAD_DIM
    work = [(res, jb) for res in range(q_ref.shape[1] // B_PW) for jb in range(nb)]
    for w0 in range(0, len(work), group):
        items = [(res, jb, h) for res, jb in work[w0:w0 + group] for h in range(B_HEADS)]
        lanes = [slice(res * B_PW + (h // 2) * LANES, res * B_PW + (h // 2 + 1) * LANES) for res, _, h in items]
        sc = []
        for (_, jb, h), ls in zip(items, lanes):
            qt = q_ref[jb * BLOCK:(jb + 1) * BLOCK, ls]
            zero = jnp.zeros_like(qt)
            qh = jnp.where(low, qt, zero) if h % 2 == 0 else jnp.where(low, zero, qt)
            sc.append(_dot_nt(qh, _window_rows(kp_ref, kc_ref, jb, ls)) + (bias_first if jb == 0 else bias_inner))
        m = [jnp.max(s, axis=-1, keepdims=True) for s in sc]
        p = [jnp.exp(s - mm) for s, mm in zip(sc, m)]
        den = [jnp.sum(pp, axis=-1, keepdims=True) for pp in p]
        o = [_dot(pp.astype(BF16), _window_rows(vp_ref, vc_ref, jb, ls)) * (1.0 / dd)
             for pp, dd, (_, jb, _), ls in zip(p, den, items, lanes)]
        lse = [mm + jnp.log(dd) for mm, dd in zip(m, den)]
        for idx in range(0, len(items), 2):
            _, jb, _ = items[idx]
            dst = (slice(jb * BLOCK, (jb + 1) * BLOCK), lanes[idx])
            o_ref[dst] = jnp.where(low, o[idx], o[idx + 1])
            l_ref[dst] = jnp.where(low, lse[idx], lse[idx + 1])


def _attn_b(q, k, v, pat, dil, tile=1024, group=2):
    bsz, length, _ = q.shape
    res = max(1, min(tile // length, dil))
    tile = min(tile, length)
    per = tile // BLOCK
    cur = pl.BlockSpec((None, tile, res * B_PW), lambda b, r, i: (b, i, r))
    prev = pl.BlockSpec((None, BLOCK, res * B_PW), lambda b, r, i: (b, jnp.maximum(i * per - 1, 0), r))
    return pl.pallas_call(
        functools.partial(_attn_b_kernel, group),
        grid=(bsz, dil // res, length // tile),
        in_specs=[cur, prev, cur, prev, cur],
        out_specs=[cur, cur],
        out_shape=[jax.ShapeDtypeStruct((bsz, length, dil * B_PW), F32)] * 2,
        compiler_params=_cparams("parallel", "parallel", "parallel"),
        name=f"attn_b{pat}",
    )(q, k, k, v, v)


def _attn_out_kernel(x_ref, oa_ref, o0_ref, o1_ref, o2_ref, l0_ref, l1_ref, l2_ref, w_ref,
                     gf_ref, wg_ref, wu_ref, wd_ref, out_ref, buf, acc_ref):
    tm = x_ref.shape[0]

    def natural(ref, dil, slot):
        if dil == 1:
            return ref[...]
        n = tm // dil
        halves = B_PW // LANES
        for r in range(dil):
            for j in range(halves):
                buf[slot * halves + j, pl.ds(r, n, stride=dil), :] = ref[:, r * B_PW + j * LANES:r * B_PW + (j + 1) * LANES]
        return jnp.concatenate([buf[slot * halves + j] for j in range(halves)], axis=1)

    dils = [dil for _, dil in B_PATTERNS]
    o = [natural(ref, dil, i) for i, (ref, dil) in enumerate(zip((o0_ref, o1_ref, o2_ref), dils))]
    l = [natural(ref, dil, 3 + i) for i, (ref, dil) in enumerate(zip((l0_ref, l1_ref, l2_ref), dils))]
    m = jnp.maximum(jnp.maximum(l[0], l[1]), l[2])
    e = [jnp.exp(v - m) for v in l]
    ob = (e[0] * o[0] + e[1] * o[1] + e[2] * o[2]) / (e[0] + e[1] + e[2])
    mix = _dot(oa_ref[...], w_ref[:A_Q_W, :]) + _dot(ob.astype(BF16), w_ref[A_Q_W:, :])
    out_ref[...] = _ffn_block(x_ref[...] + mix, gf_ref, wg_ref, wu_ref, wd_ref, acc_ref)


def _attn_out(x, oa, obs, lses, w_out, ffn, tm=512):
    bsz, s, d = x.shape
    tok = lambda w: pl.BlockSpec((None, tm, w), lambda b, i: (b, i, 0))
    pat = [pl.BlockSpec((None, tm // dil, dil * B_PW), lambda b, i: (b, i, 0)) for _, dil in B_PATTERNS]
    return pl.pallas_call(
        _attn_out_kernel,
        grid=(bsz, s // tm),
        in_specs=[tok(d), tok(A_Q_W)] + pat + pat + [_weight_spec(a) for a in (w_out,) + ffn],
        out_specs=tok(d),
        out_shape=jax.ShapeDtypeStruct((bsz, s, d), F32),
        scratch_shapes=[pltpu.VMEM((2 * len(B_PATTERNS) * (B_PW // LANES), tm, LANES), F32),
                        pltpu.VMEM((tm, d), F32)],
        compiler_params=_cparams("parallel", "parallel", fusible_inputs=9 + len(ffn)),
        name="attn_out_ffn",
    )(x, oa, *obs, *lses, *map(_weight_arg, (w_out,) + ffn))


FF_CHUNK = 256


def _ffn_block(x, g_ref, wg_ref, wu_ref, wd_ref, acc_ref):
    h = _rms(x, g_ref[...]).astype(BF16)
    for c in range(wg_ref.shape[1] // FF_CHUNK):
        sl = slice(c * FF_CHUNK, (c + 1) * FF_CHUNK)
        gate = _dot(h, wg_ref[:, sl])
        up = _dot(h, wu_ref[:, sl])
        contrib = _dot((gate * _sigmoid(gate) * up).astype(BF16), wd_ref[sl, :])
        if c == 0:
            acc_ref[...] = x + contrib
        else:
            acc_ref[...] += contrib
    return acc_ref[...]


def _resident(shape):
    return pl.BlockSpec(shape, lambda *_: (0,) * len(shape), pipeline_mode=pl.Buffered(1))


def _weight_spec(w):
    if not isinstance(w, tuple):
        return _resident(w.shape)
    stack, idx = w
    rest = stack.shape[len(idx):]
    return pl.BlockSpec((None,) * len(idx) + rest, lambda *_: idx + (0,) * len(rest), pipeline_mode=pl.Buffered(1))


def _weight_arg(w):
    return w[0] if isinstance(w, tuple) else w


def _rwkv_in_kernel(has_vlora, *refs):
    (x_ref, gn_ref, mu_ref, wr_ref, wk_ref, wv_ref, w0_ref, w1_ref, w2_ref, a0_ref, a1_ref, a2_ref,
     g1_ref, g2_ref, kk_ref, ka_ref) = refs[:16]
    refs = refs[16:]
    if has_vlora:
        v0_ref, v1_ref, v2_ref, vf_ref = refs[:4]
        refs = refs[4:]
    r_o, lw_o, k_o, v_o, kk_o, ab_o, g_o, carry, mix = refs
    tm, d = x_ref.shape

    @pl.when(pl.program_id(1) == 0)
    def _():
        carry[...] = jnp.zeros(carry.shape, F32)

    h = _rms(x_ref[...], gn_ref[...])
    rolled = pltpu.roll(h, 1, 0)
    first = lax.broadcasted_iota(jnp.int32, (8, 1), 0) == 0
    hprev = jnp.concatenate([jnp.where(first, carry[7:8, :], rolled[0:8]), rolled[8:]], axis=0)
    carry[...] = h[tm - 8:tm, :]
    hb = h.astype(BF16)
    xxb = (hprev - h).astype(BF16)
    mub = mu_ref[...].astype(BF16)
    for i in range(mix.shape[0]):
        mix[i] = hb + xxb * mub[i:i + 1, :]

    mid_w = jnp.tanh(_dot(mix[1], w1_ref[...])).astype(BF16)
    mid_a = _dot(mix[4], a1_ref[...]).astype(BF16)
    mid_g = _sigmoid(_dot(mix[5], g1_ref[...])).astype(BF16)
    if has_vlora:
        mid_v = _dot(mix[3], v1_ref[...]).astype(BF16)

    ones_bd = jnp.where(_head_mask(MXU_W), 1.0, 0.0).astype(BF16)
    for c in range(d // MXU_W):
        cs = slice(c * MXU_W, (c + 1) * MXU_W)
        r_o[:, cs] = _dot(mix[0], wr_ref[:, cs])
        lw_o[:, cs] = -DECAY_SCALE * _sigmoid(w0_ref[:, cs] + _dot(mid_w, w2_ref[:, cs]))
        v = _dot(mix[3], wv_ref[:, cs])
        if has_vlora:
            v = v + (vf_ref[:, cs] - v) * _sigmoid(v0_ref[:, cs] + _dot(mid_v, v2_ref[:, cs]))
        v_o[:, cs] = v
        g_o[:, cs] = _dot(mid_g, g2_ref[:, cs]).astype(BF16)
        k = _dot(mix[2], wk_ref[:, cs])
        a = _sigmoid(a0_ref[:, cs] + _dot(mid_a, a2_ref[:, cs]))
        kk = k * kk_ref[:, cs]
        kk = kk / jnp.maximum(jnp.sqrt(_head_sum(kk * kk, ones_bd)), 1e-12)
        kk_o[:, cs] = kk
        ab_o[:, cs] = kk * a
        k_o[:, cs] = k * (1.0 + (a - 1.0) * ka_ref[:, cs])


def _rwkv_in(x, gn, p, v_lora, v_first, tm=512):
    bsz, s, d = x.shape
    tok = pl.BlockSpec((None, tm, d), lambda b, i: (b, i, 0))
    row = lambda a: a.reshape(1, d)
    args = [x, row(gn), p["mu"], p["w_r"], p["w_k"], p["w_v"], row(p["w0"]), p["w1"], p["w2"], row(p["a0"]), p["a1"], p["a2"],
            p["g1"], p["g2"], row(p["k_k"]), row(p["k_a"])]
    specs = [tok] + [_weight_spec(a) for a in args[1:]]
    if v_lora is not None:
        v0, v1, v2 = v_lora
        extra = [row(v0), v1, v2]
        args += extra + [v_first]
        specs += [_weight_spec(a) for a in extra] + [tok]
    return pl.pallas_call(
        functools.partial(_rwkv_in_kernel, v_lora is not None),
        grid=(bsz, s // tm),
        in_specs=specs,
        out_specs=[tok] * 7,
        out_shape=[jax.ShapeDtypeStruct((bsz, s, d), F32)] * 6 + [jax.ShapeDtypeStruct((bsz, s, d), BF16)],
        scratch_shapes=[pltpu.VMEM((8, d), F32), pltpu.VMEM((p["mu"].shape[0], tm, d), BF16)],
        compiler_params=_cparams("parallel", "arbitrary", fusible_inputs=len(args)),
        name="rwkv_in",
    )(*map(_weight_arg, args))


def _bd2(z):
    lane = lax.broadcasted_iota(jnp.int32, z.shape, 1)
    zero = jnp.zeros(z.shape, z.dtype)
    return jnp.concatenate([jnp.where(lane < HEAD_DIM, z, zero), jnp.where(lane >= HEAD_DIM, z, zero)], axis=0)


def _fold2(full):
    lane = lax.broadcasted_iota(jnp.int32, (HEAD_DIM, LANES), 1)
    return jnp.where(lane < HEAD_DIM, full[:HEAD_DIM], full[HEAD_DIM:])


def _wkv_prep_kernel(r_ref, lw_ref, k_ref, v_ref, kk_ref, ab_ref, rk_ref,
                     q_ref, y0_ref, mc_ref, z_ref, dec_ref, bonus_ref):
    cs = WKV_CHUNK
    rows, d = r_ref.shape
    nc = rows // cs
    ri = lax.broadcasted_iota(jnp.int32, (rows, rows), 0)
    ci = lax.broadcasted_iota(jnp.int32, (rows, rows), 1)
    tri = jnp.where((ri >= ci) & (ri // cs == ci // cs), 1.0, 0.0).astype(BF16)
    lw = lw_ref[...]
    hi = lw.astype(BF16)
    lo = (lw - hi.astype(F32)).astype(BF16)
    cum = _dot(tri, hi) + _dot(tri, lo)
    totals = [cum[(j + 1) * cs - 1:(j + 1) * cs, :] for j in range(nc)]
    tot = jnp.concatenate([jnp.broadcast_to(t, (cs, d)) for t in totals], axis=0)
    e_neg = jnp.exp(-cum)
    e_tot = jnp.exp(tot - cum)
    kk, ab, kx = kk_ref[...], ab_ref[...], k_ref[...]
    at = (-kk * jnp.exp(cum - lw)).astype(BF16)
    rt = r_ref[...] * jnp.exp(cum)
    rtb = rt.astype(BF16)
    bt = (ab * e_neg).astype(BF16)
    kt = (kx * e_neg).astype(BF16)
    bh = (ab * e_tot).astype(BF16)
    kh = (kx * e_tot).astype(BF16)
    vb = v_ref[...].astype(BF16)
    ones_bd = jnp.where(_head_mask(MXU_W), 1.0, 0.0).astype(BF16)
    bonus_ref[...] = _head_sum(r_ref[...] * kx * rk_ref[...], ones_bd, split=False) * v_ref[...]

    trow = lax.broadcasted_iota(jnp.int32, (cs, LANES), 0)
    tcol = lax.broadcasted_iota(jnp.int32, (cs, LANES), 1) % HEAD_DIM
    strict = trow > tcol
    incl = trow >= tcol
    eye = jnp.where(trow == tcol, 1.0, 0.0)

    for j in range(nc):
        dec_ref[8 * j:8 * j + 8, :] = jnp.broadcast_to(jnp.exp(totals[j]), (8, d))

    chains = [(slice(j * cs, (j + 1) * cs), slice(p * LANES, (p + 1) * LANES))
              for j in range(nc) for p in range(d // LANES)]

    g12 = [_dot_nt(jnp.concatenate([at[c], rtb[c]], axis=0),
                   jnp.concatenate([_bd2(bt[c]), _bd2(kt[c])], axis=0)) for c in chains]
    a_ab = [jnp.where(strict, g[:cs, :LANES], 0.0) for g in g12]
    a_rb = [jnp.where(incl, g[cs:, :LANES], 0.0).astype(BF16) for g in g12]
    a_k = [jnp.concatenate([jnp.where(strict, g[:cs, LANES:], 0.0),
                            jnp.where(incl, g[cs:, LANES:], 0.0)], axis=0).astype(BF16) for g in g12]
    g4 = [_dot(a, _bd2(vb[c])) for a, c in zip(a_k, chains)]
    x = [eye + jnp.where((trow - tcol == 1) & (trow % 2 == 1), a, 0.0) for a in a_ab]
    size = 2
    while size < cs:
        lower_left = (trow // size - tcol // size == 1) & ((trow // size) % 2 == 1)
        xb = [v.astype(BF16) for v in x]
        xn = [_dot(v, _bd2(jnp.where(lower_left, a, 0.0).astype(BF16))) for v, a in zip(xb, a_ab)]
        x = [v + _dot(n.astype(BF16), _bd2(vb16)) for v, n, vb16 in zip(x, xn, xb)]
        size *= 2
    wu = [_dot(v.astype(BF16), jnp.concatenate([_bd2(at[c]), _bd2(g[:cs].astype(BF16))], axis=1))
          for v, g, c in zip(x, g4, chains)]
    wb = [v[:, :LANES].astype(BF16) for v in wu]
    ub = [v[:, LANES:].astype(BF16) for v in wu]
    qy = [_dot(a, jnp.concatenate([_bd2(w), _bd2(u)], axis=1)) for a, w, u in zip(a_rb, wb, ub)]
    mc = [_dot_tn(w, bh[c]) for w, c in zip(wb, chains)]
    zz = [_dot_tn(jnp.concatenate([u, vb[c]], axis=0), jnp.concatenate([bh[c], kh[c]], axis=0))
          for u, c in zip(ub, chains)]
    for i, c in enumerate(chains):
        q_ref[c] = (rt[c] + qy[i][:, :LANES]).astype(BF16)
        y0_ref[c] = g4[i][cs:] + qy[i][:, LANES:]
        mc_ref[c] = _fold2(mc[i]).astype(BF16)
        z_ref[c] = _fold2(zz[i])


def _wkv_out_kernel(final_norm, q_ref, y0_ref, mc_ref, z_ref, dec_ref, x_ref, bonus_ref, g_ref, lnw_ref, lnb_ref,
                    wo_ref, gf_ref, wg_ref, wu_ref, wd_ref, gl_ref, out_ref, s_ref, y_buf, acc_ref):
    cs = WKV_CHUNK
    bsz, rows, d = q_ref.shape

    @pl.when(pl.program_id(0) == 0)
    def _():
        s_ref[...] = jnp.zeros(s_ref.shape, F32)

    seqs = [(b, slice(p * LANES, (p + 1) * LANES)) for b in range(bsz) for p in range(d // LANES)]
    state = [s_ref[b, :, ls] for b, ls in seqs]
    for j in range(rows // cs):
        rs = slice(j * cs, (j + 1) * cs)
        sb = [s.astype(BF16) for s in state]
        upd = [_dot(v, _bd2(mc_ref[b, rs, ls])) for v, (b, ls) in zip(sb, seqs)]
        for v, (b, ls) in zip(sb, seqs):
            y_buf[b * rows + j * cs:b * rows + (j + 1) * cs, ls] = _dot_nt(q_ref[b, rs, ls], _bd2(v)) + y0_ref[b, rs, ls]
        state = [s * dec_ref[b, 8 * j:8 * j + 1, ls] + u + z_ref[b, rs, ls]
                 for s, u, (b, ls) in zip(state, upd, seqs)]
    for s, (b, ls) in zip(state, seqs):
        s_ref[b, :, ls] = s

    ones_bd = jnp.where(_head_mask(MXU_W), 1.0, 0.0).astype(BF16)
    y = y_buf[...]
    dev = y - _head_sum(y, ones_bd, split=False) * (1.0 / HEAD_DIM)
    var = _head_sum(dev * dev, ones_bd, split=False) * (1.0 / HEAD_DIM)
    yn = dev * lax.rsqrt(var + LNX_EPS) * lnw_ref[...] + lnb_ref[...]
    gated = jnp.concatenate([((yn[b * rows:(b + 1) * rows] + bonus_ref[b]) * g_ref[b]).astype(BF16)
                             for b in range(bsz)], axis=0)
    x1 = jnp.concatenate([x_ref[b] for b in range(bsz)], axis=0) + _dot(gated, wo_ref[...])
    out = _ffn_block(x1, gf_ref, wg_ref, wu_ref, wd_ref, acc_ref)
    if final_norm:
        out = _rms(out, gl_ref[...])
    for b in range(bsz):
        out_ref[b] = out[b * rows:(b + 1) * rows]


def _wkv(x, r, lw, k, v, kk, ab, g, rk, lnw, lnb, wo, ffn, g_last, final_norm, prep_chunks=4, scan_chunks=2):
    bsz, s, d = r.shape
    cs = WKV_CHUNK
    rows = prep_chunks * cs
    blk = pl.BlockSpec((None, rows, d), lambda b, c: (b, c, 0))
    dec_blk = pl.BlockSpec((None, 8 * prep_chunks, d), lambda b, c: (b, c, 0))
    act = lambda dt: jax.ShapeDtypeStruct((bsz, s, d), dt)
    q, y0, mc, z, dec, bonus = pl.pallas_call(
        _wkv_prep_kernel,
        grid=(bsz, s // rows),
        in_specs=[blk] * 6 + [pl.BlockSpec((1, d), lambda b, c: (0, 0))],
        out_specs=[blk, blk, blk, blk, dec_blk, blk],
        out_shape=[act(BF16), act(F32), act(BF16), act(F32),
                   jax.ShapeDtypeStruct((bsz, 8 * s // cs, d), F32), act(F32)],
        compiler_params=_cparams("parallel", "parallel"),
        name="wkv7_prep",
    )(r, lw, k, v, kk, ab, rk.reshape(1, d))
    rows = scan_chunks * cs
    blk = pl.BlockSpec((bsz, rows, d), lambda c: (0, c, 0))
    dec_blk = pl.BlockSpec((bsz, 8 * scan_chunks, d), lambda c: (0, c, 0))
    row = pl.BlockSpec((1, d), lambda c: (0, 0))
    return pl.pallas_call(
        functools.partial(_wkv_out_kernel, final_norm),
        grid=(s // rows,),
        in_specs=[blk, blk, blk, blk, dec_blk, blk, blk, blk, row, row]
                 + [_weight_spec(a) for a in (wo,) + ffn] + [row],
        out_specs=blk,
        out_shape=act(F32),
        scratch_shapes=[pltpu.VMEM((bsz, HEAD_DIM, d), F32), pltpu.VMEM((bsz * rows, d), F32),
                        pltpu.VMEM((bsz * rows, d), F32)],
        compiler_params=_cparams("arbitrary", fusible_inputs=12 + len(ffn)),
        name="wkv7_out_ffn",
    )(q, y0, mc, z, dec, x, bonus, g, lnw.reshape(1, d), lnb.reshape(1, d), *map(_weight_arg, (wo,) + ffn),
      g_last.reshape(1, d))


def kernel(x, positions, norm_mix, norm_ffn, norm_final, attn_w_in, attn_b_in, attn_sinks, attn_w_out, rwkv_mu, rwkv_w_rkv, rwkv_w0, rwkv_w1, rwkv_w2, rwkv_a0, rwkv_a1, rwkv_a2, rwkv_g1, rwkv_g2, rwkv_k_k, rwkv_k_a, rwkv_r_k, rwkv_lnx_w, rwkv_lnx_b, rwkv_w_o, rwkv_v0, rwkv_v1, rwkv_v2, ffn_w_gate, ffn_w_up, ffn_w_down):
    depth = norm_mix.shape[0]
    assert depth % 2 == 0, "the final norm is fused into the last RWKV layer"
    assert all(win // dil == BLOCK for win, dil in B_PATTERNS), "mixer B bands must span one block"
    bf = lambda a: a.astype(BF16)
    wg_all, wu_all, wd_all = bf(ffn_w_gate), bf(ffn_w_up), bf(ffn_w_down)
    w_in_all, w_out_all, w_rkv_all, w_o_all = bf(attn_w_in), bf(attn_w_out), bf(rwkv_w_rkv), bf(rwkv_w_o)
    tables = _rope_tables(positions)
    v_first = None
    for layer in range(depth):
        i = layer // 2
        ffn = (norm_ffn[layer].reshape(1, -1), (wg_all, (layer,)), (wu_all, (layer,)), (wd_all, (layer,)))
        if layer % 2 == 0:
            qa, kva, qkv_b = _attn_in(x, norm_mix[layer], (w_in_all, (i,)), attn_b_in[i], tables)
            oa = _attn_a(qa, kva, attn_sinks[i])
            obs, lses = zip(*[_attn_b(*qkv_b[pat], pat, dil) for pat, (_, dil) in enumerate(B_PATTERNS)])
            x = _attn_out(x, oa, obs, lses, (w_out_all, (i,)), ffn)
        else:
            p = dict(mu=rwkv_mu[i], w_r=(w_rkv_all, (i, 0)), w_k=(w_rkv_all, (i, 1)), w_v=(w_rkv_all, (i, 2)),
                     w0=rwkv_w0[i], w1=bf(rwkv_w1[i]), w2=bf(rwkv_w2[i]),
                     a0=rwkv_a0[i], a1=bf(rwkv_a1[i]), a2=bf(rwkv_a2[i]), g1=bf(rwkv_g1[i]), g2=bf(rwkv_g2[i]),
                     k_k=rwkv_k_k[i], k_a=rwkv_k_a[i])
            v_lora = None if i == 0 else (rwkv_v0[i - 1], bf(rwkv_v1[i - 1]), bf(rwkv_v2[i - 1]))
            r, lw, k, v, kk, ab, g = _rwkv_in(x, norm_mix[layer], p, v_lora, v_first)
            if i == 0:
                v_first = v
            x = _wkv(x, r, lw, k, v, kk, ab, g, rwkv_r_k[i], rwkv_lnx_w[i], rwkv_lnx_b[i], (w_o_all, (i,)),
                     ffn, norm_final, final_norm=(layer == depth - 1))
    return x
```
